```python
import math
import jax, jax.numpy as jnp
from jax import lax
import numpy as np

D_MODEL = 2048
BATCH = 4
SEQ = 2048
DEPTH = 2
DEC_BATCH = 128
DEC_SEQ = 4
PAST_LEN = 16384
PAGE_SIZE = 128

SSD_HEAD_DIM = 64
SSD_INNER = D_MODEL
SSD_HEADS = SSD_INNER // SSD_HEAD_DIM
SSD_GROUPS = 4
SSD_HEADS_PER_GROUP = SSD_HEADS // SSD_GROUPS
SSD_STATE = 128
SSD_CONV = 4
SSD_CONV_CH = SSD_INNER + 2 * SSD_GROUPS * SSD_STATE
SSD_CHUNK = 128
CONF_CH = D_MODEL
CONF_KERNEL = 31
N_GROUPS = 4
EXPERTS_PER_GROUP = 8
N_EXPERTS = N_GROUPS * EXPERTS_PER_GROUP
TOP_K_IN_GROUP = 2
D_FF_EXPERT = D_MODEL // 8
IN_SPLITS = (SSD_INNER, SSD_CONV_CH, SSD_HEADS, 2 * CONF_CH, D_MODEL, D_MODEL)
IN_COLS = sum(IN_SPLITS)
EPS = 1e-6

kernel_name = "hybrid_ssd_conformer_hmoe_adaln_step"


def _rmsnorm(x, g):
    x32 = x.astype(jnp.float32)
    y = x32 * lax.rsqrt(jnp.mean(x32 * x32, axis=-1, keepdims=True) + EPS)
    return (y * g.astype(jnp.float32)).astype(x.dtype)


def _layernorm(x, g, b):
    x32 = x.astype(jnp.float32)
    mu = jnp.mean(x32, axis=-1, keepdims=True)
    var = jnp.mean(jnp.square(x32 - mu), axis=-1, keepdims=True)
    y = (x32 - mu) * lax.rsqrt(var + EPS)
    return (y * g.astype(jnp.float32) + b.astype(jnp.float32)).astype(x.dtype)


def _split(a, sizes):
    idx = np.cumsum(sizes)[:-1].tolist()
    return jnp.split(a, idx, axis=-1)


def _causal_dwconv(u, prev, w, b):
    k = w.shape[0]
    padded = jnp.concatenate([prev.astype(u.dtype), u], axis=1)
    out = lax.conv_general_dilated(
        padded, w[:, None, :].astype(u.dtype), window_strides=(1,), padding='VALID',
        dimension_numbers=('NWC', 'WIO', 'NWC'), feature_group_count=u.shape[-1])
    return out + b.astype(u.dtype), padded[:, padded.shape[1] - (k - 1):]


def _ssd(x, dt, a, bm, cm, s0):
    f32 = jnp.float32
    dtype = x.dtype
    bsz, t, g, k, p = x.shape
    n = bm.shape[-1]
    L = SSD_CHUNK if t % SSD_CHUNK == 0 else t
    nc = t // L
    x = x.astype(f32).reshape(bsz, nc, L, g, k, p)
    dt = dt.astype(f32).reshape(bsz, nc, L, g, k)
    bm = bm.astype(f32).reshape(bsz, nc, L, g, n)
    cm = cm.astype(f32).reshape(bsz, nc, L, g, n)
    acum = jnp.cumsum(dt * a.astype(f32), axis=2)
    causal = jnp.tril(jnp.ones((L, L), dtype=bool))[None, None, :, :, None, None]
    seg = acum[:, :, :, None] - acum[:, :, None, :]
    decay = jnp.exp(jnp.where(causal, seg, -jnp.inf))
    cb = jnp.einsum('bcign,bcjgn->bcijg', cm, bm)
    m = cb[..., None] * decay * dt[:, :, None]
    y = jnp.einsum('bcijgk,bcjgkp->bcigkp', m, x)
    w_end = jnp.exp(acum[:, :, -1:] - acum) * dt
    local = jnp.einsum('bclgn,bclgk,bclgkp->bcgkpn', bm, w_end, x)
    chunk_decay = jnp.exp(acum[:, :, -1])

    def step(s, inp):
        dec, loc = inp
        return dec[..., None, None] * s + loc, s

    s_last, s_prev = lax.scan(step, s0.astype(f32),
                              (jnp.moveaxis(chunk_decay, 1, 0), jnp.moveaxis(local, 1, 0)))
    s_prev = jnp.moveaxis(s_prev, 0, 1)
    y = y + jnp.einsum('bclgn,bcgkpn->bclgkp', cm, s_prev) * jnp.exp(acum)[..., None]
    return y.reshape(bsz, t, g, k, p).astype(dtype), s_last.astype(dtype)


def _hier_moe(h, w_gr, b_gr, w_er, b_er, w_gate, w_up, w_down):
    f32 = jnp.float32
    bsz, t, d = h.shape
    hf = h.reshape(bsz * t, d)
    g_prob = jax.nn.softmax((hf @ w_gr + b_gr).astype(f32), axis=-1)
    g_top, g_idx = lax.top_k(g_prob, 1)
    e_logit = (hf @ w_er + b_er).astype(f32).reshape(-1, N_GROUPS, EXPERTS_PER_GROUP)
    e_logit = jnp.take_along_axis(e_logit, g_idx[:, :, None], axis=1)[:, 0]
    e_top, e_idx = lax.top_k(jax.nn.softmax(e_logit, axis=-1), TOP_K_IN_GROUP)
    e_top = e_top / jnp.sum(e_top, axis=-1, keepdims=True)
    within = jnp.sum(jax.nn.one_hot(e_idx, EXPERTS_PER_GROUP, dtype=f32) * e_top[..., None], axis=1)
    gate = (jax.nn.one_hot(g_idx[:, 0], N_GROUPS, dtype=f32) * g_top)[:, :, None] * within[:, None, :]
    gate = gate.reshape(-1, N_EXPERTS).astype(h.dtype)
    hidden = jax.nn.silu(jnp.einsum('td,edf->tef', hf, w_gate)) * jnp.einsum('td,edf->tef', hf, w_up)
    out = jnp.einsum('tef,efd->td', hidden * gate[:, :, None], w_down)
    return out.reshape(bsz, t, d)


def _layer(x, c, ssm0, sconv0, cconv0, norm_mix_g, norm_ffn_g, w_ada, b_ada, w_in,
           ssd_conv_w, ssd_conv_b, ssd_dt_bias, ssd_a_log, ssd_d, ssd_norm_g, w_ssd_out,
           conf_conv_w, conf_conv_b, conf_ln_g, conf_ln_b, w_conf_out, w_o,
           w_group_router, b_group_router, w_expert_router, b_expert_router,
           w_exp_gate, w_exp_up, w_exp_down):
    bsz, t, _ = x.shape
    G, K, P, N = SSD_GROUPS, SSD_HEADS_PER_GROUP, SSD_HEAD_DIM, SSD_STATE
    mod = jax.nn.silu(c) @ w_ada + b_ada
    sh1, sc1, g1, sh2, sc2, g2 = jnp.split(mod[:, None, :], 6, axis=-1)
    h = _rmsnorm(x, norm_mix_g) * (1 + sc1) + sh1
    z, xbc, dt_raw, glu_in, gate_ssd, gate_conf = _split(h @ w_in, IN_SPLITS)
    xbc, sconv_new = _causal_dwconv(xbc, sconv0, ssd_conv_w, ssd_conv_b)
    xbc = jax.nn.silu(xbc)
    xs, bm, cm = _split(xbc, (SSD_INNER, G * N, G * N))
    xs = xs.reshape(bsz, t, G, K, P)
    bm = bm.reshape(bsz, t, G, N)
    cm = cm.reshape(bsz, t, G, N)
    dt = jax.nn.softplus(dt_raw.astype(jnp.float32) + ssd_dt_bias.astype(jnp.float32)).reshape(bsz, t, G, K)
    a = -jnp.exp(ssd_a_log.astype(jnp.float32)).reshape(G, K)
    y, s_last = _ssd(xs, dt, a, bm, cm, ssm0.reshape(bsz, G, K, P, N))
    y = y + ssd_d.reshape(G, K, 1).astype(y.dtype) * xs
    y = y.reshape(bsz, t, G, K * P) * jax.nn.silu(z.reshape(bsz, t, G, K * P))
    y = _rmsnorm(y, ssd_norm_g.reshape(G, K * P))
    ssd_branch = y.reshape(bsz, t, SSD_INNER) @ w_ssd_out
    u_a, u_g = jnp.split(glu_in, 2, axis=-1)
    u = u_a * jax.nn.sigmoid(u_g)
    v, cconv_new = _causal_dwconv(u, cconv0, conf_conv_w, conf_conv_b)
    v = jax.nn.silu(_layernorm(v, conf_ln_g, conf_ln_b))
    conf_branch = v @ w_conf_out
    mixed = (jax.nn.sigmoid(gate_ssd) * ssd_branch + jax.nn.sigmoid(gate_conf) * conf_branch) @ w_o
    x = x + g1 * mixed
    h2 = _rmsnorm(x, norm_ffn_g) * (1 + sc2) + sh2
    x = x + g2 * _hier_moe(h2, w_group_router, b_group_router, w_expert_router, b_expert_router,
                           w_exp_gate, w_exp_up, w_exp_down)
    return x, s_last.reshape(bsz, SSD_HEADS, P, N), sconv_new, cconv_new


def _trunk(x, c, ssm, sconv, cconv, weights, final_norm_g):
    ssm_out, sconv_out, cconv_out = [], [], []
    for l in range(DEPTH):
        x, s1, s2, s3 = _layer(x, c, ssm[l], sconv[l], cconv[l], *[w[l] for w in weights])
        ssm_out.append(s1)
        sconv_out.append(s2)
        cconv_out.append(s3)
    return (_rmsnorm(x, final_norm_g), jnp.stack(ssm_out), jnp.stack(sconv_out), jnp.stack(cconv_out))


def setup_inputs(seed: int = 0) -> dict:
    key = jax.random.key(seed)
    ks = iter(jax.random.split(key, 40))

    def nrm(shape, scale):
        return jax.random.normal(next(ks), shape, jnp.float32) * scale

    dt0 = jnp.exp(jax.random.uniform(next(ks), (DEPTH, SSD_HEADS), jnp.float32,
                                     minval=math.log(1e-3), maxval=math.log(1e-1)))
    inp = {
        "x_prompt": nrm((BATCH, SEQ, D_MODEL), 1.0),
        "x_sample": nrm((DEC_BATCH, DEC_SEQ, D_MODEL), 1.0),
        "c_prompt": nrm((BATCH, D_MODEL), 1.0),
        "c_sample": nrm((DEC_BATCH, D_MODEL), 1.0),
        "state_ssm": nrm((DEPTH, DEC_BATCH, SSD_HEADS, SSD_HEAD_DIM, SSD_STATE), 0.1),
        "state_ssd_conv": nrm((DEPTH, DEC_BATCH, SSD_CONV - 1, SSD_CONV_CH), 1.0),
        "state_conf_conv": nrm((DEPTH, DEC_BATCH, CONF_KERNEL - 1, CONF_CH), 0.5),
        "norm_mix_g": 1.0 + nrm((DEPTH, D_MODEL), 0.02),
        "norm_ffn_g": 1.0 + nrm((DEPTH, D_MODEL), 0.02),
        "w_ada": nrm((DEPTH, D_MODEL, 6 * D_MODEL), 0.5 * D_MODEL ** -0.5),
        "b_ada": nrm((DEPTH, 6 * D_MODEL), 0.02),
        "w_in": nrm((DEPTH, D_MODEL, IN_COLS), D_MODEL ** -0.5),
        "ssd_conv_w": nrm((DEPTH, SSD_CONV, SSD_CONV_CH), SSD_CONV ** -0.5),
        "ssd_conv_b": nrm((DEPTH, SSD_CONV_CH), 0.02),
        "ssd_dt_bias": dt0 + jnp.log(-jnp.expm1(-dt0)),
        "ssd_a_log": jnp.log(jax.random.uniform(next(ks), (DEPTH, SSD_HEADS), jnp.float32, minval=1.0, maxval=16.0)),
        "ssd_d": 1.0 + nrm((DEPTH, SSD_HEADS), 0.1),
        "ssd_norm_g": 1.0 + nrm((DEPTH, SSD_INNER), 0.02),
        "w_ssd_out": nrm((DEPTH, SSD_INNER, D_MODEL), SSD_INNER ** -0.5),
        "conf_conv_w": nrm((DEPTH, CONF_KERNEL, CONF_CH), CONF_KERNEL ** -0.5),
        "conf_conv_b": nrm((DEPTH, CONF_CH), 0.02),
        "conf_ln_g": 1.0 + nrm((DEPTH, CONF_CH), 0.02),
        "conf_ln_b": nrm((DEPTH, CONF_CH), 0.02),
        "w_conf_out": nrm((DEPTH, CONF_CH, D_MODEL), CONF_CH ** -0.5),
        "w_o": nrm((DEPTH, D_MODEL, D_MODEL), D_MODEL ** -0.5),
        "w_group_router": nrm((DEPTH, D_MODEL, N_GROUPS), D_MODEL ** -0.5),
        "b_group_router": nrm((DEPTH, N_GROUPS), 0.01),
        "w_expert_router": nrm((DEPTH, D_MODEL, N_EXPERTS), D_MODEL ** -0.5),
        "b_expert_router": nrm((DEPTH, N_EXPERTS), 0.01),
        "w_exp_gate": nrm((DEPTH, N_EXPERTS, D_MODEL, D_FF_EXPERT), D_MODEL ** -0.5),
        "w_exp_up": nrm((DEPTH, N_EXPERTS, D_MODEL, D_FF_EXPERT), D_MODEL ** -0.5),
        "w_exp_down": nrm((DEPTH, N_EXPERTS, D_FF_EXPERT, D_MODEL), D_FF_EXPERT ** -0.5),
        "final_norm_g": 1.0 + nrm((D_MODEL,), 0.02),
    }
    return inp


def reference(x_prompt, x_sample, c_prompt, c_sample, state_ssm, state_ssd_conv, state_conf_conv,
              norm_mix_g, norm_ffn_g, w_ada, b_ada, w_in, ssd_conv_w, ssd_conv_b, ssd_dt_bias,
              ssd_a_log, ssd_d, ssd_norm_g, w_ssd_out, conf_conv_w, conf_conv_b, conf_ln_g,
              conf_ln_b, w_conf_out, w_o, w_group_router, b_group_router, w_expert_router,
              b_expert_router, w_exp_gate, w_exp_up, w_exp_down, final_norm_g):
    weights = (norm_mix_g, norm_ffn_g, w_ada, b_ada, w_in, ssd_conv_w, ssd_conv_b, ssd_dt_bias,
               ssd_a_log, ssd_d, ssd_norm_g, w_ssd_out, conf_conv_w, conf_conv_b, conf_ln_g,
               conf_ln_b, w_conf_out, w_o, w_group_router, b_group_router, w_expert_router,
               b_expert_router, w_exp_gate, w_exp_up, w_exp_down)
    bp = x_prompt.shape[0]
    dtp = x_prompt.dtype
    ssm_p0 = jnp.zeros((DEPTH, bp, SSD_HEADS, SSD_HEAD_DIM, SSD_STATE), dtp)
    sconv_p0 = jnp.zeros((DEPTH, bp, SSD_CONV - 1, SSD_CONV_CH), dtp)
    cconv_p0 = jnp.zeros((DEPTH, bp, CONF_KERNEL - 1, CONF_CH), dtp)
    y_prompt, ssm_p, sconv_p, cconv_p = _trunk(x_prompt, c_prompt, ssm_p0, sconv_p0, cconv_p0,
                                               weights, final_norm_g)
    y_sample, ssm_s, sconv_s, cconv_s = _trunk(x_sample, c_sample, state_ssm, state_ssd_conv,
                                               state_conf_conv, weights, final_norm_g)
    return (y_prompt, y_sample, ssm_p, sconv_p, cconv_p, ssm_s, sconv_s, cconv_s)
```

```python
import functools

import jax
import jax.numpy as jnp
from jax import lax
from jax.experimental import pallas as pl
from jax.experimental.pallas import tpu as pltpu

F32 = jnp.float32
BF16 = jnp.bfloat16
EPS = 1e-6
LANES = 128
SUBLANES = 8
VMEM_LIMIT = 52 * 1024 * 1024
NEG_BIG = -1e30
SSD_CHUNK = 128
CONF_CHUNK = 256
SAMPLE_SEQ_BLOCK = 8
SAMPLE_PRE_BLOCK = 32
CONF_SAMPLE_BLOCK = 16
ADA_TN = 1024
PROJ_TN = 512


def _cparams(sem):
    return pltpu.CompilerParams(dimension_semantics=sem, vmem_limit_bytes=VMEM_LIMIT)


def _silu(x):
    return x * jax.nn.sigmoid(x)


def _softplus(x):
    return jnp.maximum(x, 0.0) + jnp.log1p(jnp.exp(-jnp.abs(x)))


def _split2(x):
    hi = x.astype(BF16)
    lo = (x - hi.astype(F32)).astype(BF16)
    return hi, lo


def _split3(x):
    hi = x.astype(BF16)
    r = x - hi.astype(F32)
    mid = r.astype(BF16)
    lo = (r - mid.astype(F32)).astype(BF16)
    return hi, mid, lo


def _dot(a, b):
    return jnp.dot(a, b, preferred_element_type=F32)


def _dot_nt(a, b):
    return lax.dot_general(a, b, (((1,), (1,)), ((), ())), preferred_element_type=F32)


def _dot_parts(parts, w):
    acc = _dot(parts[0], w)
    for p in parts[1:]:
        acc = acc + _dot(p, w)
    return acc


def _adaln_kernel(c_ref, w_ref, b_ref, o_ref):
    s = _silu(c_ref[...]).astype(BF16)
    o_ref[...] = _dot(s, w_ref[...].astype(BF16)) + b_ref[...]


def _adaln(c_all, w_ada, b_ada):
    depth, d, n6 = w_ada.shape
    rc = c_all.shape[0]
    tn = min(ADA_TN, n6)
    return pl.pallas_call(
        _adaln_kernel,
        grid=(depth, n6 // tn),
        in_specs=[
            pl.BlockSpec((rc, d), lambda l, j: (0, 0)),
            pl.BlockSpec((None, d, tn), lambda l, j: (l, 0, j)),
            pl.BlockSpec((None, 1, tn), lambda l, j: (l, 0, j)),
        ],
        out_specs=pl.BlockSpec((None, rc, tn), lambda l, j: (l, 0, j)),
        out_shape=jax.ShapeDtypeStruct((depth, rc, n6), F32),
        compiler_params=_cparams(("arbitrary", "arbitrary")),
        name="adaln",
    )(c_all, w_ada, b_ada.reshape(depth, 1, n6))


def _tile_mod(i, npt, ts, p_ref, s_ref):
    ms = s_ref[...]
    tiled = jnp.concatenate([ms] * ts, axis=0)
    return jnp.where(i < npt, p_ref[...], tiled)


def _mod_specs(npt, tiles_per_seq, nseq_p, bs, width, sec, col_of):
    def p_map(*idx):
        i = idx[0]
        return (jnp.minimum(i // tiles_per_seq, nseq_p - 1), 0, col_of(sec, *idx))

    def s_map(*idx):
        return (0, col_of(sec, *idx))

    return [pl.BlockSpec((None, 1, width), p_map), pl.BlockSpec((bs, width), s_map)]


def _normproj_kernel(x_ref, g_ref, scp_ref, scs_ref, shp_ref, shs_ref, *rest, npt, ts, epilogue):
    nw = 2 if epilogue == "glu" else 1
    w_refs = rest[:nw]
    o_ref = rest[nw]
    h_scr = rest[nw + 1]
    i = pl.program_id(0)

    @pl.when(pl.program_id(1) == 0)
    def _():
        x = x_ref[...]
        xn = x * lax.rsqrt(jnp.mean(x * x, axis=-1, keepdims=True) + EPS) * g_ref[...]
        sc = _tile_mod(i, npt, ts, scp_ref, scs_ref)
        sh = _tile_mod(i, npt, ts, shp_ref, shs_ref)
        h_scr[...] = (xn * (1.0 + sc) + sh).astype(BF16)

    h = h_scr[...]
    acc = _dot(h, w_refs[0][...])
    if epilogue == "glu":
        acc = acc * jax.nn.sigmoid(_dot(h, w_refs[1][...]))
    elif epilogue == "sigmoid":
        acc = jax.nn.sigmoid(acc)
    o_ref[...] = acc.astype(o_ref.dtype)


def _normproj(x, gain, modp, mods, sec_sc, sec_sh, ws, out_dtype, epilogue, dims):
    m, d = x.shape
    tm, npt, tps, nb, bs, ts = dims
    ncols = ws[0].shape[1]
    tn = min(PROJ_TN, ncols)
    col0 = lambda sec, i, j: sec
    in_specs = [
        pl.BlockSpec((tm, d), lambda i, j: (i, 0)),
        pl.BlockSpec((1, d), lambda i, j: (0, 0)),
        *_mod_specs(npt, tps, nb, bs, d, sec_sc, col0),
        *_mod_specs(npt, tps, nb, bs, d, sec_sh, col0),
    ] + [pl.BlockSpec((d, tn), lambda i, j: (0, j)) for _ in ws]
    return pl.pallas_call(
        functools.partial(_normproj_kernel, npt=npt, ts=ts, epilogue=epilogue),
        grid=(m // tm, ncols // tn),
        in_specs=in_specs,
        out_specs=pl.BlockSpec((tm, tn), lambda i, j: (i, j)),
        out_shape=jax.ShapeDtypeStruct((m, ncols), out_dtype),
        scratch_shapes=[pltpu.VMEM((tm, d), BF16)],
        compiler_params=_cparams(("arbitrary", "arbitrary")),
        name="normproj_" + epilogue,
    )(x, gain, modp, mods, modp, mods, *ws)


def _ssd_prompt_kernel(zx_ref, dt_ref, cw_ref, cb_ref, dtb_ref, alog_ref, dexp_ref, ng_ref, eh_ref,
                       y_ref, st_ref, xpad, y_scr, *, d, g_n, n, k_h, p):
    c = pl.program_id(1)
    L = zx_ref.shape[0]
    kp = k_h * p
    cc = cw_ref.shape[1]
    kw = cw_ref.shape[0]

    @pl.when(c == 0)
    def _():
        xpad[0:SUBLANES, :] = jnp.zeros((SUBLANES, cc), F32)
        st_ref[...] = jnp.zeros(st_ref.shape, F32)

    zx = zx_ref[...]
    z = zx[:, :d].astype(F32)
    xr = zx[:, d:].astype(F32)
    xpad[SUBLANES:SUBLANES + L, :] = xr
    conv = jnp.broadcast_to(cb_ref[...], (L, cc))
    for k in range(kw):
        off = SUBLANES - (kw - 1) + k
        conv = conv + cw_ref[k:k + 1, :] * xpad[off:off + L, :]
    xpad[0:SUBLANES, :] = xr[L - SUBLANES:, :]
    xbc = _silu(conv)
    xs = xbc[:, :d]
    bm = xbc[:, d:d + g_n * n]
    cm = xbc[:, d + g_n * n:]

    dt = _softplus(dt_ref[...] + dtb_ref[...])
    a = -jnp.exp(alog_ref[...])
    dta = dt * a
    row = lax.broadcasted_iota(jnp.int32, (L, L), 0)
    col = lax.broadcasted_iota(jnp.int32, (L, L), 1)
    causal = row >= col
    tril = jnp.where(causal, 1.0, 0.0).astype(BF16)
    a_hi, a_mid, a_lo = _split3(dta)
    acum = _dot(tril, a_hi) + _dot(tril, a_mid) + _dot(tril, a_lo)
    acum_t = acum.T
    dt_t = dt.T
    alast = acum[L - 1:L, :]
    wend = jnp.exp(alast - acum) * dt
    eac = jnp.exp(acum)
    eh = eh_ref[...]
    wend_x = _dot_parts(_split2(wend), eh)
    eac_x = _dot_parts(_split2(eac), eh)
    xw = xs * wend_x
    dec = jnp.exp(alast)
    lane = lax.broadcasted_iota(jnp.int32, (L, 2 * p), 1)

    for g in range(g_n):
        bg = bm[:, g * n:(g + 1) * n].astype(BF16)
        cg = cm[:, g * n:(g + 1) * n].astype(BF16)
        cbm = _dot_nt(cg, bg)
        s_g = st_ref[g * kp:(g + 1) * kp, :]
        yoff = _dot_nt(cg, s_g.astype(BF16))
        for j in range(k_h // 2):
            h0 = g * k_h + 2 * j
            ms = []
            for h in (h0, h0 + 1):
                seg = acum[:, h:h + 1] - acum_t[h:h + 1, :]
                dcy = jnp.exp(jnp.where(causal, seg, NEG_BIG))
                ms.append((cbm * dcy * dt_t[h:h + 1, :]).astype(BF16))
            lhs = jnp.concatenate(ms, axis=1)
            xp = xs[:, h0 * p:(h0 + 2) * p]
            rhs = jnp.concatenate([jnp.where(lane < p, xp, 0.0), jnp.where(lane >= p, xp, 0.0)],
                                  axis=0).astype(BF16)
            lo = h0 * p
            y_scr[:, lo:lo + 2 * p] = (_dot(lhs, rhs)
                                       + yoff[:, 2 * j * p:(2 * j + 2) * p] * eac_x[:, lo:lo + 2 * p])
        upd = _dot(xw[:, g * kp:(g + 1) * kp].T.astype(BF16), bg)
        for hh in range(k_h):
            h = g * k_h + hh
            r0 = h * p
            dsc = jnp.broadcast_to(dec[:, h:h + 1], (p, n))
            st_ref[r0:r0 + p, :] = dsc * s_g[hh * p:(hh + 1) * p, :] + upd[hh * p:(hh + 1) * p, :]

    y = (y_scr[...] + dexp_ref[...] * xs) * _silu(z)
    ng = ng_ref[...]
    for g in range(g_n):
        yg = y[:, g * kp:(g + 1) * kp]
        ms_ = jnp.mean(yg * yg, axis=-1, keepdims=True)
        y_ref[:, g * kp:(g + 1) * kp] = (yg * lax.rsqrt(ms_ + EPS) * ng[:, g * kp:(g + 1) * kp]).astype(y_ref.dtype)


def _ssd_prompt(zx, dtraw, cw, cb, dtb, alog, dexp, ng, eh, nb, t, hp, n, g_n, k_h, p):
    d = hp
    L = min(SSD_CHUNK, t)
    nc = t // L
    cc = cw.shape[1]
    kern = functools.partial(_ssd_prompt_kernel, d=d, g_n=g_n, n=n, k_h=k_h, p=p)
    const = lambda shape: pl.BlockSpec(shape, lambda b, c: (0,) * len(shape))
    return pl.pallas_call(
        kern,
        grid=(nb, nc),
        in_specs=[
            pl.BlockSpec((L, d + cc), lambda b, c: (b * nc + c, 0)),
            pl.BlockSpec((L, LANES), lambda b, c: (b * nc + c, 0)),
            const(cw.shape), const(cb.shape), const(dtb.shape), const(alog.shape),
            const(dexp.shape), const(ng.shape), const(eh.shape),
        ],
        out_specs=[
            pl.BlockSpec((L, d), lambda b, c: (b * nc + c, 0)),
            pl.BlockSpec((None, hp, n), lambda b, c: (b, 0, 0)),
        ],
        out_shape=[
            jax.ShapeDtypeStruct((nb * t, d), BF16),
            jax.ShapeDtypeStruct((nb, hp, n), F32),
        ],
        scratch_shapes=[pltpu.VMEM((L + SUBLANES, cc), F32), pltpu.VMEM((L, d), F32)],
        compiler_params=_cparams(("arbitrary", "arbitrary")),
        name="ssd_prompt",
    )(zx, dtraw, cw, cb, dtb, alog, dexp, ng, eh)


def _ssd_sample_pre_kernel(zx_ref, dt_ref, st_ref, cw_ref, cb_ref, dtb_ref, alog_ref, dexp_ref,
                           eh_ref, gh_ref, a_ref, ex_ref, zs_ref, xw_ref, c_ref, b_ref, dec_ref,
                           *, d, g_n, n):
    ts, bs, _ = zx_ref.shape
    kw = cw_ref.shape[0]
    gn = g_n * n
    rows = [st_ref[j] for j in range(kw - 1)]
    rows += [zx_ref[t][:, d:].astype(F32) for t in range(ts)]
    eh = eh_ref[...]
    gh = gh_ref[...]
    a = -jnp.exp(alog_ref[...])
    xs, bm, cm, dts, acs = [], [], [], [], []
    run = None
    for t in range(ts):
        conv = cb_ref[...] + cw_ref[0:1, :] * rows[t]
        for k in range(1, kw):
            conv = conv + cw_ref[k:k + 1, :] * rows[t + k]
        xbc = _silu(conv)
        xs.append(xbc[:, :d])
        bm.append(xbc[:, d:d + gn])
        cm.append(xbc[:, d + gn:])
        dtt = _softplus(dt_ref[t] + dtb_ref[...])
        dts.append(dtt)
        run = dtt * a if run is None else run + dtt * a
        acs.append(run)
    alast = acs[-1]
    dec_ref[...] = jnp.exp(alast)
    for t in range(ts):
        acc = dexp_ref[...] * xs[t]
        for j in range(t + 1):
            cbh = _dot_parts(_split2(cm[t] * bm[j]), gh)
            coef = cbh * dts[j]
            if j < t:
                coef = coef * jnp.exp(acs[t] - acs[j])
            acc = acc + _dot_parts(_split2(coef), eh) * xs[j]
        a_ref[t] = acc
        ex_ref[t] = _dot_parts(_split2(jnp.exp(acs[t])), eh)
        zs_ref[t] = _silu(zx_ref[t][:, :d].astype(F32))
        xw_ref[t] = xs[t] * _dot_parts(_split2(jnp.exp(alast - acs[t]) * dts[t]), eh)
        c_ref[t] = cm[t]
        b_ref[t] = bm[t]


def _ssd_sample_pre(zx_s, dt_s, sconv, cw, cb, dtb, alog, dexp, eh, gh, d, g_n, n):
    ts, bs, _ = zx_s.shape
    gn = g_n * n
    kern = functools.partial(_ssd_sample_pre_kernel, d=d, g_n=g_n, n=n)
    sb = min(SAMPLE_PRE_BLOCK, bs)
    kw = cw.shape[0]
    const = lambda a: pl.BlockSpec(a.shape, lambda i: (0,) * a.ndim)
    seq3 = lambda c: pl.BlockSpec((ts, sb, c), lambda i: (0, i, 0))
    big = jax.ShapeDtypeStruct((ts, bs, d), F32)
    small = jax.ShapeDtypeStruct((ts, bs, gn), F32)
    return pl.pallas_call(
        kern,
        grid=(bs // sb,),
        in_specs=[seq3(zx_s.shape[2]), seq3(LANES), pl.BlockSpec((kw - 1, sb, sconv.shape[2]), lambda i: (0, i, 0)),
                  const(cw), const(cb), const(dtb), const(alog), const(dexp), const(eh), const(gh)],
        out_specs=[seq3(d), seq3(d), seq3(d), seq3(d), seq3(gn), seq3(gn),
                   pl.BlockSpec((sb, LANES), lambda i: (i, 0))],
        out_shape=[big, big, big, big, small, small, jax.ShapeDtypeStruct((bs, LANES), F32)],
        compiler_params=_cparams(("arbitrary",)),
        name="ssd_sample_pre",
    )(zx_s, dt_s, sconv, cw, cb, dtb, alog, dexp, eh, gh)


def _ssd_sample_state_kernel(dec_ref, s0_ref, a_ref, ex_ref, zs_ref, xwt_ref, c_ref, b_ref, ng_ref,
                             y_ref, s1_ref, *, g_n, n, k_h, p, sb):
    i = pl.program_id(0)
    kp = k_h * p
    n_heads = g_n * k_h
    rows8 = sb * SUBLANES
    rid = lax.broadcasted_iota(jnp.int32, (rows8, 1), 0)
    ng = ng_ref[...]

    def body(s, carry):
        r0 = pl.multiple_of(s * SUBLANES, SUBLANES)
        in_s = (rid >= r0) & (rid < r0 + SUBLANES)
        for g in range(g_n):
            cg = c_ref[pl.ds(r0, SUBLANES), g * n:(g + 1) * n].astype(BF16)
            s_g = s0_ref[s, g * kp:(g + 1) * kp, :]
            yoff = _dot_nt(cg, s_g.astype(BF16))
            b_all = b_ref[:, g * n:(g + 1) * n]
            b_msk = jnp.where(in_s, b_all, 0.0).astype(BF16)
            upd = _dot(xwt_ref[g * kp:(g + 1) * kp, :].astype(BF16), b_msk)
            for hh in range(k_h):
                h = g * k_h + hh
                dsc = dec_ref[(i * sb + s) * n_heads + h]
                s1_ref[s, h * p:(h + 1) * p, :] = (dsc * s_g[hh * p:(hh + 1) * p, :]
                                                   + upd[hh * p:(hh + 1) * p, :])
            cols = slice(g * kp, (g + 1) * kp)
            yg = ((a_ref[pl.ds(r0, SUBLANES), cols] + ex_ref[pl.ds(r0, SUBLANES), cols] * yoff)
                  * zs_ref[pl.ds(r0, SUBLANES), cols])
            ms_ = jnp.mean(yg * yg, axis=-1, keepdims=True)
            y_ref[pl.ds(r0, SUBLANES), cols] = (yg * lax.rsqrt(ms_ + EPS) * ng[:, cols]).astype(y_ref.dtype)
        return carry

    lax.fori_loop(0, sb, body, 0)


def _ssd_sample_state(dec, s0, a8, ex8, zs8, xwt, c8, b8, ng, g_n, n, k_h, p):
    bs, hp, _ = s0.shape
    d = a8.shape[1]
    gn = g_n * n
    sb = min(SAMPLE_SEQ_BLOCK, bs)
    r8 = sb * SUBLANES
    kern = functools.partial(_ssd_sample_state_kernel, g_n=g_n, n=n, k_h=k_h, p=p, sb=sb)
    return pl.pallas_call(
        kern,
        grid=(bs // sb,),
        in_specs=[
            pl.BlockSpec(memory_space=pltpu.SMEM),
            pl.BlockSpec((sb, hp, n), lambda i: (i, 0, 0)),
            pl.BlockSpec((r8, d), lambda i: (i, 0)),
            pl.BlockSpec((r8, d), lambda i: (i, 0)),
            pl.BlockSpec((r8, d), lambda i: (i, 0)),
            pl.BlockSpec((None, d, r8), lambda i: (i, 0, 0)),
            pl.BlockSpec((r8, gn), lambda i: (i, 0)),
            pl.BlockSpec((r8, gn), lambda i: (i, 0)),
            pl.BlockSpec((1, d), lambda i: (0, 0)),
        ],
        out_specs=[
            pl.BlockSpec((r8, d), lambda i: (i, 0)),
            pl.BlockSpec((sb, hp, n), lambda i: (i, 0, 0)),
        ],
        out_shape=[
            jax.ShapeDtypeStruct((bs * SUBLANES, d), F32),
            jax.ShapeDtypeStruct((bs, hp, n), F32),
        ],
        compiler_params=_cparams(("arbitrary",)),
        name="ssd_sample_state",
    )(dec, s0, a8, ex8, zs8, xwt, c8, b8, ng)


def _ln_swish(v, g, b):
    mu = jnp.mean(v, axis=-1, keepdims=True)
    cen = v - mu
    var = jnp.mean(cen * cen, axis=-1, keepdims=True)
    return _silu(cen * lax.rsqrt(var + EPS) * g + b)


def _conf_prompt_kernel(u_ref, w_ref, b_ref, lg_ref, lb_ref, v_ref, upad, *, halo):
    c = pl.program_id(1)
    L, d = u_ref.shape
    kw = w_ref.shape[0]

    @pl.when(c == 0)
    def _():
        upad[0:halo, :] = jnp.zeros((halo, d), F32)

    u = u_ref[...].astype(F32)
    upad[halo:halo + L, :] = u
    acc = jnp.broadcast_to(b_ref[...], (L, d))
    for k in range(kw):
        off = halo - (kw - 1) + k
        acc = acc + w_ref[k:k + 1, :] * upad[off:off + L, :]
    upad[0:halo, :] = u[L - halo:, :]
    v_ref[...] = _ln_swish(acc, lg_ref[...], lb_ref[...]).astype(v_ref.dtype)


def _conf_prompt(u, w, b, lg, lb, nb, t):
    d = u.shape[1]
    kw = w.shape[0]
    L = min(CONF_CHUNK, t)
    nc = t // L
    halo = -(-(kw - 1) // SUBLANES) * SUBLANES
    const = lambda a: pl.BlockSpec(a.shape, lambda bb, c: (0,) * a.ndim)
    return pl.pallas_call(
        functools.partial(_conf_prompt_kernel, halo=halo),
        grid=(nb, nc),
        in_specs=[pl.BlockSpec((L, d), lambda bb, c: (bb * nc + c, 0)), const(w), const(b), const(lg), const(lb)],
        out_specs=pl.BlockSpec((L, d), lambda bb, c: (bb * nc + c, 0)),
        out_shape=jax.ShapeDtypeStruct((nb * t, d), BF16),
        scratch_shapes=[pltpu.VMEM((L + halo, d), F32)],
        compiler_params=_cparams(("arbitrary", "arbitrary")),
        name="conf_prompt",
    )(u, w, b, lg, lb)


def _conf_sample_kernel(u_ref, st_ref, w_ref, b_ref, lg_ref, lb_ref, v_ref):
    ts, sb, d = u_ref.shape
    kw = w_ref.shape[0]
    rows = [st_ref[j] for j in range(kw - 1)]
    rows += [u_ref[t].astype(F32) for t in range(ts)]
    for t in range(ts):
        acc = b_ref[...] + w_ref[0:1, :] * rows[t]
        for k in range(1, kw):
            acc = acc + w_ref[k:k + 1, :] * rows[t + k]
        v_ref[t] = _ln_swish(acc, lg_ref[...], lb_ref[...]).astype(v_ref.dtype)


def _conf_sample(u_s, cconv, w, b, lg, lb):
    ts, bs, d = u_s.shape
    kw = w.shape[0]
    sb = min(CONF_SAMPLE_BLOCK, bs)
    const = lambda a: pl.BlockSpec(a.shape, lambda i: (0,) * a.ndim)
    return pl.pallas_call(
        _conf_sample_kernel,
        grid=(bs // sb,),
        in_specs=[
            pl.BlockSpec((ts, sb, d), lambda i: (0, i, 0)),
            pl.BlockSpec((kw - 1, sb, d), lambda i: (0, i, 0)),
            const(w), const(b), const(lg), const(lb),
        ],
        out_specs=pl.BlockSpec((ts, sb, d), lambda i: (0, i, 0)),
        out_shape=jax.ShapeDtypeStruct((ts, bs, d), BF16),
        compiler_params=_cparams(("arbitrary",)),
        name="conf_sample",
    )(u_s, cconv, w, b, lg, lb)


def _merge_kernel(y_ref, v_ref, ws_ref, wc_ref, gs_ref, gc_ref, o_ref):
    a = _dot(y_ref[...], ws_ref[...])
    b = _dot(v_ref[...], wc_ref[...])
    o_ref[...] = (gs_ref[...].astype(F32) * a + gc_ref[...].astype(F32) * b).astype(o_ref.dtype)


def _merge(ynorm, v, ws, wc, gates, tm):
    m, d = ynorm.shape
    tn = min(PROJ_TN, d)
    nj = d // tn
    return pl.pallas_call(
        _merge_kernel,
        grid=(m // tm, nj),
        in_specs=[
            pl.BlockSpec((tm, d), lambda i, j: (i, 0)),
            pl.BlockSpec((tm, d), lambda i, j: (i, 0)),
            pl.BlockSpec((d, tn), lambda i, j: (0, j)),
            pl.BlockSpec((d, tn), lambda i, j: (0, j)),
            pl.BlockSpec((tm, tn), lambda i, j: (i, j)),
            pl.BlockSpec((tm, tn), lambda i, j: (i, nj + j)),
        ],
        out_specs=pl.BlockSpec((tm, tn), lambda i, j: (i, j)),
        out_shape=jax.ShapeDtypeStruct((m, d), BF16),
        compiler_params=_cparams(("arbitrary", "arbitrary")),
        name="merge",
    )(ynorm, v, ws, wc, gates, gates)


def _outproj_kernel(mx_ref, w_ref, x_ref, gp_ref, gs_ref, o_ref, *, npt, ts):
    gate = _tile_mod(pl.program_id(0), npt, ts, gp_ref, gs_ref)
    o_ref[...] = x_ref[...] + gate * _dot(mx_ref[...], w_ref[...])


def _outproj(mixed, wo, x, modp, mods, sec, dims):
    m, d = x.shape
    tm, npt, tps, nb, bs, ts = dims
    tn = min(PROJ_TN, d)
    nj = d // tn
    colj = lambda s, i, j: s * nj + j
    return pl.pallas_call(
        functools.partial(_outproj_kernel, npt=npt, ts=ts),
        grid=(m // tm, nj),
        in_specs=[
            pl.BlockSpec((tm, d), lambda i, j: (i, 0)),
            pl.BlockSpec((d, tn), lambda i, j: (0, j)),
            pl.BlockSpec((tm, tn), lambda i, j: (i, j)),
            *_mod_specs(npt, tps, nb, bs, tn, sec, colj),
        ],
        out_specs=pl.BlockSpec((tm, tn), lambda i, j: (i, j)),
        out_shape=jax.ShapeDtypeStruct((m, d), F32),
        compiler_params=_cparams(("arbitrary", "arbitrary")),
        name="outproj",
    )(mixed, wo, x, modp, mods)


def _router_kernel(x_ref, g_ref, scp_ref, scs_ref, shp_ref, shs_ref, whi_ref, wlo_ref, rb_ref,
                   h_ref, gate_ref, *, npt, ts, n_e, n_g):
    i = pl.program_id(0)
    epg = n_e // n_g
    x = x_ref[...]
    xn = x * lax.rsqrt(jnp.mean(x * x, axis=-1, keepdims=True) + EPS) * g_ref[...]
    sc = _tile_mod(i, npt, ts, scp_ref, scs_ref)
    sh = _tile_mod(i, npt, ts, shp_ref, shs_ref)
    h = xn * (1.0 + sc) + sh
    hi, lo = _split2(h)
    h_ref[...] = hi
    logit = _dot(hi, whi_ref[...]) + _dot(lo, whi_ref[...]) + _dot(hi, wlo_ref[...]) + rb_ref[...]
    tm = logit.shape[0]
    lane = lax.broadcasted_iota(jnp.int32, (tm, LANES), 1).astype(F32)
    first = lambda hit: jnp.min(jnp.where(hit, lane, float(LANES)), axis=-1, keepdims=True)
    gl = jnp.where(lane >= n_e, jnp.where(lane < n_e + n_g, logit, NEG_BIG), NEG_BIG)
    gmax = jnp.max(gl, axis=-1, keepdims=True)
    g_top = 1.0 / jnp.sum(jnp.exp(gl - gmax), axis=-1, keepdims=True)
    g_idx = first(gl == gmax) - n_e
    el = jnp.where(lane >= g_idx * epg, jnp.where(lane < (g_idx + 1.0) * epg, logit, NEG_BIG), NEG_BIG)
    m1 = jnp.max(el, axis=-1, keepdims=True)
    i1 = first(el == m1)
    el2 = jnp.where(lane == i1, NEG_BIG, el)
    m2 = jnp.max(el2, axis=-1, keepdims=True)
    i2 = first(el2 == m2)
    r = jnp.exp(m2 - m1)
    w1 = 1.0 / (1.0 + r)
    w2 = r * w1
    gate_ref[...] = g_top * (jnp.where(lane == i1, w1, 0.0) + jnp.where(lane == i2, w2, 0.0))


def _router(x, gain, modp, mods, sec_sc, sec_sh, whi, wlo, rb, n_e, n_g, dims):
    m, d = x.shape
    tm, npt, tps, nb, bs, ts = dims
    col0 = lambda sec, i: sec
    return pl.pallas_call(
        functools.partial(_router_kernel, npt=npt, ts=ts, n_e=n_e, n_g=n_g),
        grid=(m // tm,),
        in_specs=[
            pl.BlockSpec((tm, d), lambda i: (i, 0)),
            pl.BlockSpec((1, d), lambda i: (0, 0)),
            *_mod_specs(npt, tps, nb, bs, d, sec_sc, col0),
            *_mod_specs(npt, tps, nb, bs, d, sec_sh, col0),
            pl.BlockSpec((d, LANES), lambda i: (0, 0)),
            pl.BlockSpec((d, LANES), lambda i: (0, 0)),
            pl.BlockSpec((1, LANES), lambda i: (0, 0)),
        ],
        out_specs=[pl.BlockSpec((tm, d), lambda i: (i, 0)), pl.BlockSpec((tm, LANES), lambda i: (i, 0))],
        out_shape=[jax.ShapeDtypeStruct((m, d), BF16), jax.ShapeDtypeStruct((m, LANES), F32)],
        compiler_params=_cparams(("arbitrary",)),
        name="router",
    )(x, gain, modp, mods, modp, mods, whi, wlo, rb)


def _moe_kernel(h_ref, gate_ref, wg_ref, wu_ref, wd_ref, x_ref, gp_ref, gs_ref, o_ref, acc, *, npt, ts):
    i = pl.program_id(0)
    e = pl.program_id(1)

    @pl.when(e == 0)
    def _():
        acc[...] = jnp.zeros(acc.shape, F32)

    h = h_ref[...]
    hid = _silu(_dot(h, wg_ref[...].astype(BF16))) * _dot(h, wu_ref[...].astype(BF16))
    gates = gate_ref[...]
    lane = lax.broadcasted_iota(jnp.int32, gates.shape, 1)
    gcol = jnp.sum(jnp.where(lane == e, gates, 0.0), axis=-1, keepdims=True)
    acc[...] += _dot((hid * gcol).astype(BF16), wd_ref[...].astype(BF16))

    @pl.when(e == pl.num_programs(1) - 1)
    def _():
        g2 = _tile_mod(i, npt, ts, gp_ref, gs_ref)
        o_ref[...] = x_ref[...] + g2 * acc[...]


def _moe(h2, gates, wg, wu, wd, x, modp, mods, sec, dims):
    m, d = x.shape
    tm, npt, tps, nb, bs, ts = dims
    n_e, _, f = wg.shape
    col0 = lambda s, i, e: s
    return pl.pallas_call(
        functools.partial(_moe_kernel, npt=npt, ts=ts),
        grid=(m // tm, n_e),
        in_specs=[
            pl.BlockSpec((tm, d), lambda i, e: (i, 0)),
            pl.BlockSpec((tm, LANES), lambda i, e: (i, 0)),
            pl.BlockSpec((None, d, f), lambda i, e: (e, 0, 0)),
            pl.BlockSpec((None, d, f), lambda i, e: (e, 0, 0)),
            pl.BlockSpec((None, f, d), lambda i, e: (e, 0, 0)),
            pl.BlockSpec((tm, d), lambda i, e: (i, 0)),
            *_mod_specs(npt, tps, nb, bs, d, sec, col0),
        ],
        out_specs=pl.BlockSpec((tm, d), lambda i, e: (i, 0)),
        out_shape=jax.ShapeDtypeStruct((m, d), F32),
        scratch_shapes=[pltpu.VMEM((tm, d), F32)],
        compiler_params=_cparams(("arbitrary", "arbitrary")),
        name="moe",
    )(h2, gates, wg, wu, wd, x, modp, mods)


def _final_norm_kernel(x_ref, g_ref, o_ref):
    x = x_ref[...]
    o_ref[...] = x * lax.rsqrt(jnp.mean(x * x, axis=-1, keepdims=True) + EPS) * g_ref[...]


def _final_norm(x, g, tm):
    m, d = x.shape
    return pl.pallas_call(
        _final_norm_kernel,
        grid=(m // tm,),
        in_specs=[pl.BlockSpec((tm, d), lambda i: (i, 0)), pl.BlockSpec((1, d), lambda i: (0, 0))],
        out_specs=pl.BlockSpec((tm, d), lambda i: (i, 0)),
        out_shape=jax.ShapeDtypeStruct((m, d), F32),
        compiler_params=_cparams(("arbitrary",)),
        name="final_norm",
    )(x, g)


def _to_seq8(a):
    ts, bs, c = a.shape
    a = jnp.transpose(a, (1, 0, 2))
    a = jnp.pad(a, ((0, 0), (0, SUBLANES - ts), (0, 0)))
    return a.reshape(bs * SUBLANES, c)


def kernel(x_prompt, x_sample, c_prompt, c_sample, state_ssm, state_ssd_conv, state_conf_conv, norm_mix_g, norm_ffn_g, w_ada, b_ada, w_in, ssd_conv_w, ssd_conv_b, ssd_dt_bias, ssd_a_log, ssd_d, ssd_norm_g, w_ssd_out, conf_conv_w, conf_conv_b, conf_ln_g, conf_ln_b, w_conf_out, w_o, w_group_router, b_group_router, w_expert_router, b_expert_router, w_exp_gate, w_exp_up, w_exp_down, final_norm_g):
    nb, t, d = x_prompt.shape
    bs, ts, _ = x_sample.shape
    depth = w_in.shape[0]
    n_h = ssd_dt_bias.shape[1]
    p = d // n_h
    n = state_ssm.shape[-1]
    cc = ssd_conv_w.shape[-1]
    g_n = (cc - d) // (2 * n)
    k_h = n_h // g_n
    gn = g_n * n
    kc = conf_conv_w.shape[1]
    n_e = w_exp_gate.shape[1]
    n_g = w_group_router.shape[-1]
    tm = bs * ts
    assert t % tm == 0 and tm % SUBLANES == 0 and bs % SUBLANES == 0 and ts <= SUBLANES
    assert 2 * p == LANES and k_h % 2 == 0 and n_h <= LANES and n_e + n_g <= LANES
    npt = nb * t // tm
    dims = (tm, npt, t // tm, nb, bs, ts)
    mp = nb * t

    x = jnp.concatenate([x_prompt.reshape(mp, d), jnp.transpose(x_sample, (1, 0, 2)).reshape(tm, d)], axis=0)

    rc = -(-(nb + bs) // SUBLANES) * SUBLANES
    c_all = jnp.concatenate([c_prompt, c_sample, jnp.zeros((rc - nb - bs, d), F32)], axis=0)
    mod = _adaln(c_all, w_ada, b_ada)

    head_of = jnp.arange(d) // p
    eh = (jnp.arange(LANES)[:, None] == head_of[None, :]).astype(BF16)
    gh = ((jnp.arange(gn)[:, None] // n) == (jnp.arange(LANES)[None, :] // k_h)).astype(BF16)
    gh = gh * (jnp.arange(LANES)[None, :] < n_h).astype(BF16)
    pad_h = lambda v: jnp.pad(v, (0, LANES - n_h)).reshape(1, LANES)

    ssm_p, sconv_p, cconv_p, ssm_s, sconv_s, cconv_s = [], [], [], [], [], []
    for l in range(depth):
        modp = mod[l, :nb].reshape(nb, 1, 6 * d)
        mods = mod[l, nb:nb + bs]
        wl = w_in[l]
        o_z, o_x, o_dt, o_glu, o_gs = 0, d, d + cc, d + cc + n_h, d + cc + n_h + 2 * d
        w_zx = wl[:, :o_dt].astype(BF16)
        w_dt = jnp.pad(wl[:, o_dt:o_glu], ((0, 0), (0, LANES - n_h))).astype(BF16)
        w_ga = wl[:, o_glu:o_glu + d].astype(BF16)
        w_gg = wl[:, o_glu + d:o_gs].astype(BF16)
        w_gates = wl[:, o_gs:].astype(BF16)
        g_mix = norm_mix_g[l].reshape(1, d)
        proj = functools.partial(_normproj, x, g_mix, modp, mods, 1, 0, dims=dims)
        zx = proj([w_zx], BF16, "plain")
        dtraw = proj([w_dt], F32, "plain")
        u = proj([w_ga, w_gg], BF16, "glu")
        gates = proj([w_gates], BF16, "sigmoid")

        cw = ssd_conv_w[l]
        cb = ssd_conv_b[l].reshape(1, cc)
        dtb = pad_h(ssd_dt_bias[l])
        alog = pad_h(ssd_a_log[l])
        dexp = jnp.repeat(ssd_d[l], p).reshape(1, d)
        ng = ssd_norm_g[l].reshape(1, d)

        y_p, s_p = _ssd_prompt(zx, dtraw, cw, cb, dtb, alog, dexp, ng, eh, nb, t, n_h * p, n, g_n, k_h, p)
        zx_s = zx[mp:].reshape(ts, bs, d + cc)
        dt_s = dtraw[mp:].reshape(ts, bs, LANES)
        sconv0 = state_ssd_conv[l]
        a_t, ex_t, zs_t, xw_t, c_t, b_t, dec = _ssd_sample_pre(
            zx_s, dt_s, jnp.transpose(sconv0, (1, 0, 2)), cw, cb, dtb, alog, dexp, eh, gh, d, g_n, n)
        sb = min(SAMPLE_SEQ_BLOCK, bs)
        xwt = _to_seq8(xw_t).reshape(bs // sb, sb * SUBLANES, d).transpose(0, 2, 1)
        dec = dec[:, :n_h].reshape(bs * n_h)
        y_s8, s_s = _ssd_sample_state(dec, state_ssm[l].reshape(bs, n_h * p, n), _to_seq8(a_t), _to_seq8(ex_t),
                                      _to_seq8(zs_t), xwt, _to_seq8(c_t), _to_seq8(b_t), ng, g_n, n, k_h, p)
        y_s = jnp.transpose(y_s8.reshape(bs, SUBLANES, d)[:, :ts], (1, 0, 2)).reshape(tm, d).astype(BF16)
        ynorm = jnp.concatenate([y_p, y_s], axis=0)

        ccw = conf_conv_w[l]
        ccb = conf_conv_b[l].reshape(1, d)
        lg = conf_ln_g[l].reshape(1, d)
        lb = conf_ln_b[l].reshape(1, d)
        v_p = _conf_prompt(u, ccw, ccb, lg, lb, nb, t)
        u_s = u[mp:].reshape(ts, bs, d)
        cconv0 = state_conf_conv[l]
        v_s = _conf_sample(u_s, jnp.transpose(cconv0, (1, 0, 2)), ccw, ccb, lg, lb)
        v = jnp.concatenate([v_p, v_s.reshape(tm, d)], axis=0)

        mixed = _merge(ynorm, v, w_ssd_out[l].astype(BF16), w_conf_out[l].astype(BF16), gates, tm)
        x = _outproj(mixed, w_o[l].astype(BF16), x, modp, mods, 2, dims)

        w_r = jnp.concatenate([w_expert_router[l], w_group_router[l],
                               jnp.zeros((d, LANES - n_e - n_g), F32)], axis=1)
        w_r_hi = w_r.astype(BF16)
        w_r_lo = (w_r - w_r_hi.astype(F32)).astype(BF16)
        rb = jnp.concatenate([b_expert_router[l], b_group_router[l], jnp.zeros((LANES - n_e - n_g,), F32)]).reshape(1, LANES)
        h2, gate_w = _router(x, norm_ffn_g[l].reshape(1, d), modp, mods, 4, 3, w_r_hi, w_r_lo, rb, n_e, n_g, dims)
        x = _moe(h2, gate_w, w_exp_gate[l], w_exp_up[l], w_exp_down[l], x, modp, mods, 5, dims)

        ssm_p.append(s_p.reshape(nb, n_h, p, n))
        xraw_p = zx[:mp, d:].reshape(nb, t, cc)
        sconv_p.append(xraw_p[:, t - (sconv0.shape[1]):].astype(F32))
        cconv_p.append(u[:mp].reshape(nb, t, d)[:, t - (kc - 1):].astype(F32))
        ssm_s.append(s_s.reshape(bs, n_h, p, n))
        xraw_s = jnp.transpose(zx_s[:, :, d:], (1, 0, 2)).astype(F32)
        sconv_s.append(jnp.concatenate([sconv0, xraw_s], axis=1)[:, ts:])
        cconv_s.append(jnp.concatenate([cconv0, jnp.transpose(u_s, (1, 0, 2)).astype(F32)], axis=1)[:, ts:])

    y = _final_norm(x, final_norm_g.reshape(1, d), tm)
    y_prompt = y[:mp].reshape(nb, t, d)
    y_sample = jnp.transpose(y[mp:].reshape(ts, bs, d), (1, 0, 2))
    return (y_prompt, y_sample, jnp.stack(ssm_p), jnp.stack(sconv_p), jnp.stack(cconv_p),
            jnp.stack(ssm_s), jnp.stack(sconv_s), jnp.stack(cconv_s))
```

```python
import functools

import jax
import jax.numpy as jnp
from jax import lax
from jax.experimental import pallas as pl
from jax.experimental.pallas import tpu as pltpu

F32 = jnp.float32
BF16 = jnp.bfloat16
EPS = 1e-6
LANES = 128
SUBLANES = 8
VMEM_LIMIT = 52 * 1024 * 1024
NEG_BIG = -1e30
SSD_CHUNK = 128
CONF_CHUNK = 256
SAMPLE_SEQ_BLOCK = 8
SAMPLE_PRE_BLOCK = 32
CONF_SAMPLE_BLOCK = 16
ADA_TN = 1024
PROJ_TN = 1024


def _pick_tn(ncols):
    assert ncols % LANES == 0
    return max(tn for tn in range(LANES, min(PROJ_TN, ncols) + 1, LANES) if ncols % tn == 0)


def _cparams(sem):
    return pltpu.CompilerParams(dimension_semantics=sem, vmem_limit_bytes=VMEM_LIMIT)


def _silu(x):
    return x * jax.nn.sigmoid(x)


def _softplus(x):
    return jnp.maximum(x, 0.0) + jnp.log1p(jnp.exp(-jnp.abs(x)))


def _split2(x):
    hi = x.astype(BF16)
    lo = (x - hi.astype(F32)).astype(BF16)
    return hi, lo


def _split3(x):
    hi = x.astype(BF16)
    r = x - hi.astype(F32)
    mid = r.astype(BF16)
    lo = (r - mid.astype(F32)).astype(BF16)
    return hi, mid, lo


def _dot(a, b):
    return jnp.dot(a, b, preferred_element_type=F32)


def _dot_nt(a, b):
    return lax.dot_general(a, b, (((1,), (1,)), ((), ())), preferred_element_type=F32)


def _dot_parts(parts, w):
    acc = _dot(parts[0], w)
    for p in parts[1:]:
        acc = acc + _dot(p, w)
    return acc


def _adaln_kernel(c_ref, w_ref, b_ref, o_ref):
    s = _silu(c_ref[...]).astype(BF16)
    o_ref[...] = _dot(s, w_ref[...].astype(BF16)) + b_ref[...]


def _adaln(c_all, w_ada, b_ada):
    depth, d, n6 = w_ada.shape
    rc = c_all.shape[0]
    tn = min(ADA_TN, n6)
    return pl.pallas_call(
        _adaln_kernel,
        grid=(depth, n6 // tn),
        in_specs=[
            pl.BlockSpec((rc, d), lambda l, j: (0, 0)),
            pl.BlockSpec((None, d, tn), lambda l, j: (l, 0, j)),
            pl.BlockSpec((None, 1, tn), lambda l, j: (l, 0, j)),
        ],
        out_specs=pl.BlockSpec((None, rc, tn), lambda l, j: (l, 0, j)),
        out_shape=jax.ShapeDtypeStruct((depth, rc, n6), F32),
        compiler_params=_cparams(("arbitrary", "arbitrary")),
        name="adaln",
    )(c_all, w_ada, b_ada.reshape(depth, 1, n6))


def _tile_mod(i, npt, ts, p_ref, s_ref):
    ms = s_ref[...]
    tiled = jnp.concatenate([ms] * ts, axis=0)
    return jnp.where(i < npt, p_ref[...], tiled)


def _mod_specs(npt, tiles_per_seq, nseq_p, bs, width, sec, col_of):
    def p_map(*idx):
        i = idx[0]
        return (jnp.minimum(i // tiles_per_seq, nseq_p - 1), 0, col_of(sec, *idx))

    def s_map(*idx):
        return (0, col_of(sec, *idx))

    return [pl.BlockSpec((None, 1, width), p_map), pl.BlockSpec((bs, width), s_map)]


def _normproj_kernel(x_ref, g_ref, scp_ref, scs_ref, shp_ref, shs_ref, *rest, npt, ts, epilogue):
    nw = 2 if epilogue == "glu" else 1
    w_refs = rest[:nw]
    o_ref = rest[nw]
    h_scr = rest[nw + 1]
    i = pl.program_id(0)

    @pl.when(pl.program_id(1) == 0)
    def _():
        x = x_ref[...]
        xn = x * lax.rsqrt(jnp.mean(x * x, axis=-1, keepdims=True) + EPS) * g_ref[...]
        sc = _tile_mod(i, npt, ts, scp_ref, scs_ref)
        sh = _tile_mod(i, npt, ts, shp_ref, shs_ref)
        h_scr[...] = (xn * (1.0 + sc) + sh).astype(BF16)

    h = h_scr[...]
    acc = _dot(h, w_refs[0][...])
    if epilogue == "glu":
        acc = acc * jax.nn.sigmoid(_dot(h, w_refs[1][...]))
    elif epilogue == "sigmoid":
        acc = jax.nn.sigmoid(acc)
    o_ref[...] = acc.astype(o_ref.dtype)


def _normproj(x, gain, modp, mods, sec_sc, sec_sh, ws, out_dtype, epilogue, dims):
    m, d = x.shape
    tm, npt, tps, nb, bs, ts = dims
    ncols = ws[0].shape[1]
    tn = _pick_tn(ncols)
    col0 = lambda sec, i, j: sec
    in_specs = [
        pl.BlockSpec((tm, d), lambda i, j: (i, 0)),
        pl.BlockSpec((1, d), lambda i, j: (0, 0)),
        *_mod_specs(npt, tps, nb, bs, d, sec_sc, col0),
        *_mod_specs(npt, tps, nb, bs, d, sec_sh, col0),
    ] + [pl.BlockSpec((d, tn), lambda i, j: (0, j)) for _ in ws]
    return pl.pallas_call(
        functools.partial(_normproj_kernel, npt=npt, ts=ts, epilogue=epilogue),
        grid=(m // tm, ncols // tn),
        in_specs=in_specs,
        out_specs=pl.BlockSpec((tm, tn), lambda i, j: (i, j)),
        out_shape=jax.ShapeDtypeStruct((m, ncols), out_dtype),
        scratch_shapes=[pltpu.VMEM((tm, d), BF16)],
        compiler_params=_cparams(("arbitrary", "arbitrary")),
        name="normproj_" + epilogue,
    )(x, gain, modp, mods, modp, mods, *ws)


def _ssd_prompt_kernel(zx_ref, dt_ref, cw_ref, cb_ref, dtb_ref, alog_ref, dexp_ref, ng_ref, eh_ref,
                       y_ref, st_ref, xpad, y_scr, *, d, g_n, n, k_h, p):
    c = pl.program_id(1)
    L = zx_ref.shape[0]
    kp = k_h * p
    cc = cw_ref.shape[1]
    kw = cw_ref.shape[0]

    @pl.when(c == 0)
    def _():
        xpad[0:SUBLANES, :] = jnp.zeros((SUBLANES, cc), F32)
        st_ref[...] = jnp.zeros(st_ref.shape, F32)

    zx = zx_ref[...]
    z = zx[:, :d].astype(F32)
    xr = zx[:, d:].astype(F32)
    xpad[SUBLANES:SUBLANES + L, :] = xr
    conv = jnp.broadcast_to(cb_ref[...], (L, cc))
    for k in range(kw):
        off = SUBLANES - (kw - 1) + k
        conv = conv + cw_ref[k:k + 1, :] * xpad[off:off + L, :]
    xpad[0:SUBLANES, :] = xr[L - SUBLANES:, :]
    xbc = _silu(conv)
    xs = xbc[:, :d]
    bm = xbc[:, d:d + g_n * n]
    cm = xbc[:, d + g_n * n:]

    dt = _softplus(dt_ref[...] + dtb_ref[...])
    a = -jnp.exp(alog_ref[...])
    dta = dt * a
    row = lax.broadcasted_iota(jnp.int32, (L, L), 0)
    col = lax.broadcasted_iota(jnp.int32, (L, L), 1)
    causal = row >= col
    tril = jnp.where(causal, 1.0, 0.0).astype(BF16)
    a_hi, a_mid, a_lo = _split3(dta)
    acum = _dot(tril, a_hi) + _dot(tril, a_mid) + _dot(tril, a_lo)
    acum_t = acum.T
    dt_t = dt.T
    alast = acum[L - 1:L, :]
    wend = jnp.exp(alast - acum) * dt
    eac = jnp.exp(acum)
    eh = eh_ref[...]
    wend_x = _dot_parts(_split2(wend), eh)
    eac_x = _dot_parts(_split2(eac), eh)
    xw = xs * wend_x
    dec = jnp.exp(alast)
    lane = lax.broadcasted_iota(jnp.int32, (L, 2 * p), 1)

    for g in range(g_n):
        bg = bm[:, g * n:(g + 1) * n].astype(BF16)
        cg = cm[:, g * n:(g + 1) * n].astype(BF16)
        cbm = _dot_nt(cg, bg)
        s_g = st_ref[g * kp:(g + 1) * kp, :]
        yoff = _dot_nt(cg, s_g.astype(BF16))
        for j in range(k_h // 2):
            h0 = g * k_h + 2 * j
            ms = []
            for h in (h0, h0 + 1):
                seg = acum[:, h:h + 1] - acum_t[h:h + 1, :]
                dcy = jnp.exp(jnp.where(causal, seg, NEG_BIG))
                ms.append((cbm * dcy * dt_t[h:h + 1, :]).astype(BF16))
            lhs = jnp.concatenate(ms, axis=1)
            xp = xs[:, h0 * p:(h0 + 2) * p]
            rhs = jnp.concatenate([jnp.where(lane < p, xp, 0.0), jnp.where(lane >= p, xp, 0.0)],
                                  axis=0).astype(BF16)
            lo = h0 * p
            y_scr[:, lo:lo + 2 * p] = (_dot(lhs, rhs)
                                       + yoff[:, 2 * j * p:(2 * j + 2) * p] * eac_x[:, lo:lo + 2 * p])
        upd = _dot(xw[:, g * kp:(g + 1) * kp].T.astype(BF16), bg)
        for hh in range(k_h):
            h = g * k_h + hh
            r0 = h * p
            dsc = jnp.broadcast_to(dec[:, h:h + 1], (p, n))
            st_ref[r0:r0 + p, :] = dsc * s_g[hh * p:(hh + 1) * p, :] + upd[hh * p:(hh + 1) * p, :]

    y = (y_scr[...] + dexp_ref[...] * xs) * _silu(z)
    ng = ng_ref[...]
    for g in range(g_n):
        yg = y[:, g * kp:(g + 1) * kp]
        ms_ = jnp.mean(yg * yg, axis=-1, keepdims=True)
        y_ref[:, g * kp:(g + 1) * kp] = (yg * lax.rsqrt(ms_ + EPS) * ng[:, g * kp:(g + 1) * kp]).astype(y_ref.dtype)


def _ssd_prompt(zx, dtraw, cw, cb, dtb, alog, dexp, ng, eh, nb, t, hp, n, g_n, k_h, p):
    d = hp
    L = min(SSD_CHUNK, t)
    nc = t // L
    cc = cw.shape[1]
    kern = functools.partial(_ssd_prompt_kernel, d=d, g_n=g_n, n=n, k_h=k_h, p=p)
    const = lambda shape: pl.BlockSpec(shape, lambda b, c: (0,) * len(shape))
    return pl.pallas_call(
        kern,
        grid=(nb, nc),
        in_specs=[
            pl.BlockSpec((L, d + cc), lambda b, c: (b * nc + c, 0)),
            pl.BlockSpec((L, LANES), lambda b, c: (b * nc + c, 0)),
            const(cw.shape), const(cb.shape), const(dtb.shape), const(alog.shape),
            const(dexp.shape), const(ng.shape), const(eh.shape),
        ],
        out_specs=[
            pl.BlockSpec((L, d), lambda b, c: (b * nc + c, 0)),
            pl.BlockSpec((None, hp, n), lambda b, c: (b, 0, 0)),
        ],
        out_shape=[
            jax.ShapeDtypeStruct((nb * t, d), BF16),
            jax.ShapeDtypeStruct((nb, hp, n), F32),
        ],
        scratch_shapes=[pltpu.VMEM((L + SUBLANES, cc), F32), pltpu.VMEM((L, d), F32)],
        compiler_params=_cparams(("arbitrary", "arbitrary")),
        name="ssd_prompt",
    )(zx, dtraw, cw, cb, dtb, alog, dexp, ng, eh)


def _ssd_sample_pre_kernel(zx_ref, dt_ref, st_ref, cw_ref, cb_ref, dtb_ref, alog_ref, dexp_ref,
                           eh_ref, gh_ref, a_ref, ex_ref, zs_ref, xw_ref, c_ref, b_ref, dec_ref,
                           *, d, g_n, n):
    ts, bs, _ = zx_ref.shape
    kw = cw_ref.shape[0]
    gn = g_n * n
    rows = [st_ref[j] for j in range(kw - 1)]
    rows += [zx_ref[t][:, d:].astype(F32) for t in range(ts)]
    eh = eh_ref[...]
    gh = gh_ref[...]
    a = -jnp.exp(alog_ref[...])
    xs, bm, cm, dts, acs = [], [], [], [], []
    run = None
    for t in range(ts):
        conv = cb_ref[...] + cw_ref[0:1, :] * rows[t]
        for k in range(1, kw):
            conv = conv + cw_ref[k:k + 1, :] * rows[t + k]
        xbc = _silu(conv)
        xs.append(xbc[:, :d])
        bm.append(xbc[:, d:d + gn])
        cm.append(xbc[:, d + gn:])
        dtt = _softplus(dt_ref[t] + dtb_ref[...])
        dts.append(dtt)
        run = dtt * a if run is None else run + dtt * a
        acs.append(run)
    alast = acs[-1]
    dec_ref[...] = jnp.exp(alast)
    for t in range(ts):
        acc = dexp_ref[...] * xs[t]
        for j in range(t + 1):
            cbh = _dot_parts(_split2(cm[t] * bm[j]), gh)
            coef = cbh * dts[j]
            if j < t:
                coef = coef * jnp.exp(acs[t] - acs[j])
            acc = acc + _dot_parts(_split2(coef), eh) * xs[j]
        a_ref[t] = acc
        ex_ref[t] = _dot_parts(_split2(jnp.exp(acs[t])), eh)
        zs_ref[t] = _silu(zx_ref[t][:, :d].astype(F32))
        xw_ref[t] = xs[t] * _dot_parts(_split2(jnp.exp(alast - acs[t]) * dts[t]), eh)
        c_ref[t] = cm[t]
        b_ref[t] = bm[t]


def _ssd_sample_pre(zx3, dt3, tblk, ts, sconv_t, layer, cw, cb, dtb, alog, dexp, eh, gh, d, g_n, n):
    bs = zx3.shape[1]
    gn = g_n * n
    kern = functools.partial(_ssd_sample_pre_kernel, d=d, g_n=g_n, n=n)
    sb = min(SAMPLE_PRE_BLOCK, bs)
    kw = cw.shape[0]
    const = lambda a: pl.BlockSpec(a.shape, lambda i: (0,) * a.ndim)
    seq3 = lambda c: pl.BlockSpec((ts, sb, c), lambda i: (0, i, 0))
    src3 = lambda c: pl.BlockSpec((ts, sb, c), lambda i: (tblk, i, 0))
    big = jax.ShapeDtypeStruct((ts, bs, d), F32)
    small = jax.ShapeDtypeStruct((ts, bs, gn), F32)
    return pl.pallas_call(
        kern,
        grid=(bs // sb,),
        in_specs=[src3(zx3.shape[2]), src3(LANES),
                  pl.BlockSpec((None, kw - 1, sb, sconv_t.shape[3]), lambda i: (layer, 0, i, 0)),
                  const(cw), const(cb), const(dtb), const(alog), const(dexp), const(eh), const(gh)],
        out_specs=[seq3(d), seq3(d), seq3(d), seq3(d), seq3(gn), seq3(gn),
                   pl.BlockSpec((sb, LANES), lambda i: (i, 0))],
        out_shape=[big, big, big, big, small, small, jax.ShapeDtypeStruct((bs, LANES), F32)],
        compiler_params=_cparams(("arbitrary",)),
        name="ssd_sample_pre",
    )(zx3, dt3, sconv_t, cw, cb, dtb, alog, dexp, eh, gh)


def _ssd_sample_state_kernel(dec_ref, s0_ref, a_ref, ex_ref, zs_ref, xwt_ref, c_ref, b_ref, ng_ref,
                             *rest, g_n, n, k_h, p, sb):
    y_ref, s1_ref = rest[-2:]
    i = pl.program_id(0)
    kp = k_h * p
    n_heads = g_n * k_h
    rows8 = sb * SUBLANES
    rid = lax.broadcasted_iota(jnp.int32, (rows8, 1), 0)
    ng = ng_ref[...]

    def body(s, carry):
        r0 = pl.multiple_of(s * SUBLANES, SUBLANES)
        in_s = (rid >= r0) & (rid < r0 + SUBLANES)
        for g in range(g_n):
            cg = c_ref[pl.ds(r0, SUBLANES), g * n:(g + 1) * n].astype(BF16)
            s_g = s0_ref[s, g * kp:(g + 1) * kp, :]
            yoff = _dot_nt(cg, s_g.astype(BF16))
            b_all = b_ref[:, g * n:(g + 1) * n]
            b_msk = jnp.where(in_s, b_all, 0.0).astype(BF16)
            upd = _dot(xwt_ref[g * kp:(g + 1) * kp, :].astype(BF16), b_msk)
            for hh in range(k_h):
                h = g * k_h + hh
                dsc = dec_ref[(i * sb + s) * n_heads + h]
                s1_ref[s, h * p:(h + 1) * p, :] = (dsc * s_g[hh * p:(hh + 1) * p, :]
                                                   + upd[hh * p:(hh + 1) * p, :])
            cols = slice(g * kp, (g + 1) * kp)
            yg = ((a_ref[pl.ds(r0, SUBLANES), cols] + ex_ref[pl.ds(r0, SUBLANES), cols] * yoff)
                  * zs_ref[pl.ds(r0, SUBLANES), cols])
            ms_ = jnp.mean(yg * yg, axis=-1, keepdims=True)
            y_ref[pl.ds(r0, SUBLANES), cols] = (yg * lax.rsqrt(ms_ + EPS) * ng[:, cols]).astype(y_ref.dtype)
        return carry

    lax.fori_loop(0, sb, body, 0)


def _ssd_sample_state(dec, s0_all, layer, s1_prev, a8, ex8, zs8, xwt, c8, b8, ng, g_n, n, k_h, p):
    depth, bs, hp, _ = s0_all.shape
    d = a8.shape[1]
    gn = g_n * n
    sb = min(SAMPLE_SEQ_BLOCK, bs)
    r8 = sb * SUBLANES
    kern = functools.partial(_ssd_sample_state_kernel, g_n=g_n, n=n, k_h=k_h, p=p, sb=sb)
    prev_specs = [] if s1_prev is None else [pl.BlockSpec(memory_space=pl.ANY)]
    prev_args = [] if s1_prev is None else [s1_prev]
    return pl.pallas_call(
        kern,
        grid=(bs // sb,),
        in_specs=[
            pl.BlockSpec(memory_space=pltpu.SMEM),
            pl.BlockSpec((None, sb, hp, n), lambda i: (layer, i, 0, 0)),
            pl.BlockSpec((r8, d), lambda i: (i, 0)),
            pl.BlockSpec((r8, d), lambda i: (i, 0)),
            pl.BlockSpec((r8, d), lambda i: (i, 0)),
            pl.BlockSpec((None, d, r8), lambda i: (i, 0, 0)),
            pl.BlockSpec((r8, gn), lambda i: (i, 0)),
            pl.BlockSpec((r8, gn), lambda i: (i, 0)),
            pl.BlockSpec((1, d), lambda i: (0, 0)),
        ] + prev_specs,
        out_specs=[
            pl.BlockSpec((r8, d), lambda i: (i, 0)),
            pl.BlockSpec((None, sb, hp, n), lambda i: (layer, i, 0, 0)),
        ],
        out_shape=[
            jax.ShapeDtypeStruct((bs * SUBLANES, d), F32),
            jax.ShapeDtypeStruct((depth, bs, hp, n), F32),
        ],
        input_output_aliases={} if s1_prev is None else {9: 1},
        compiler_params=_cparams(("arbitrary",)),
        name="ssd_sample_state",
    )(dec, s0_all, a8, ex8, zs8, xwt, c8, b8, ng, *prev_args)


def _ln_swish(v, g, b):
    mu = jnp.mean(v, axis=-1, keepdims=True)
    cen = v - mu
    var = jnp.mean(cen * cen, axis=-1, keepdims=True)
    return _silu(cen * lax.rsqrt(var + EPS) * g + b)


def _conf_prompt_kernel(u_ref, w_ref, b_ref, lg_ref, lb_ref, v_ref, upad, ush, *, halo):
    c = pl.program_id(1)
    L, d = u_ref.shape
    kw = w_ref.shape[0]
    span = L + halo - SUBLANES

    @pl.when(c == 0)
    def _():
        upad[0:halo, :] = jnp.zeros((halo, d), F32)

    u = u_ref[...].astype(F32)
    upad[halo:halo + L, :] = u
    for r in range(1, SUBLANES):
        ush[r - 1, 0:span, :] = upad[r:r + span, :]
    acc = jnp.broadcast_to(b_ref[...], (L, d))
    for k in range(kw):
        q, r = divmod(halo - (kw - 1) + k, SUBLANES)
        src = upad[q * SUBLANES:q * SUBLANES + L, :] if r == 0 else ush[r - 1, q * SUBLANES:q * SUBLANES + L, :]
        acc = acc + w_ref[k:k + 1, :] * src
    upad[0:halo, :] = u[L - halo:, :]
    v_ref[...] = _ln_swish(acc, lg_ref[...], lb_ref[...]).astype(v_ref.dtype)


def _conf_prompt(u, w, b, lg, lb, nb, t):
    d = u.shape[1]
    kw = w.shape[0]
    L = min(CONF_CHUNK, t)
    nc = t // L
    halo = -(-(kw - 1) // SUBLANES) * SUBLANES
    const = lambda a: pl.BlockSpec(a.shape, lambda bb, c: (0,) * a.ndim)
    return pl.pallas_call(
        functools.partial(_conf_prompt_kernel, halo=halo),
        grid=(nb, nc),
        in_specs=[pl.BlockSpec((L, d), lambda bb, c: (bb * nc + c, 0)), const(w), const(b), const(lg), const(lb)],
        out_specs=pl.BlockSpec((L, d), lambda bb, c: (bb * nc + c, 0)),
        out_shape=jax.ShapeDtypeStruct((nb * t, d), BF16),
        scratch_shapes=[pltpu.VMEM((L + halo, d), F32), pltpu.VMEM((SUBLANES - 1, L + halo, d), F32)],
        compiler_params=_cparams(("arbitrary", "arbitrary")),
        name="conf_prompt",
    )(u, w, b, lg, lb)


def _conf_sample_kernel(u_ref, st_ref, w_ref, b_ref, lg_ref, lb_ref, v_ref):
    ts, sb, d = u_ref.shape
    kw = w_ref.shape[0]
    rows = [st_ref[j] for j in range(kw - 1)]
    rows += [u_ref[t].astype(F32) for t in range(ts)]
    for t in range(ts):
        acc = b_ref[...] + w_ref[0:1, :] * rows[t]
        for k in range(1, kw):
            acc = acc + w_ref[k:k + 1, :] * rows[t + k]
        v_ref[t] = _ln_swish(acc, lg_ref[...], lb_ref[...]).astype(v_ref.dtype)


def _conf_sample(u3, tblk, ts, cconv_t, layer, w, b, lg, lb):
    _, bs, d = u3.shape
    kw = w.shape[0]
    sb = min(CONF_SAMPLE_BLOCK, bs)
    const = lambda a: pl.BlockSpec(a.shape, lambda i: (0,) * a.ndim)
    return pl.pallas_call(
        _conf_sample_kernel,
        grid=(bs // sb,),
        in_specs=[
            pl.BlockSpec((ts, sb, d), lambda i: (tblk, i, 0)),
            pl.BlockSpec((None, kw - 1, sb, d), lambda i: (layer, 0, i, 0)),
            const(w), const(b), const(lg), const(lb),
        ],
        out_specs=pl.BlockSpec((ts, sb, d), lambda i: (0, i, 0)),
        out_shape=jax.ShapeDtypeStruct((ts, bs, d), BF16),
        compiler_params=_cparams(("arbitrary",)),
        name="conf_sample",
    )(u3, cconv_t, w, b, lg, lb)


def _merge_kernel(yp_ref, ys_ref, vp_ref, vs_ref, ws_ref, wc_ref, gs_ref, gc_ref, o_ref, *, npt):
    is_p = pl.program_id(0) < npt
    y = jnp.where(is_p, yp_ref[...], ys_ref[...])
    v = jnp.where(is_p, vp_ref[...], vs_ref[...])
    a = _dot(y, ws_ref[...])
    b = _dot(v, wc_ref[...])
    o_ref[...] = (gs_ref[...].astype(F32) * a + gc_ref[...].astype(F32) * b).astype(o_ref.dtype)


def _merge(y_p, y_s, v_p, v_s, ws, wc, gates, tm):
    m, d = gates.shape[0], y_p.shape[1]
    npt = y_p.shape[0] // tm
    tn = _pick_tn(d)
    nj = d // tn
    p_spec = pl.BlockSpec((tm, d), lambda i, j: (jnp.minimum(i, npt - 1), 0))
    s_spec = pl.BlockSpec((tm, d), lambda i, j: (0, 0))
    return pl.pallas_call(
        functools.partial(_merge_kernel, npt=npt),
        grid=(m // tm, nj),
        in_specs=[
            p_spec, s_spec, p_spec, s_spec,
            pl.BlockSpec((d, tn), lambda i, j: (0, j)),
            pl.BlockSpec((d, tn), lambda i, j: (0, j)),
            pl.BlockSpec((tm, tn), lambda i, j: (i, j)),
            pl.BlockSpec((tm, tn), lambda i, j: (i, nj + j)),
        ],
        out_specs=pl.BlockSpec((tm, tn), lambda i, j: (i, j)),
        out_shape=jax.ShapeDtypeStruct((m, d), BF16),
        compiler_params=_cparams(("arbitrary", "arbitrary")),
        name="merge",
    )(y_p, y_s, v_p, v_s, ws, wc, gates, gates)


def _outproj_kernel(mx_ref, w_ref, x_ref, gp_ref, gs_ref, o_ref, *, npt, ts):
    gate = _tile_mod(pl.program_id(0), npt, ts, gp_ref, gs_ref)
    o_ref[...] = x_ref[...] + gate * _dot(mx_ref[...], w_ref[...])


def _outproj(mixed, wo, x, modp, mods, sec, dims):
    m, d = x.shape
    tm, npt, tps, nb, bs, ts = dims
    tn = _pick_tn(d)
    nj = d // tn
    colj = lambda s, i, j: s * nj + j
    return pl.pallas_call(
        functools.partial(_outproj_kernel, npt=npt, ts=ts),
        grid=(m // tm, nj),
        in_specs=[
            pl.BlockSpec((tm, d), lambda i, j: (i, 0)),
            pl.BlockSpec((d, tn), lambda i, j: (0, j)),
            pl.BlockSpec((tm, tn), lambda i, j: (i, j)),
            *_mod_specs(npt, tps, nb, bs, tn, sec, colj),
        ],
        out_specs=pl.BlockSpec((tm, tn), lambda i, j: (i, j)),
        out_shape=jax.ShapeDtypeStruct((m, d), F32),
        compiler_params=_cparams(("arbitrary", "arbitrary")),
        name="outproj",
    )(mixed, wo, x, modp, mods)


def _router_kernel(x_ref, g_ref, scp_ref, scs_ref, shp_ref, shs_ref, whi_ref, wlo_ref, rb_ref,
                   h_ref, gate_ref, *, npt, ts, n_e, n_g):
    i = pl.program_id(0)
    epg = n_e // n_g
    x = x_ref[...]
    xn = x * lax.rsqrt(jnp.mean(x * x, axis=-1, keepdims=True) + EPS) * g_ref[...]
    sc = _tile_mod(i, npt, ts, scp_ref, scs_ref)
    sh = _tile_mod(i, npt, ts, shp_ref, shs_ref)
    h = xn * (1.0 + sc) + sh
    hi, lo = _split2(h)
    h_ref[...] = hi
    logit = _dot(hi, whi_ref[...]) + _dot(lo, whi_ref[...]) + _dot(hi, wlo_ref[...]) + rb_ref[...]
    tm = logit.shape[0]
    lane = lax.broadcasted_iota(jnp.int32, (tm, LANES), 1).astype(F32)
    first = lambda hit: jnp.min(jnp.where(hit, lane, float(LANES)), axis=-1, keepdims=True)
    gl = jnp.where(lane >= n_e, jnp.where(lane < n_e + n_g, logit, NEG_BIG), NEG_BIG)
    gmax = jnp.max(gl, axis=-1, keepdims=True)
    g_top = 1.0 / jnp.sum(jnp.exp(gl - gmax), axis=-1, keepdims=True)
    g_idx = first(gl == gmax) - n_e
    el = jnp.where(lane >= g_idx * epg, jnp.where(lane < (g_idx + 1.0) * epg, logit, NEG_BIG), NEG_BIG)
    m1 = jnp.max(el, axis=-1, keepdims=True)
    i1 = first(el == m1)
    el2 = jnp.where(lane == i1, NEG_BIG, el)
    m2 = jnp.max(el2, axis=-1, keepdims=True)
    i2 = first(el2 == m2)
    r = jnp.exp(m2 - m1)
    w1 = 1.0 / (1.0 + r)
    w2 = r * w1
    gate_ref[...] = g_top * (jnp.where(lane == i1, w1, 0.0) + jnp.where(lane == i2, w2, 0.0))


def _router(x, gain, modp, mods, sec_sc, sec_sh, whi, wlo, rb, n_e, n_g, dims):
    m, d = x.shape
    tm, npt, tps, nb, bs, ts = dims
    col0 = lambda sec, i: sec
    return pl.pallas_call(
        functools.partial(_router_kernel, npt=npt, ts=ts, n_e=n_e, n_g=n_g),
        grid=(m // tm,),
        in_specs=[
            pl.BlockSpec((tm, d), lambda i: (i, 0)),
            pl.BlockSpec((1, d), lambda i: (0, 0)),
            *_mod_specs(npt, tps, nb, bs, d, sec_sc, col0),
            *_mod_specs(npt, tps, nb, bs, d, sec_sh, col0),
            pl.BlockSpec((d, LANES), lambda i: (0, 0)),
            pl.BlockSpec((d, LANES), lambda i: (0, 0)),
            pl.BlockSpec((1, LANES), lambda i: (0, 0)),
        ],
        out_specs=[pl.BlockSpec((tm, d), lambda i: (i, 0)), pl.BlockSpec((tm, LANES), lambda i: (i, 0))],
        out_shape=[jax.ShapeDtypeStruct((m, d), BF16), jax.ShapeDtypeStruct((m, LANES), F32)],
        compiler_params=_cparams(("arbitrary",)),
        name="router",
    )(x, gain, modp, mods, modp, mods, whi, wlo, rb)


def _moe_kernel(h_ref, gate_ref, wg_ref, wu_ref, wd_ref, x_ref, gp_ref, gs_ref, o_ref, acc, *, npt, ts):
    i = pl.program_id(0)
    e = pl.program_id(1)

    @pl.when(e == 0)
    def _():
        acc[...] = jnp.zeros(acc.shape, F32)

    h = h_ref[...]
    hid = _silu(_dot(h, wg_ref[...].astype(BF16))) * _dot(h, wu_ref[...].astype(BF16))
    gates = gate_ref[...]
    lane = lax.broadcasted_iota(jnp.int32, gates.shape, 1)
    gcol = jnp.sum(jnp.where(lane == e, gates, 0.0), axis=-1, keepdims=True)
    acc[...] += _dot((hid * gcol).astype(BF16), wd_ref[...].astype(BF16))

    @pl.when(e == pl.num_programs(1) - 1)
    def _():
        g2 = _tile_mod(i, npt, ts, gp_ref, gs_ref)
        o_ref[...] = x_ref[...] + g2 * acc[...]


def _moe(h2, gates, wg, wu, wd, layer, x, modp, mods, sec, dims):
    m, d = x.shape
    tm, npt, tps, nb, bs, ts = dims
    _, n_e, _, f = wg.shape
    col0 = lambda s, i, e: s
    return pl.pallas_call(
        functools.partial(_moe_kernel, npt=npt, ts=ts),
        grid=(m // tm, n_e),
        in_specs=[
            pl.BlockSpec((tm, d), lambda i, e: (i, 0)),
            pl.BlockSpec((tm, LANES), lambda i, e: (i, 0)),
            pl.BlockSpec((None, None, d, f), lambda i, e: (layer, e, 0, 0)),
            pl.BlockSpec((None, None, d, f), lambda i, e: (layer, e, 0, 0)),
            pl.BlockSpec((None, None, f, d), lambda i, e: (layer, e, 0, 0)),
            pl.BlockSpec((tm, d), lambda i, e: (i, 0)),
            *_mod_specs(npt, tps, nb, bs, d, sec, col0),
        ],
        out_specs=pl.BlockSpec((tm, d), lambda i, e: (i, 0)),
        out_shape=jax.ShapeDtypeStruct((m, d), F32),
        scratch_shapes=[pltpu.VMEM((tm, d), F32)],
        compiler_params=_cparams(("arbitrary", "arbitrary")),
        name="moe",
    )(h2, gates, wg, wu, wd, x, modp, mods)


def _final_norm_kernel(x_ref, g_ref, o_ref):
    x = x_ref[...]
    o_ref[...] = x * lax.rsqrt(jnp.mean(x * x, axis=-1, keepdims=True) + EPS) * g_ref[...]


def _final_norm(x, g, tm, tile0, ntiles):
    d = x.shape[1]
    return pl.pallas_call(
        _final_norm_kernel,
        grid=(ntiles,),
        in_specs=[pl.BlockSpec((tm, d), lambda i: (tile0 + i, 0)), pl.BlockSpec((1, d), lambda i: (0, 0))],
        out_specs=pl.BlockSpec((tm, d), lambda i: (i, 0)),
        out_shape=jax.ShapeDtypeStruct((ntiles * tm, d), F32),
        compiler_params=_cparams(("arbitrary",)),
        name="final_norm",
    )(x, g)


def _to_seq8(a):
    ts, bs, c = a.shape
    a = jnp.transpose(a, (1, 0, 2))
    a = jnp.pad(a, ((0, 0), (0, SUBLANES - ts), (0, 0)))
    return a.reshape(bs * SUBLANES, c)


def kernel(x_prompt, x_sample, c_prompt, c_sample, state_ssm, state_ssd_conv, state_conf_conv, norm_mix_g, norm_ffn_g, w_ada, b_ada, w_in, ssd_conv_w, ssd_conv_b, ssd_dt_bias, ssd_a_log, ssd_d, ssd_norm_g, w_ssd_out, conf_conv_w, conf_conv_b, conf_ln_g, conf_ln_b, w_conf_out, w_o, w_group_router, b_group_router, w_expert_router, b_expert_router, w_exp_gate, w_exp_up, w_exp_down, final_norm_g):
    nb, t, d = x_prompt.shape
    bs, ts, _ = x_sample.shape
    depth = w_in.shape[0]
    n_h = ssd_dt_bias.shape[1]
    p = d // n_h
    n = state_ssm.shape[-1]
    cc = ssd_conv_w.shape[-1]
    g_n = (cc - d) // (2 * n)
    k_h = n_h // g_n
    gn = g_n * n
    kc = conf_conv_w.shape[1]
    n_e = w_exp_gate.shape[1]
    n_g = w_group_router.shape[-1]
    tm = bs * ts
    assert t % tm == 0 and tm % SUBLANES == 0 and bs % SUBLANES == 0 and ts <= SUBLANES
    assert 2 * p == LANES and k_h % 2 == 0 and n_h <= LANES and n_e + n_g <= LANES
    npt = nb * t // tm
    dims = (tm, npt, t // tm, nb, bs, ts)
    mp = nb * t

    x = jnp.concatenate([x_prompt.reshape(mp, d), jnp.transpose(x_sample, (1, 0, 2)).reshape(tm, d)], axis=0)

    rc = -(-(nb + bs) // SUBLANES) * SUBLANES
    c_all = jnp.concatenate([c_prompt, c_sample, jnp.zeros((rc - nb - bs, d), F32)], axis=0)
    mod = _adaln(c_all, w_ada, b_ada)

    head_of = jnp.arange(d) // p
    eh = (jnp.arange(LANES)[:, None] == head_of[None, :]).astype(BF16)
    gh = ((jnp.arange(gn)[:, None] // n) == (jnp.arange(LANES)[None, :] // k_h)).astype(BF16)
    gh = gh * (jnp.arange(LANES)[None, :] < n_h).astype(BF16)
    pad_h = lambda v: jnp.pad(v, (0, LANES - n_h)).reshape(1, LANES)

    m = mp + tm
    tblk = mp // bs // ts
    assert tblk * bs * ts == mp
    sconv_t = jnp.transpose(state_ssd_conv, (0, 2, 1, 3))
    cconv_t = jnp.transpose(state_conf_conv, (0, 2, 1, 3))
    ssm0_all = state_ssm.reshape(depth, bs, n_h * p, n)
    ssm_s = None
    ssm_p, sconv_p, cconv_p, sconv_s, cconv_s = [], [], [], [], []
    for l in range(depth):
        modp = mod[l, :nb].reshape(nb, 1, 6 * d)
        mods = mod[l, nb:nb + bs]
        wl = w_in[l]
        o_z, o_x, o_dt, o_glu, o_gs = 0, d, d + cc, d + cc + n_h, d + cc + n_h + 2 * d
        w_zx = wl[:, :o_dt].astype(BF16)
        w_dt = jnp.pad(wl[:, o_dt:o_glu], ((0, 0), (0, LANES - n_h))).astype(BF16)
        w_ga = wl[:, o_glu:o_glu + d].astype(BF16)
        w_gg = wl[:, o_glu + d:o_gs].astype(BF16)
        w_gates = wl[:, o_gs:].astype(BF16)
        g_mix = norm_mix_g[l].reshape(1, d)
        proj = functools.partial(_normproj, x, g_mix, modp, mods, 1, 0, dims=dims)
        zx = proj([w_zx], BF16, "plain")
        dtraw = proj([w_dt], F32, "plain")
        u = proj([w_ga, w_gg], BF16, "glu")
        gates = proj([w_gates], BF16, "sigmoid")

        cw = ssd_conv_w[l]
        cb = ssd_conv_b[l].reshape(1, cc)
        dtb = pad_h(ssd_dt_bias[l])
        alog = pad_h(ssd_a_log[l])
        dexp = jnp.repeat(ssd_d[l], p).reshape(1, d)
        ng = ssd_norm_g[l].reshape(1, d)

        y_p, s_p = _ssd_prompt(zx, dtraw, cw, cb, dtb, alog, dexp, ng, eh, nb, t, n_h * p, n, g_n, k_h, p)
        zx_s = zx[mp:].reshape(ts, bs, d + cc)
        sconv0 = state_ssd_conv[l]
        a_t, ex_t, zs_t, xw_t, c_t, b_t, dec = _ssd_sample_pre(
            zx.reshape(m // bs, bs, d + cc), dtraw.reshape(m // bs, bs, LANES), tblk, ts, sconv_t, l,
            cw, cb, dtb, alog, dexp, eh, gh, d, g_n, n)
        sb = min(SAMPLE_SEQ_BLOCK, bs)
        xwt = _to_seq8(xw_t).reshape(bs // sb, sb * SUBLANES, d).transpose(0, 2, 1)
        dec = dec[:, :n_h].reshape(bs * n_h)
        y_s8, ssm_s = _ssd_sample_state(dec, ssm0_all, l, ssm_s, _to_seq8(a_t), _to_seq8(ex_t),
                                        _to_seq8(zs_t), xwt, _to_seq8(c_t), _to_seq8(b_t), ng, g_n, n, k_h, p)
        y_s = jnp.transpose(y_s8.reshape(bs, SUBLANES, d)[:, :ts], (1, 0, 2)).reshape(tm, d).astype(BF16)

        ccw = conf_conv_w[l]
        ccb = conf_conv_b[l].reshape(1, d)
        lg = conf_ln_g[l].reshape(1, d)
        lb = conf_ln_b[l].reshape(1, d)
        v_p = _conf_prompt(u, ccw, ccb, lg, lb, nb, t)
        u_s = u[mp:].reshape(ts, bs, d)
        cconv0 = state_conf_conv[l]
        v_s = _conf_sample(u.reshape(m // bs, bs, d), tblk, ts, cconv_t, l, ccw, ccb, lg, lb).reshape(tm, d)

        mixed = _merge(y_p, y_s, v_p, v_s, w_ssd_out[l].astype(BF16), w_conf_out[l].astype(BF16), gates, tm)
        x = _outproj(mixed, w_o[l].astype(BF16), x, modp, mods, 2, dims)

        w_r = jnp.concatenate([w_expert_router[l], w_group_router[l],
                               jnp.zeros((d, LANES - n_e - n_g), F32)], axis=1)
        w_r_hi = w_r.astype(BF16)
        w_r_lo = (w_r - w_r_hi.astype(F32)).astype(BF16)
        rb = jnp.concatenate([b_expert_router[l], b_group_router[l], jnp.zeros((LANES - n_e - n_g,), F32)]).reshape(1, LANES)
        h2, gate_w = _router(x, norm_ffn_g[l].reshape(1, d), modp, mods, 4, 3, w_r_hi, w_r_lo, rb, n_e, n_g, dims)
        x = _moe(h2, gate_w, w_exp_gate, w_exp_up, w_exp_down, l, x, modp, mods, 5, dims)

        ssm_p.append(s_p.reshape(nb, n_h, p, n))
        xraw_p = zx[:mp, d:].reshape(nb, t, cc)
        sconv_p.append(xraw_p[:, t - (sconv0.shape[1]):].astype(F32))
        cconv_p.append(u[:mp].reshape(nb, t, d)[:, t - (kc - 1):].astype(F32))
        xraw_s = jnp.transpose(zx_s[:, :, d:], (1, 0, 2)).astype(F32)
        sconv_s.append(jnp.concatenate([sconv0, xraw_s], axis=1)[:, ts:])
        cconv_s.append(jnp.concatenate([cconv0, jnp.transpose(u_s, (1, 0, 2)).astype(F32)], axis=1)[:, ts:])

    fg = final_norm_g.reshape(1, d)
    y_prompt = _final_norm(x, fg, tm, 0, npt).reshape(nb, t, d)
    y_sample = jnp.transpose(_final_norm(x, fg, tm, npt, 1).reshape(ts, bs, d), (1, 0, 2))
    return (y_prompt, y_sample, jnp.stack(ssm_p), jnp.stack(sconv_p), jnp.stack(cconv_p),
            ssm_s.reshape(depth, bs, n_h, p, n), jnp.stack(sconv_s), jnp.stack(cconv_s))
```

```python
import functools
import math

import jax
import jax.numpy as jnp
from jax import lax
from jax.experimental import pallas as pl
from jax.experimental.pallas import tpu as pltpu

F32 = jnp.float32
BF16 = jnp.bfloat16
EPS = 1e-6
LANES = 128
SUBLANES = 8
VMEM_LIMIT = 52 * 1024 * 1024
NEG_BIG = -1e30
SSD_CHUNK = 128
CONF_CHUNK = 256
SAMPLE_SEQ_BLOCK = 8
SAMPLE_PRE_BLOCK = 32
CONF_SAMPLE_BLOCK = 16
MOE_TG = 256
DISPATCH_CHUNK = 256
ADA_TN = 1024
PROJ_TN = 1024


def _pick_tn(ncols):
    assert ncols % LANES == 0
    return max(tn for tn in range(LANES, min(PROJ_TN, ncols) + 1, LANES) if ncols % tn == 0)


def _cparams(sem):
    return pltpu.CompilerParams(dimension_semantics=sem, vmem_limit_bytes=VMEM_LIMIT)


def _silu(x):
    return x * jax.nn.sigmoid(x)


def _softplus(x):
    return jnp.maximum(x, 0.0) + jnp.log1p(jnp.exp(-jnp.abs(x)))


def _split2(x):
    hi = x.astype(BF16)
    lo = (x - hi.astype(F32)).astype(BF16)
    return hi, lo


def _split3(x):
    hi = x.astype(BF16)
    r = x - hi.astype(F32)
    mid = r.astype(BF16)
    lo = (r - mid.astype(F32)).astype(BF16)
    return hi, mid, lo


def _dot(a, b):
    return jnp.dot(a, b, preferred_element_type=F32)


def _dot_nt(a, b):
    return lax.dot_general(a, b, (((1,), (1,)), ((), ())), preferred_element_type=F32)


def _dot_parts(parts, w):
    acc = _dot(parts[0], w)
    for p in parts[1:]:
        acc = acc + _dot(p, w)
    return acc


def _adaln_kernel(c_ref, w_ref, b_ref, o_ref):
    s = _silu(c_ref[...]).astype(BF16)
    o_ref[...] = _dot(s, w_ref[...].astype(BF16)) + b_ref[...]


def _adaln(c_all, w_ada, b_ada):
    depth, d, n6 = w_ada.shape
    rc = c_all.shape[0]
    tn = min(ADA_TN, n6)
    return pl.pallas_call(
        _adaln_kernel,
        grid=(depth, n6 // tn),
        in_specs=[
            pl.BlockSpec((rc, d), lambda l, j: (0, 0)),
            pl.BlockSpec((None, d, tn), lambda l, j: (l, 0, j)),
            pl.BlockSpec((None, 1, tn), lambda l, j: (l, 0, j)),
        ],
        out_specs=pl.BlockSpec((None, rc, tn), lambda l, j: (l, 0, j)),
        out_shape=jax.ShapeDtypeStruct((depth, rc, n6), F32),
        compiler_params=_cparams(("arbitrary", "arbitrary")),
        name="adaln",
    )(c_all, w_ada, b_ada.reshape(depth, 1, n6))


def _tile_mod(i, npt, ts, p_ref, s_ref):
    ms = s_ref[...]
    tiled = jnp.concatenate([ms] * ts, axis=0)
    return jnp.where(i < npt, p_ref[...], tiled)


def _mod_specs(npt, tiles_per_seq, nseq_p, bs, width, sec, col_of):
    def p_map(*idx):
        i = idx[0]
        return (jnp.minimum(i // tiles_per_seq, nseq_p - 1), 0, col_of(sec, *idx))

    def s_map(*idx):
        return (0, col_of(sec, *idx))

    return [pl.BlockSpec((None, 1, width), p_map), pl.BlockSpec((bs, width), s_map)]


def _normproj_kernel(x_ref, g_ref, scp_ref, scs_ref, shp_ref, shs_ref, *rest, npt, ts, epilogue):
    nw = 2 if epilogue == "glu" else 1
    w_refs = rest[:nw]
    o_ref = rest[nw]
    h_scr = rest[nw + 1]
    i = pl.program_id(0)

    @pl.when(pl.program_id(1) == 0)
    def _():
        x = x_ref[...]
        xn = x * lax.rsqrt(jnp.mean(x * x, axis=-1, keepdims=True) + EPS) * g_ref[...]
        sc = _tile_mod(i, npt, ts, scp_ref, scs_ref)
        sh = _tile_mod(i, npt, ts, shp_ref, shs_ref)
        h_scr[...] = (xn * (1.0 + sc) + sh).astype(BF16)

    h = h_scr[...]
    acc = _dot(h, w_refs[0][...])
    if epilogue == "glu":
        acc = acc * jax.nn.sigmoid(_dot(h, w_refs[1][...]))
    elif epilogue == "sigmoid":
        acc = jax.nn.sigmoid(acc)
    o_ref[...] = acc.astype(o_ref.dtype)


def _normproj(x, gain, modp, mods, sec_sc, sec_sh, ws, out_dtype, epilogue, dims):
    m, d = x.shape
    tm, npt, tps, nb, bs, ts = dims
    ncols = ws[0].shape[1]
    tn = _pick_tn(ncols)
    col0 = lambda sec, i, j: sec
    in_specs = [
        pl.BlockSpec((tm, d), lambda i, j: (i, 0)),
        pl.BlockSpec((1, d), lambda i, j: (0, 0)),
        *_mod_specs(npt, tps, nb, bs, d, sec_sc, col0),
        *_mod_specs(npt, tps, nb, bs, d, sec_sh, col0),
    ] + [pl.BlockSpec((d, tn), lambda i, j: (0, j)) for _ in ws]
    return pl.pallas_call(
        functools.partial(_normproj_kernel, npt=npt, ts=ts, epilogue=epilogue),
        grid=(m // tm, ncols // tn),
        in_specs=in_specs,
        out_specs=pl.BlockSpec((tm, tn), lambda i, j: (i, j)),
        out_shape=jax.ShapeDtypeStruct((m, ncols), out_dtype),
        scratch_shapes=[pltpu.VMEM((tm, d), BF16)],
        compiler_params=_cparams(("arbitrary", "arbitrary")),
        name="normproj_" + epilogue,
    )(x, gain, modp, mods, modp, mods, *ws)


def _ssd_prompt_kernel(zx_ref, dt_ref, cw_ref, cb_ref, dtb_ref, alog_ref, dexp_ref, ng_ref, eh_ref,
                       y_ref, st_ref, xpad, y_scr, *, d, g_n, n, k_h, p):
    c = pl.program_id(1)
    L = zx_ref.shape[0]
    kp = k_h * p
    cc = cw_ref.shape[1]
    kw = cw_ref.shape[0]

    @pl.when(c == 0)
    def _():
        xpad[0:SUBLANES, :] = jnp.zeros((SUBLANES, cc), F32)
        st_ref[...] = jnp.zeros(st_ref.shape, F32)

    zx = zx_ref[...]
    z = zx[:, :d].astype(F32)
    xr = zx[:, d:].astype(F32)
    xpad[SUBLANES:SUBLANES + L, :] = xr
    conv = jnp.broadcast_to(cb_ref[...], (L, cc))
    for k in range(kw):
        off = SUBLANES - (kw - 1) + k
        conv = conv + cw_ref[k:k + 1, :] * xpad[off:off + L, :]
    xpad[0:SUBLANES, :] = xr[L - SUBLANES:, :]
    xbc = _silu(conv)
    xs = xbc[:, :d]
    bm = xbc[:, d:d + g_n * n]
    cm = xbc[:, d + g_n * n:]

    dt = _softplus(dt_ref[...] + dtb_ref[...])
    a = -jnp.exp(alog_ref[...])
    dta = dt * a
    row = lax.broadcasted_iota(jnp.int32, (L, L), 0)
    col = lax.broadcasted_iota(jnp.int32, (L, L), 1)
    causal = row >= col
    tril = jnp.where(causal, 1.0, 0.0).astype(BF16)
    a_hi, a_mid, a_lo = _split3(dta)
    acum = _dot(tril, a_hi) + _dot(tril, a_mid) + _dot(tril, a_lo)
    acum_t = acum.T
    dt_t = dt.T
    alast = acum[L - 1:L, :]
    wend = jnp.exp(alast - acum) * dt
    eac = jnp.exp(acum)
    eh = eh_ref[...]
    wend_x = _dot_parts(_split2(wend), eh)
    eac_x = _dot_parts(_split2(eac), eh)
    xw = xs * wend_x
    dec = jnp.exp(alast)
    lane = lax.broadcasted_iota(jnp.int32, (L, 2 * p), 1)

    for g in range(g_n):
        bg = bm[:, g * n:(g + 1) * n].astype(BF16)
        cg = cm[:, g * n:(g + 1) * n].astype(BF16)
        cbm = _dot_nt(cg, bg)
        s_g = st_ref[g * kp:(g + 1) * kp, :]
        yoff = _dot_nt(cg, s_g.astype(BF16))
        for j in range(k_h // 2):
            h0 = g * k_h + 2 * j
            ms = []
            for h in (h0, h0 + 1):
                seg = acum[:, h:h + 1] - acum_t[h:h + 1, :]
                dcy = jnp.exp(jnp.where(causal, seg, NEG_BIG))
                ms.append((cbm * dcy * dt_t[h:h + 1, :]).astype(BF16))
            lhs = jnp.concatenate(ms, axis=1)
            xp = xs[:, h0 * p:(h0 + 2) * p]
            rhs = jnp.concatenate([jnp.where(lane < p, xp, 0.0), jnp.where(lane >= p, xp, 0.0)],
                                  axis=0).astype(BF16)
            lo = h0 * p
            y_scr[:, lo:lo + 2 * p] = (_dot(lhs, rhs)
                                       + yoff[:, 2 * j * p:(2 * j + 2) * p] * eac_x[:, lo:lo + 2 * p])
        upd = _dot(xw[:, g * kp:(g + 1) * kp].T.astype(BF16), bg)
        for hh in range(k_h):
            h = g * k_h + hh
            r0 = h * p
            dsc = jnp.broadcast_to(dec[:, h:h + 1], (p, n))
            st_ref[r0:r0 + p, :] = dsc * s_g[hh * p:(hh + 1) * p, :] + upd[hh * p:(hh + 1) * p, :]

    y = (y_scr[...] + dexp_ref[...] * xs) * _silu(z)
    ng = ng_ref[...]
    for g in range(g_n):
        yg = y[:, g * kp:(g + 1) * kp]
        ms_ = jnp.mean(yg * yg, axis=-1, keepdims=True)
        y_ref[:, g * kp:(g + 1) * kp] = (yg * lax.rsqrt(ms_ + EPS) * ng[:, g * kp:(g + 1) * kp]).astype(y_ref.dtype)


def _ssd_prompt(zx, dtraw, cw, cb, dtb, alog, dexp, ng, eh, nb, t, hp, n, g_n, k_h, p):
    d = hp
    L = min(SSD_CHUNK, t)
    nc = t // L
    cc = cw.shape[1]
    kern = functools.partial(_ssd_prompt_kernel, d=d, g_n=g_n, n=n, k_h=k_h, p=p)
    const = lambda shape: pl.BlockSpec(shape, lambda b, c: (0,) * len(shape))
    return pl.pallas_call(
        kern,
        grid=(nb, nc),
        in_specs=[
            pl.BlockSpec((L, d + cc), lambda b, c: (b * nc + c, 0)),
            pl.BlockSpec((L, LANES), lambda b, c: (b * nc + c, 0)),
            const(cw.shape), const(cb.shape), const(dtb.shape), const(alog.shape),
            const(dexp.shape), const(ng.shape), const(eh.shape),
        ],
        out_specs=[
            pl.BlockSpec((L, d), lambda b, c: (b * nc + c, 0)),
            pl.BlockSpec((None, hp, n), lambda b, c: (b, 0, 0)),
        ],
        out_shape=[
            jax.ShapeDtypeStruct((nb * t, d), BF16),
            jax.ShapeDtypeStruct((nb, hp, n), F32),
        ],
        scratch_shapes=[pltpu.VMEM((L + SUBLANES, cc), F32), pltpu.VMEM((L, d), F32)],
        compiler_params=_cparams(("arbitrary", "arbitrary")),
        name="ssd_prompt",
    )(zx, dtraw, cw, cb, dtb, alog, dexp, ng, eh)


def _ssd_sample_pre_kernel(zx_ref, dt_ref, st_ref, cw_ref, cb_ref, dtb_ref, alog_ref, dexp_ref,
                           eh_ref, gh_ref, a_ref, ex_ref, zs_ref, xw_ref, c_ref, b_ref, dec_ref,
                           *, d, g_n, n):
    ts, bs, _ = zx_ref.shape
    kw = cw_ref.shape[0]
    gn = g_n * n
    rows = [st_ref[j] for j in range(kw - 1)]
    rows += [zx_ref[t][:, d:].astype(F32) for t in range(ts)]
    eh = eh_ref[...]
    gh = gh_ref[...]
    a = -jnp.exp(alog_ref[...])
    xs, bm, cm, dts, acs = [], [], [], [], []
    run = None
    for t in range(ts):
        conv = cb_ref[...] + cw_ref[0:1, :] * rows[t]
        for k in range(1, kw):
            conv = conv + cw_ref[k:k + 1, :] * rows[t + k]
        xbc = _silu(conv)
        xs.append(xbc[:, :d])
        bm.append(xbc[:, d:d + gn])
        cm.append(xbc[:, d + gn:])
        dtt = _softplus(dt_ref[t] + dtb_ref[...])
        dts.append(dtt)
        run = dtt * a if run is None else run + dtt * a
        acs.append(run)
    alast = acs[-1]
    dec_ref[...] = jnp.exp(alast)
    for t in range(ts):
        acc = dexp_ref[...] * xs[t]
        for j in range(t + 1):
            cbh = _dot_parts(_split2(cm[t] * bm[j]), gh)
            coef = cbh * dts[j]
            if j < t:
                coef = coef * jnp.exp(acs[t] - acs[j])
            acc = acc + _dot_parts(_split2(coef), eh) * xs[j]
        a_ref[t] = acc
        ex_ref[t] = _dot_parts(_split2(jnp.exp(acs[t])), eh)
        zs_ref[t] = _silu(zx_ref[t][:, :d].astype(F32))
        xw_ref[t] = xs[t] * _dot_parts(_split2(jnp.exp(alast - acs[t]) * dts[t]), eh)
        c_ref[t] = cm[t]
        b_ref[t] = bm[t]


def _ssd_sample_pre(zx3, dt3, tblk, ts, sconv_t, layer, cw, cb, dtb, alog, dexp, eh, gh, d, g_n, n):
    bs = zx3.shape[1]
    gn = g_n * n
    kern = functools.partial(_ssd_sample_pre_kernel, d=d, g_n=g_n, n=n)
    sb = min(SAMPLE_PRE_BLOCK, bs)
    kw = cw.shape[0]
    const = lambda a: pl.BlockSpec(a.shape, lambda i: (0,) * a.ndim)
    seq3 = lambda c: pl.BlockSpec((ts, sb, c), lambda i: (0, i, 0))
    src3 = lambda c: pl.BlockSpec((ts, sb, c), lambda i: (tblk, i, 0))
    big = jax.ShapeDtypeStruct((ts, bs, d), F32)
    small = jax.ShapeDtypeStruct((ts, bs, gn), F32)
    return pl.pallas_call(
        kern,
        grid=(bs // sb,),
        in_specs=[src3(zx3.shape[2]), src3(LANES),
                  pl.BlockSpec((None, kw - 1, sb, sconv_t.shape[3]), lambda i: (layer, 0, i, 0)),
                  const(cw), const(cb), const(dtb), const(alog), const(dexp), const(eh), const(gh)],
        out_specs=[seq3(d), seq3(d), seq3(d), seq3(d), seq3(gn), seq3(gn),
                   pl.BlockSpec((sb, LANES), lambda i: (i, 0))],
        out_shape=[big, big, big, big, small, small, jax.ShapeDtypeStruct((bs, LANES), F32)],
        compiler_params=_cparams(("arbitrary",)),
        name="ssd_sample_pre",
    )(zx3, dt3, sconv_t, cw, cb, dtb, alog, dexp, eh, gh)


def _ssd_sample_state_kernel(dec_ref, s0_ref, a_ref, ex_ref, zs_ref, xwt_ref, c_ref, b_ref, ng_ref,
                             prev_ref, y_ref, s1_ref, *, g_n, n, k_h, p, sb):
    del prev_ref
    i = pl.program_id(0)
    kp = k_h * p
    n_heads = g_n * k_h
    rows8 = sb * SUBLANES
    rid = lax.broadcasted_iota(jnp.int32, (rows8, 1), 0)
    ng = ng_ref[...]

    def body(s, carry):
        r0 = pl.multiple_of(s * SUBLANES, SUBLANES)
        in_s = (rid >= r0) & (rid < r0 + SUBLANES)
        for g in range(g_n):
            cg = c_ref[pl.ds(r0, SUBLANES), g * n:(g + 1) * n].astype(BF16)
            s_g = s0_ref[s, g * kp:(g + 1) * kp, :]
            yoff = _dot_nt(cg, s_g.astype(BF16))
            b_all = b_ref[:, g * n:(g + 1) * n]
            b_msk = jnp.where(in_s, b_all, 0.0).astype(BF16)
            upd = _dot(xwt_ref[g * kp:(g + 1) * kp, :].astype(BF16), b_msk)
            for hh in range(k_h):
                h = g * k_h + hh
                dsc = dec_ref[(i * sb + s) * n_heads + h]
                s1_ref[s, h * p:(h + 1) * p, :] = (dsc * s_g[hh * p:(hh + 1) * p, :]
                                                   + upd[hh * p:(hh + 1) * p, :])
            cols = slice(g * kp, (g + 1) * kp)
            yg = ((a_ref[pl.ds(r0, SUBLANES), cols] + ex_ref[pl.ds(r0, SUBLANES), cols] * yoff)
                  * zs_ref[pl.ds(r0, SUBLANES), cols])
            ms_ = jnp.mean(yg * yg, axis=-1, keepdims=True)
            y_ref[pl.ds(r0, SUBLANES), cols] = (yg * lax.rsqrt(ms_ + EPS) * ng[:, cols]).astype(y_ref.dtype)
        return carry

    lax.fori_loop(0, sb, body, 0)


def _ssd_sample_state(dec, s0_all, layer, s1_prev, a8, ex8, zs8, xwt, c8, b8, ng, g_n, n, k_h, p):
    depth, bs, hp, _ = s0_all.shape
    d = a8.shape[1]
    gn = g_n * n
    sb = min(SAMPLE_SEQ_BLOCK, bs)
    r8 = sb * SUBLANES
    kern = functools.partial(_ssd_sample_state_kernel, g_n=g_n, n=n, k_h=k_h, p=p, sb=sb)
    return pl.pallas_call(
        kern,
        grid=(bs // sb,),
        in_specs=[
            pl.BlockSpec(memory_space=pltpu.SMEM),
            pl.BlockSpec((None, sb, hp, n), lambda i: (layer, i, 0, 0)),
            pl.BlockSpec((r8, d), lambda i: (i, 0)),
            pl.BlockSpec((r8, d), lambda i: (i, 0)),
            pl.BlockSpec((r8, d), lambda i: (i, 0)),
            pl.BlockSpec((None, d, r8), lambda i: (i, 0, 0)),
            pl.BlockSpec((r8, gn), lambda i: (i, 0)),
            pl.BlockSpec((r8, gn), lambda i: (i, 0)),
            pl.BlockSpec((1, d), lambda i: (0, 0)),
            pl.BlockSpec(memory_space=pl.ANY),
        ],
        out_specs=[
            pl.BlockSpec((r8, d), lambda i: (i, 0)),
            pl.BlockSpec((None, sb, hp, n), lambda i: (layer, i, 0, 0)),
        ],
        out_shape=[
            jax.ShapeDtypeStruct((bs * SUBLANES, d), F32),
            jax.ShapeDtypeStruct((depth, bs, hp, n), F32),
        ],
        input_output_aliases={9: 1},
        compiler_params=_cparams(("arbitrary",)),
        name="ssd_sample_state",
    )(dec, s0_all, a8, ex8, zs8, xwt, c8, b8, ng, s1_prev)


def _ln_swish(v, g, b):
    mu = jnp.mean(v, axis=-1, keepdims=True)
    cen = v - mu
    var = jnp.mean(cen * cen, axis=-1, keepdims=True)
    return _silu(cen * lax.rsqrt(var + EPS) * g + b)


def _conf_prompt_kernel(u_ref, w_ref, b_ref, lg_ref, lb_ref, v_ref, upad, ush, *, halo):
    c = pl.program_id(1)
    L, d = u_ref.shape
    kw = w_ref.shape[0]
    span = L + halo - SUBLANES

    @pl.when(c == 0)
    def _():
        upad[0:halo, :] = jnp.zeros((halo, d), F32)

    u = u_ref[...].astype(F32)
    upad[halo:halo + L, :] = u
    for r in range(1, SUBLANES):
        ush[r - 1, 0:span, :] = upad[r:r + span, :]
    acc = jnp.broadcast_to(b_ref[...], (L, d))
    for k in range(kw):
        q, r = divmod(halo - (kw - 1) + k, SUBLANES)
        src = upad[q * SUBLANES:q * SUBLANES + L, :] if r == 0 else ush[r - 1, q * SUBLANES:q * SUBLANES + L, :]
        acc = acc + w_ref[k:k + 1, :] * src
    upad[0:halo, :] = u[L - halo:, :]
    v_ref[...] = _ln_swish(acc, lg_ref[...], lb_ref[...]).astype(v_ref.dtype)


def _conf_prompt(u, w, b, lg, lb, nb, t):
    d = u.shape[1]
    kw = w.shape[0]
    L = min(CONF_CHUNK, t)
    nc = t // L
    halo = -(-(kw - 1) // SUBLANES) * SUBLANES
    const = lambda a: pl.BlockSpec(a.shape, lambda bb, c: (0,) * a.ndim)
    return pl.pallas_call(
        functools.partial(_conf_prompt_kernel, halo=halo),
        grid=(nb, nc),
        in_specs=[pl.BlockSpec((L, d), lambda bb, c: (bb * nc + c, 0)), const(w), const(b), const(lg), const(lb)],
        out_specs=pl.BlockSpec((L, d), lambda bb, c: (bb * nc + c, 0)),
        out_shape=jax.ShapeDtypeStruct((nb * t, d), BF16),
        scratch_shapes=[pltpu.VMEM((L + halo, d), F32), pltpu.VMEM((SUBLANES - 1, L + halo, d), F32)],
        compiler_params=_cparams(("arbitrary", "arbitrary")),
        name="conf_prompt",
    )(u, w, b, lg, lb)


def _conf_sample_kernel(u_ref, st_ref, w_ref, b_ref, lg_ref, lb_ref, v_ref):
    ts, sb, d = u_ref.shape
    kw = w_ref.shape[0]
    rows = [st_ref[j] for j in range(kw - 1)]
    rows += [u_ref[t].astype(F32) for t in range(ts)]
    for t in range(ts):
        acc = b_ref[...] + w_ref[0:1, :] * rows[t]
        for k in range(1, kw):
            acc = acc + w_ref[k:k + 1, :] * rows[t + k]
        v_ref[t] = _ln_swish(acc, lg_ref[...], lb_ref[...]).astype(v_ref.dtype)


def _conf_sample(u3, tblk, ts, cconv_t, layer, w, b, lg, lb):
    _, bs, d = u3.shape
    kw = w.shape[0]
    sb = min(CONF_SAMPLE_BLOCK, bs)
    const = lambda a: pl.BlockSpec(a.shape, lambda i: (0,) * a.ndim)
    return pl.pallas_call(
        _conf_sample_kernel,
        grid=(bs // sb,),
        in_specs=[
            pl.BlockSpec((ts, sb, d), lambda i: (tblk, i, 0)),
            pl.BlockSpec((None, kw - 1, sb, d), lambda i: (layer, 0, i, 0)),
            const(w), const(b), const(lg), const(lb),
        ],
        out_specs=pl.BlockSpec((ts, sb, d), lambda i: (0, i, 0)),
        out_shape=jax.ShapeDtypeStruct((ts, bs, d), BF16),
        compiler_params=_cparams(("arbitrary",)),
        name="conf_sample",
    )(u3, cconv_t, w, b, lg, lb)


def _merge_kernel(yp_ref, ys_ref, vp_ref, vs_ref, ws_ref, wc_ref, gs_ref, gc_ref, o_ref, *, npt):
    is_p = pl.program_id(0) < npt
    y = jnp.where(is_p, yp_ref[...], ys_ref[...])
    v = jnp.where(is_p, vp_ref[...], vs_ref[...])
    a = _dot(y, ws_ref[...])
    b = _dot(v, wc_ref[...])
    o_ref[...] = (gs_ref[...].astype(F32) * a + gc_ref[...].astype(F32) * b).astype(o_ref.dtype)


def _merge(y_p, y_s, v_p, v_s, ws, wc, gates, tm):
    m, d = gates.shape[0], y_p.shape[1]
    npt = y_p.shape[0] // tm
    tn = _pick_tn(d)
    nj = d // tn
    p_spec = pl.BlockSpec((tm, d), lambda i, j: (jnp.minimum(i, npt - 1), 0))
    s_spec = pl.BlockSpec((tm, d), lambda i, j: (0, 0))
    return pl.pallas_call(
        functools.partial(_merge_kernel, npt=npt),
        grid=(m // tm, nj),
        in_specs=[
            p_spec, s_spec, p_spec, s_spec,
            pl.BlockSpec((d, tn), lambda i, j: (0, j)),
            pl.BlockSpec((d, tn), lambda i, j: (0, j)),
            pl.BlockSpec((tm, tn), lambda i, j: (i, j)),
            pl.BlockSpec((tm, tn), lambda i, j: (i, nj + j)),
        ],
        out_specs=pl.BlockSpec((tm, tn), lambda i, j: (i, j)),
        out_shape=jax.ShapeDtypeStruct((m, d), BF16),
        compiler_params=_cparams(("arbitrary", "arbitrary")),
        name="merge",
    )(y_p, y_s, v_p, v_s, ws, wc, gates, gates)


def _outproj_kernel(mx_ref, w_ref, x_ref, gp_ref, gs_ref, o_ref, *, npt, ts):
    gate = _tile_mod(pl.program_id(0), npt, ts, gp_ref, gs_ref)
    o_ref[...] = x_ref[...] + gate * _dot(mx_ref[...], w_ref[...])


def _outproj(mixed, wo, x, modp, mods, sec, dims):
    m, d = x.shape
    tm, npt, tps, nb, bs, ts = dims
    tn = _pick_tn(d)
    nj = d // tn
    colj = lambda s, i, j: s * nj + j
    return pl.pallas_call(
        functools.partial(_outproj_kernel, npt=npt, ts=ts),
        grid=(m // tm, nj),
        in_specs=[
            pl.BlockSpec((tm, d), lambda i, j: (i, 0)),
            pl.BlockSpec((d, tn), lambda i, j: (0, j)),
            pl.BlockSpec((tm, tn), lambda i, j: (i, j)),
            *_mod_specs(npt, tps, nb, bs, tn, sec, colj),
        ],
        out_specs=pl.BlockSpec((tm, tn), lambda i, j: (i, j)),
        out_shape=jax.ShapeDtypeStruct((m, d), F32),
        compiler_params=_cparams(("arbitrary", "arbitrary")),
        name="outproj",
    )(mixed, wo, x, modp, mods)


def _pack_bf16_pairs(v):
    half = v.shape[1] // 2
    bits = pltpu.bitcast(v.astype(BF16).astype(F32), jnp.uint32)
    return bits[:, :half] | (bits[:, half:] >> 16)


def _unpack_bf16_pairs(w):
    left = pltpu.bitcast(w & jnp.uint32(0xFFFF0000), F32)
    right = pltpu.bitcast(w << 16, F32)
    return jnp.concatenate([left, right], axis=1)


INFO_RANK1, INFO_RANK2, INFO_E1, INFO_E2, INFO_W1, INFO_W2 = range(6)


def _router_kernel(x_ref, g_ref, scp_ref, scs_ref, shp_ref, shs_ref, whi_ref, wlo_ref, rb_ref,
                   hp_ref, info_ref, cnt_ref, *, npt, ts, n_e, n_g):
    i = pl.program_id(0)
    epg = n_e // n_g

    @pl.when(i == 0)
    def _():
        cnt_ref[...] = jnp.zeros(cnt_ref.shape, F32)

    x = x_ref[...]
    xn = x * lax.rsqrt(jnp.mean(x * x, axis=-1, keepdims=True) + EPS) * g_ref[...]
    sc = _tile_mod(i, npt, ts, scp_ref, scs_ref)
    sh = _tile_mod(i, npt, ts, shp_ref, shs_ref)
    h = xn * (1.0 + sc) + sh
    hi, lo = _split2(h)
    hp_ref[...] = _pack_bf16_pairs(h)
    logit = _dot(hi, whi_ref[...]) + _dot(lo, whi_ref[...]) + _dot(hi, wlo_ref[...]) + rb_ref[...]
    tm = logit.shape[0]
    lane = lax.broadcasted_iota(jnp.int32, (tm, LANES), 1).astype(F32)
    first = lambda hit: jnp.min(jnp.where(hit, lane, float(LANES)), axis=-1, keepdims=True)
    gl = jnp.where(lane >= n_e, jnp.where(lane < n_e + n_g, logit, NEG_BIG), NEG_BIG)
    gmax = jnp.max(gl, axis=-1, keepdims=True)
    g_top = 1.0 / jnp.sum(jnp.exp(gl - gmax), axis=-1, keepdims=True)
    g_idx = first(gl == gmax) - n_e
    el = jnp.where(lane >= g_idx * epg, jnp.where(lane < (g_idx + 1.0) * epg, logit, NEG_BIG), NEG_BIG)
    m1 = jnp.max(el, axis=-1, keepdims=True)
    i1 = first(el == m1)
    el2 = jnp.where(lane == i1, NEG_BIG, el)
    m2 = jnp.max(el2, axis=-1, keepdims=True)
    i2 = first(el2 == m2)
    r = jnp.exp(m2 - m1)
    w1 = 1.0 / (1.0 + r)
    w2 = r * w1
    oh1 = jnp.where(lane == i1, 1.0, 0.0)
    oh2 = jnp.where(lane == i2, 1.0, 0.0)
    both = oh1 + oh2
    row = lax.broadcasted_iota(jnp.int32, (tm, tm), 0)
    col = lax.broadcasted_iota(jnp.int32, (tm, tm), 1)
    before = _dot(jnp.where(row > col, 1.0, 0.0).astype(BF16), both.astype(BF16)) + cnt_ref[0:1, :]
    rank1 = jnp.sum(before * oh1, axis=-1, keepdims=True)
    rank2 = jnp.sum(before * oh2, axis=-1, keepdims=True)
    cnt_ref[...] = cnt_ref[...] + jnp.sum(both, axis=0, keepdims=True)
    rec = [rank1, rank2, i1, i2, g_top * w1, g_top * w2]
    info = jnp.zeros((tm, LANES), F32)
    for k, val in enumerate(rec):
        info = jnp.where(lane == float(k), val, info)
    info_ref[...] = info


def _router(x, gain, modp, mods, sec_sc, sec_sh, whi, wlo, rb, n_e, n_g, dims):
    m, d = x.shape
    tm, npt, tps, nb, bs, ts = dims
    col0 = lambda sec, i: sec
    return pl.pallas_call(
        functools.partial(_router_kernel, npt=npt, ts=ts, n_e=n_e, n_g=n_g),
        grid=(m // tm,),
        in_specs=[
            pl.BlockSpec((tm, d), lambda i: (i, 0)),
            pl.BlockSpec((1, d), lambda i: (0, 0)),
            *_mod_specs(npt, tps, nb, bs, d, sec_sc, col0),
            *_mod_specs(npt, tps, nb, bs, d, sec_sh, col0),
            pl.BlockSpec((d, LANES), lambda i: (0, 0)),
            pl.BlockSpec((d, LANES), lambda i: (0, 0)),
            pl.BlockSpec((1, LANES), lambda i: (0, 0)),
        ],
        out_specs=[pl.BlockSpec((tm, d // 2), lambda i: (i, 0)), pl.BlockSpec((tm, LANES), lambda i: (i, 0)),
                   pl.BlockSpec((SUBLANES, LANES), lambda i: (0, 0))],
        out_shape=[jax.ShapeDtypeStruct((m, d // 2), jnp.uint32), jax.ShapeDtypeStruct((m, LANES), F32),
                   jax.ShapeDtypeStruct((SUBLANES, LANES), F32)],
        compiler_params=_cparams(("arbitrary",)),
        name="router",
    )(x, gain, modp, mods, modp, mods, whi, wlo, rb)


def _moe_plan(info, cnt, n_e, tg, nt_max):
    counts = cnt[0, :n_e].astype(jnp.int32)
    tiles = (counts + tg - 1) // tg
    tile_end = jnp.cumsum(tiles)
    offs = (tile_end - tiles) * tg
    n_used = tile_end[n_e - 1]
    e1 = info[:, INFO_E1].astype(jnp.int32)
    e2 = info[:, INFO_E2].astype(jnp.int32)
    pos1 = offs[e1] + info[:, INFO_RANK1].astype(jnp.int32)
    pos2 = offs[e2] + info[:, INFO_RANK2].astype(jnp.int32)
    tile = jnp.minimum(jnp.arange(nt_max, dtype=jnp.int32), n_used - 1)
    tile_expert = jnp.searchsorted(tile_end, tile, side="right").astype(jnp.int32)
    return pos1, pos2, tile_expert, n_used.reshape(1)


def _dispatch_kernel(pos1_ref, pos2_ref, hp_ref, xs0_ref, xs_ref, sem, *, m, chunk):
    del xs0_ref

    def row_copy(t, slot):
        return pltpu.make_async_copy(hp_ref.at[pl.ds(t, 1)], xs_ref.at[pl.ds(slot, 1)], sem)

    def issue(c):
        def body(r, carry):
            t = c * chunk + r
            row_copy(t, pos1_ref[t]).start()
            row_copy(t, pos2_ref[t]).start()
            return carry
        lax.fori_loop(0, chunk, body, 0)

    def drain():
        def body(r, carry):
            row_copy(0, 0).wait()
            row_copy(0, 0).wait()
            return carry
        lax.fori_loop(0, chunk, body, 0)

    issue(0)

    def outer(c, carry):
        issue(c)
        drain()
        return carry

    lax.fori_loop(1, m // chunk, outer, 0)
    drain()


def _dispatch(pos1, pos2, hp, n_slots):
    m, half = hp.shape
    chunk = math.gcd(m, DISPATCH_CHUNK)
    smem = pl.BlockSpec(memory_space=pltpu.SMEM)
    hbm = pl.BlockSpec(memory_space=pl.ANY)
    return pl.pallas_call(
        functools.partial(_dispatch_kernel, m=m, chunk=chunk),
        in_specs=[smem, smem, hbm, hbm],
        out_specs=hbm,
        out_shape=jax.ShapeDtypeStruct((n_slots, half), jnp.uint32),
        scratch_shapes=[pltpu.SemaphoreType.DMA(())],
        input_output_aliases={3: 0},
        name="moe_dispatch",
    )(pos1, pos2, hp, jnp.zeros((n_slots, half), jnp.uint32))


def _experts_kernel(te_ref, nu_ref, xs_ref, wg_ref, wu_ref, wd_ref, ys_ref):
    del te_ref
    i = pl.program_id(0)

    @pl.when(i < nu_ref[0])
    def _():
        x = _unpack_bf16_pairs(xs_ref[...]).astype(BF16)
        hid = _silu(_dot(x, wg_ref[...].astype(BF16))) * _dot(x, wu_ref[...].astype(BF16))
        ys_ref[...] = _pack_bf16_pairs(_dot(hid.astype(BF16), wd_ref[...].astype(BF16)))

    @pl.when(i >= nu_ref[0])
    def _():
        ys_ref[...] = jnp.zeros(ys_ref.shape, jnp.uint32)


def _experts(tile_expert, n_used, xs, wg, wu, wd, layer, tg):
    n_slots, half = xs.shape
    _, _, d, f = wg.shape
    grid_spec = pltpu.PrefetchScalarGridSpec(
        num_scalar_prefetch=2,
        grid=(n_slots // tg,),
        in_specs=[
            pl.BlockSpec((tg, half), lambda i, te, nu: (i, 0)),
            pl.BlockSpec((None, None, d, f), lambda i, te, nu: (layer, te[i], 0, 0)),
            pl.BlockSpec((None, None, d, f), lambda i, te, nu: (layer, te[i], 0, 0)),
            pl.BlockSpec((None, None, f, d), lambda i, te, nu: (layer, te[i], 0, 0)),
        ],
        out_specs=pl.BlockSpec((tg, half), lambda i, te, nu: (i, 0)),
    )
    return pl.pallas_call(
        _experts_kernel,
        grid_spec=grid_spec,
        out_shape=jax.ShapeDtypeStruct((n_slots, half), jnp.uint32),
        compiler_params=_cparams(("arbitrary",)),
        name="moe_experts",
    )(tile_expert, n_used, xs, wg, wu, wd)


def _combine_kernel(pos1_ref, pos2_ref, info_ref, x_ref, gp_ref, gs_ref, ys_ref, o_ref, buf_a, buf_b, sem,
                    *, npt, ts):
    i = pl.program_id(0)
    tm = x_ref.shape[0]
    slot = lax.rem(i, 2)

    def issue(tile, s):
        def body(r, carry):
            t = tile * tm + r
            pltpu.make_async_copy(ys_ref.at[pl.ds(pos1_ref[t], 1)], buf_a.at[s, pl.ds(r, 1)], sem.at[s]).start()
            pltpu.make_async_copy(ys_ref.at[pl.ds(pos2_ref[t], 1)], buf_b.at[s, pl.ds(r, 1)], sem.at[s]).start()
            return carry
        lax.fori_loop(0, tm, body, 0)

    @pl.when(i == 0)
    def _():
        issue(0, 0)

    @pl.when(i + 1 < pl.num_programs(0))
    def _():
        issue(i + 1, 1 - slot)

    def wait_body(r, carry):
        pltpu.make_async_copy(ys_ref.at[pl.ds(0, 1)], buf_a.at[slot, pl.ds(0, 1)], sem.at[slot]).wait()
        pltpu.make_async_copy(ys_ref.at[pl.ds(0, 1)], buf_b.at[slot, pl.ds(0, 1)], sem.at[slot]).wait()
        return carry
    lax.fori_loop(0, tm, wait_body, 0)

    info = info_ref[...]
    w1 = info[:, INFO_W1:INFO_W1 + 1]
    w2 = info[:, INFO_W2:INFO_W2 + 1]
    moe = w1 * _unpack_bf16_pairs(buf_a[slot]) + w2 * _unpack_bf16_pairs(buf_b[slot])
    o_ref[...] = x_ref[...] + _tile_mod(i, npt, ts, gp_ref, gs_ref) * moe


def _combine(pos1, pos2, info, x, modp, mods, sec, ys, dims):
    m, d = x.shape
    tm, npt, tps, nb, bs, ts = dims
    col0 = lambda s, i: s
    smem = pl.BlockSpec(memory_space=pltpu.SMEM)
    return pl.pallas_call(
        functools.partial(_combine_kernel, npt=npt, ts=ts),
        grid=(m // tm,),
        in_specs=[
            smem, smem,
            pl.BlockSpec((tm, LANES), lambda i: (i, 0)),
            pl.BlockSpec((tm, d), lambda i: (i, 0)),
            *_mod_specs(npt, tps, nb, bs, d, sec, col0),
            pl.BlockSpec(memory_space=pl.ANY),
        ],
        out_specs=pl.BlockSpec((tm, d), lambda i: (i, 0)),
        out_shape=jax.ShapeDtypeStruct((m, d), F32),
        scratch_shapes=[pltpu.VMEM((2, tm, d // 2), jnp.uint32), pltpu.VMEM((2, tm, d // 2), jnp.uint32),
                        pltpu.SemaphoreType.DMA((2,))],
        compiler_params=_cparams(("arbitrary",)),
        name="moe_combine",
    )(pos1, pos2, info, x, modp, mods, ys)


def _moe(hp, info, cnt, wg, wu, wd, layer, x, modp, mods, sec, dims):
    m = x.shape[0]
    n_e = wg.shape[1]
    tg = MOE_TG
    nt_max = -(-2 * m // tg) + n_e
    pos1, pos2, tile_expert, n_used = _moe_plan(info, cnt, n_e, tg, nt_max)
    xs = _dispatch(pos1, pos2, hp, nt_max * tg)
    ys = _experts(tile_expert, n_used, xs, wg, wu, wd, layer, tg)
    return _combine(pos1, pos2, info, x, modp, mods, sec, ys, dims)


def _final_norm_kernel(x_ref, g_ref, o_ref):
    x = x_ref[...]
    o_ref[...] = x * lax.rsqrt(jnp.mean(x * x, axis=-1, keepdims=True) + EPS) * g_ref[...]


def _final_norm(x, g, tm, tile0, ntiles):
    d = x.shape[1]
    return pl.pallas_call(
        _final_norm_kernel,
        grid=(ntiles,),
        in_specs=[pl.BlockSpec((tm, d), lambda i: (tile0 + i, 0)), pl.BlockSpec((1, d), lambda i: (0, 0))],
        out_specs=pl.BlockSpec((tm, d), lambda i: (i, 0)),
        out_shape=jax.ShapeDtypeStruct((ntiles * tm, d), F32),
        compiler_params=_cparams(("arbitrary",)),
        name="final_norm",
    )(x, g)


def _to_seq8(a):
    ts, bs, c = a.shape
    a = jnp.transpose(a, (1, 0, 2))
    a = jnp.pad(a, ((0, 0), (0, SUBLANES - ts), (0, 0)))
    return a.reshape(bs * SUBLANES, c)


def kernel(x_prompt, x_sample, c_prompt, c_sample, state_ssm, state_ssd_conv, state_conf_conv, norm_mix_g, norm_ffn_g, w_ada, b_ada, w_in, ssd_conv_w, ssd_conv_b, ssd_dt_bias, ssd_a_log, ssd_d, ssd_norm_g, w_ssd_out, conf_conv_w, conf_conv_b, conf_ln_g, conf_ln_b, w_conf_out, w_o, w_group_router, b_group_router, w_expert_router, b_expert_router, w_exp_gate, w_exp_up, w_exp_down, final_norm_g):
    nb, t, d = x_prompt.shape
    bs, ts, _ = x_sample.shape
    depth = w_in.shape[0]
    n_h = ssd_dt_bias.shape[1]
    p = d // n_h
    n = state_ssm.shape[-1]
    cc = ssd_conv_w.shape[-1]
    g_n = (cc - d) // (2 * n)
    k_h = n_h // g_n
    gn = g_n * n
    kc = conf_conv_w.shape[1]
    n_e = w_exp_gate.shape[1]
    n_g = w_group_router.shape[-1]
    tm = bs * ts
    assert t % tm == 0 and tm % SUBLANES == 0 and bs % SUBLANES == 0 and ts <= SUBLANES
    assert 2 * p == LANES and k_h % 2 == 0 and n_h <= LANES and n_e + n_g <= LANES
    npt = nb * t // tm
    dims = (tm, npt, t // tm, nb, bs, ts)
    mp = nb * t

    x = jnp.concatenate([x_prompt.reshape(mp, d), jnp.transpose(x_sample, (1, 0, 2)).reshape(tm, d)], axis=0)

    rc = -(-(nb + bs) // SUBLANES) * SUBLANES
    c_all = jnp.concatenate([c_prompt, c_sample, jnp.zeros((rc - nb - bs, d), F32)], axis=0)
    mod = _adaln(c_all, w_ada, b_ada)

    head_of = jnp.arange(d) // p
    eh = (jnp.arange(LANES)[:, None] == head_of[None, :]).astype(BF16)
    gh = ((jnp.arange(gn)[:, None] // n) == (jnp.arange(LANES)[None, :] // k_h)).astype(BF16)
    gh = gh * (jnp.arange(LANES)[None, :] < n_h).astype(BF16)
    pad_h = lambda v: jnp.pad(v, (0, LANES - n_h)).reshape(1, LANES)

    m = mp + tm
    tblk = mp // bs // ts
    assert tblk * bs * ts == mp
    sconv_t = jnp.transpose(state_ssd_conv, (0, 2, 1, 3))
    cconv_t = jnp.transpose(state_conf_conv, (0, 2, 1, 3))
    ssm0_all = state_ssm.reshape(depth, bs, n_h * p, n)
    ssm_s = jnp.zeros(ssm0_all.shape, F32)
    ssm_p, sconv_p, cconv_p, sconv_s, cconv_s = [], [], [], [], []
    for l in range(depth):
        modp = mod[l, :nb].reshape(nb, 1, 6 * d)
        mods = mod[l, nb:nb + bs]
        wl = w_in[l]
        o_z, o_x, o_dt, o_glu, o_gs = 0, d, d + cc, d + cc + n_h, d + cc + n_h + 2 * d
        w_zx = wl[:, :o_dt].astype(BF16)
        w_dt = jnp.pad(wl[:, o_dt:o_glu], ((0, 0), (0, LANES - n_h))).astype(BF16)
        w_ga = wl[:, o_glu:o_glu + d].astype(BF16)
        w_gg = wl[:, o_glu + d:o_gs].astype(BF16)
        w_gates = wl[:, o_gs:].astype(BF16)
        g_mix = norm_mix_g[l].reshape(1, d)
        proj = functools.partial(_normproj, x, g_mix, modp, mods, 1, 0, dims=dims)
        zx = proj([w_zx], BF16, "plain")
        dtraw = proj([w_dt], F32, "plain")
        u = proj([w_ga, w_gg], BF16, "glu")
        gates = proj([w_gates], BF16, "sigmoid")

        cw = ssd_conv_w[l]
        cb = ssd_conv_b[l].reshape(1, cc)
        dtb = pad_h(ssd_dt_bias[l])
        alog = pad_h(ssd_a_log[l])
        dexp = jnp.repeat(ssd_d[l], p).reshape(1, d)
        ng = ssd_norm_g[l].reshape(1, d)

        y_p, s_p = _ssd_prompt(zx, dtraw, cw, cb, dtb, alog, dexp, ng, eh, nb, t, n_h * p, n, g_n, k_h, p)
        zx_s = zx[mp:].reshape(ts, bs, d + cc)
        sconv0 = state_ssd_conv[l]
        a_t, ex_t, zs_t, xw_t, c_t, b_t, dec = _ssd_sample_pre(
            zx.reshape(m // bs, bs, d + cc), dtraw.reshape(m // bs, bs, LANES), tblk, ts, sconv_t, l,
            cw, cb, dtb, alog, dexp, eh, gh, d, g_n, n)
        sb = min(SAMPLE_SEQ_BLOCK, bs)
        xwt = _to_seq8(xw_t).reshape(bs // sb, sb * SUBLANES, d).transpose(0, 2, 1)
        dec = dec[:, :n_h].reshape(bs * n_h)
        y_s8, ssm_s = _ssd_sample_state(dec, ssm0_all, l, ssm_s, _to_seq8(a_t), _to_seq8(ex_t),
                                        _to_seq8(zs_t), xwt, _to_seq8(c_t), _to_seq8(b_t), ng, g_n, n, k_h, p)
        y_s = jnp.transpose(y_s8.reshape(bs, SUBLANES, d)[:, :ts], (1, 0, 2)).reshape(tm, d).astype(BF16)

        ccw = conf_conv_w[l]
        ccb = conf_conv_b[l].reshape(1, d)
        lg = conf_ln_g[l].reshape(1, d)
        lb = conf_ln_b[l].reshape(1, d)
        v_p = _conf_prompt(u, ccw, ccb, lg, lb, nb, t)
        u_s = u[mp:].reshape(ts, bs, d)
        cconv0 = state_conf_conv[l]
        v_s = _conf_sample(u.reshape(m // bs, bs, d), tblk, ts, cconv_t, l, ccw, ccb, lg, lb).reshape(tm, d)

        mixed = _merge(y_p, y_s, v_p, v_s, w_ssd_out[l].astype(BF16), w_conf_out[l].astype(BF16), gates, tm)
        x = _outproj(mixed, w_o[l].astype(BF16), x, modp, mods, 2, dims)

        w_r = jnp.concatenate([w_expert_router[l], w_group_router[l],
                               jnp.zeros((d, LANES - n_e - n_g), F32)], axis=1)
        w_r_hi = w_r.astype(BF16)
        w_r_lo = (w_r - w_r_hi.astype(F32)).astype(BF16)
        rb = jnp.concatenate([b_expert_router[l], b_group_router[l], jnp.zeros((LANES - n_e - n_g,), F32)]).reshape(1, LANES)
        hp, info, cnt = _router(x, norm_ffn_g[l].reshape(1, d), modp, mods, 4, 3, w_r_hi, w_r_lo, rb, n_e, n_g, dims)
        x = _moe(hp, info, cnt, w_exp_gate, w_exp_up, w_exp_down, l, x, modp, mods, 5, dims)

        ssm_p.append(s_p.reshape(nb, n_h, p, n))
        xraw_p = zx[:mp, d:].reshape(nb, t, cc)
        sconv_p.append(xraw_p[:, t - (sconv0.shape[1]):].astype(F32))
        cconv_p.append(u[:mp].reshape(nb, t, d)[:, t - (kc - 1):].astype(F32))
        xraw_s = jnp.transpose(zx_s[:, :, d:], (1, 0, 2)).astype(F32)
        sconv_s.append(jnp.concatenate([sconv0, xraw_s], axis=1)[:, ts:])
        cconv_s.append(jnp.concatenate([cconv0, jnp.transpose(u_s, (1, 0, 2)).astype(F32)], axis=1)[:, ts:])

    fg = final_norm_g.reshape(1, d)
    y_prompt = _final_norm(x, fg, tm, 0, npt).reshape(nb, t, d)
    y_sample = jnp.transpose(_final_norm(x, fg, tm, npt, 1).reshape(ts, bs, d), (1, 0, 2))
    return (y_prompt, y_sample, jnp.stack(ssm_p), jnp.stack(sconv_p), jnp.stack(cconv_p),
            ssm_s.reshape(depth, bs, n_h, p, n), jnp.stack(sconv_s), jnp.stack(cconv_s))
```

```python
import functools

import jax
import jax.numpy as jnp
from jax import lax
from jax.experimental import pallas as pl
from jax.experimental.pallas import tpu as pltpu

F32 = jnp.float32
BF16 = jnp.bfloat16
EPS = 1e-6
LANES = 128
SUBLANES = 8
VMEM_LIMIT = 52 * 1024 * 1024
NEG_BIG = -1e30
SSD_CHUNK = 128
CONF_CHUNK = 256
SAMPLE_SEQ_BLOCK = 8
SAMPLE_PRE_BLOCK = 32
CONF_SAMPLE_BLOCK = 16
MOE_TG = 256
ROW_DMA_UNROLL = 8
ADA_TN = 1024
PROJ_TN = 1024
PROJ_TM = 1088


def _pick_tn(ncols):
    assert ncols % LANES == 0
    return max(tn for tn in range(LANES, min(PROJ_TN, ncols) + 1, LANES) if ncols % tn == 0)


def _pick_rows(m):
    return max(tm for tm in range(2 * SUBLANES, min(PROJ_TM, m) + 1, 2 * SUBLANES) if m % tm == 0)


def _cparams(sem):
    return pltpu.CompilerParams(dimension_semantics=sem, vmem_limit_bytes=VMEM_LIMIT)


def _silu(x):
    return x * jax.nn.sigmoid(x)


def _softplus(x):
    return jnp.maximum(x, 0.0) + jnp.log1p(jnp.exp(-jnp.abs(x)))


def _split2(x):
    hi = x.astype(BF16)
    lo = (x - hi.astype(F32)).astype(BF16)
    return hi, lo


def _split3(x):
    hi = x.astype(BF16)
    r = x - hi.astype(F32)
    mid = r.astype(BF16)
    lo = (r - mid.astype(F32)).astype(BF16)
    return hi, mid, lo


def _dot(a, b):
    return jnp.dot(a, b, preferred_element_type=F32)


def _dot_nt(a, b):
    return lax.dot_general(a, b, (((1,), (1,)), ((), ())), preferred_element_type=F32)


def _dot_parts(parts, w):
    acc = _dot(parts[0], w)
    for p in parts[1:]:
        acc = acc + _dot(p, w)
    return acc


def _adaln_kernel(c_ref, w_ref, b_ref, o_ref):
    s = _silu(c_ref[...]).astype(BF16)
    o_ref[...] = _dot(s, w_ref[...].astype(BF16)) + b_ref[...]


def _adaln(c_all, w_ada, b_ada):
    depth, d, n6 = w_ada.shape
    rc = c_all.shape[0]
    tn = min(ADA_TN, n6)
    return pl.pallas_call(
        _adaln_kernel,
        grid=(depth, n6 // tn),
        in_specs=[
            pl.BlockSpec((rc, d), lambda l, j: (0, 0)),
            pl.BlockSpec((None, d, tn), lambda l, j: (l, 0, j)),
            pl.BlockSpec((None, 1, tn), lambda l, j: (l, 0, j)),
        ],
        out_specs=pl.BlockSpec((None, rc, tn), lambda l, j: (l, 0, j)),
        out_shape=jax.ShapeDtypeStruct((depth, rc, n6), F32),
        compiler_params=_cparams(("arbitrary", "arbitrary")),
        name="adaln",
    )(c_all, w_ada, b_ada.reshape(depth, 1, n6))


def _tile_mod(i, npt, ts, p_ref, s_ref):
    ms = s_ref[...]
    tiled = jnp.concatenate([ms] * ts, axis=0)
    return jnp.where(i < npt, p_ref[...], tiled)


def _mod_specs(npt, tiles_per_seq, nseq_p, bs, width, sec, col_of):
    def p_map(*idx):
        i = idx[0]
        return (jnp.minimum(i // tiles_per_seq, nseq_p - 1), 0, col_of(sec, *idx))

    def s_map(*idx):
        return (0, col_of(sec, *idx))

    return [pl.BlockSpec((None, 1, width), p_map), pl.BlockSpec((bs, width), s_map)]


def _modnorm_kernel(x_ref, g_ref, scp_ref, scs_ref, shp_ref, shs_ref, h_ref, *, npt, ts):
    i = pl.program_id(0)
    x = x_ref[...]
    xn = x * lax.rsqrt(jnp.mean(x * x, axis=-1, keepdims=True) + EPS) * g_ref[...]
    sc = _tile_mod(i, npt, ts, scp_ref, scs_ref)
    sh = _tile_mod(i, npt, ts, shp_ref, shs_ref)
    h_ref[...] = (xn * (1.0 + sc) + sh).astype(h_ref.dtype)


def _modnorm(x, gain, modp, mods, sec_sc, sec_sh, dims):
    m, d = x.shape
    tm, npt, tps, nb, bs, ts = dims
    col0 = lambda sec, i: sec
    return pl.pallas_call(
        functools.partial(_modnorm_kernel, npt=npt, ts=ts),
        grid=(m // tm,),
        in_specs=[
            pl.BlockSpec((tm, d), lambda i: (i, 0)),
            pl.BlockSpec((1, d), lambda i: (0, 0)),
            *_mod_specs(npt, tps, nb, bs, d, sec_sc, col0),
            *_mod_specs(npt, tps, nb, bs, d, sec_sh, col0),
        ],
        out_specs=pl.BlockSpec((tm, d), lambda i: (i, 0)),
        out_shape=jax.ShapeDtypeStruct((m, d), BF16),
        compiler_params=_cparams(("arbitrary",)),
        name="modnorm",
    )(x, gain, modp, mods, modp, mods)


def _proj_kernel(h_ref, *rest, nw, cast_w, epilogue):
    w_refs = rest[:nw]
    o_ref = rest[nw]
    if cast_w:
        w_bf = rest[nw + 1:]

        @pl.when(pl.program_id(1) == 0)
        def _():
            for w_ref, scr in zip(w_refs, w_bf):
                scr[...] = w_ref[...].astype(BF16)
        ws = [scr[...] for scr in w_bf]
    else:
        ws = [w_ref[...] for w_ref in w_refs]
    h = h_ref[...]
    acc = _dot(h, ws[0])
    if epilogue == "glu":
        acc = acc * jax.nn.sigmoid(_dot(h, ws[1]))
    elif epilogue == "sigmoid":
        acc = jax.nn.sigmoid(acc)
    o_ref[...] = acc.astype(o_ref.dtype)


def _proj(h, ws, out_dtype, epilogue, layer=None, ncols=None):
    m, d = h.shape
    cast_w = ws[0].dtype != BF16
    ncols = ws[0].shape[-1] if ncols is None else ncols
    tn = _pick_tn(ncols)
    tm = _pick_rows(m)
    if cast_w:
        w_specs = [pl.BlockSpec((None, d, tn), lambda j, i: (layer, 0, j)) for _ in ws]
        scratch = [pltpu.VMEM((d, tn), BF16) for _ in ws]
    else:
        w_specs = [pl.BlockSpec((d, tn), lambda j, i: (0, j)) for _ in ws]
        scratch = []
    return pl.pallas_call(
        functools.partial(_proj_kernel, nw=len(ws), cast_w=cast_w, epilogue=epilogue),
        grid=(ncols // tn, m // tm),
        in_specs=[pl.BlockSpec((tm, d), lambda j, i: (i, 0))] + w_specs,
        out_specs=pl.BlockSpec((tm, tn), lambda j, i: (i, j)),
        out_shape=jax.ShapeDtypeStruct((m, ncols), out_dtype),
        scratch_shapes=scratch,
        compiler_params=_cparams(("arbitrary", "arbitrary")),
        name="proj_" + epilogue,
    )(h, *ws)


def _ssd_prompt_kernel(zx_ref, dt_ref, cw_ref, cb_ref, dtb_ref, alog_ref, dexp_ref, ng_ref, eh_ref,
                       y_ref, st_ref, xpad, y_scr, *, d, g_n, n, k_h, p):
    c = pl.program_id(1)
    L = zx_ref.shape[0]
    kp = k_h * p
    cc = cw_ref.shape[1]
    kw = cw_ref.shape[0]

    @pl.when(c == 0)
    def _():
        xpad[0:SUBLANES, :] = jnp.zeros((SUBLANES, cc), F32)
        st_ref[...] = jnp.zeros(st_ref.shape, F32)

    zx = zx_ref[...]
    z = zx[:, :d].astype(F32)
    xr = zx[:, d:].astype(F32)
    xpad[SUBLANES:SUBLANES + L, :] = xr
    conv = jnp.broadcast_to(cb_ref[...], (L, cc))
    for k in range(kw):
        off = SUBLANES - (kw - 1) + k
        conv = conv + cw_ref[k:k + 1, :] * xpad[off:off + L, :]
    xpad[0:SUBLANES, :] = xr[L - SUBLANES:, :]
    xbc = _silu(conv)
    xs = xbc[:, :d]
    bm = xbc[:, d:d + g_n * n]
    cm = xbc[:, d + g_n * n:]

    dt = _softplus(dt_ref[...] + dtb_ref[...])
    a = -jnp.exp(alog_ref[...])
    dta = dt * a
    row = lax.broadcasted_iota(jnp.int32, (L, L), 0)
    col = lax.broadcasted_iota(jnp.int32, (L, L), 1)
    causal = row >= col
    tril = jnp.where(causal, 1.0, 0.0).astype(BF16)
    a_hi, a_mid, a_lo = _split3(dta)
    acum = _dot(tril, a_hi) + _dot(tril, a_mid) + _dot(tril, a_lo)
    acum_t = acum.T
    dt_t = dt.T
    alast = acum[L - 1:L, :]
    wend = jnp.exp(alast - acum) * dt
    eac = jnp.exp(acum)
    eh = eh_ref[...]
    wend_x = _dot_parts(_split2(wend), eh)
    eac_x = _dot_parts(_split2(eac), eh)
    xw = xs * wend_x
    dec = jnp.exp(alast)
    lane = lax.broadcasted_iota(jnp.int32, (L, 2 * p), 1)

    for g in range(g_n):
        bg = bm[:, g * n:(g + 1) * n].astype(BF16)
        cg = cm[:, g * n:(g + 1) * n].astype(BF16)
        cbm = _dot_nt(cg, bg)
        s_g = st_ref[g * kp:(g + 1) * kp, :]
        yoff = _dot_nt(cg, s_g.astype(BF16))
        for j in range(k_h // 2):
            h0 = g * k_h + 2 * j
            ms = []
            for h in (h0, h0 + 1):
                seg = acum[:, h:h + 1] - acum_t[h:h + 1, :]
                dcy = jnp.exp(jnp.where(causal, seg, NEG_BIG))
                ms.append((cbm * dcy * dt_t[h:h + 1, :]).astype(BF16))
            lhs = jnp.concatenate(ms, axis=1)
            xp = xs[:, h0 * p:(h0 + 2) * p]
            rhs = jnp.concatenate([jnp.where(lane < p, xp, 0.0), jnp.where(lane >= p, xp, 0.0)],
                                  axis=0).astype(BF16)
            lo = h0 * p
            y_scr[:, lo:lo + 2 * p] = (_dot(lhs, rhs)
                                       + yoff[:, 2 * j * p:(2 * j + 2) * p] * eac_x[:, lo:lo + 2 * p])
        upd = _dot(xw[:, g * kp:(g + 1) * kp].T.astype(BF16), bg)
        for hh in range(k_h):
            h = g * k_h + hh
            r0 = h * p
            dsc = jnp.broadcast_to(dec[:, h:h + 1], (p, n))
            st_ref[r0:r0 + p, :] = dsc * s_g[hh * p:(hh + 1) * p, :] + upd[hh * p:(hh + 1) * p, :]

    y = (y_scr[...] + dexp_ref[...] * xs) * _silu(z)
    ng = ng_ref[...]
    for g in range(g_n):
        yg = y[:, g * kp:(g + 1) * kp]
        ms_ = jnp.mean(yg * yg, axis=-1, keepdims=True)
        y_ref[:, g * kp:(g + 1) * kp] = (yg * lax.rsqrt(ms_ + EPS) * ng[:, g * kp:(g + 1) * kp]).astype(y_ref.dtype)


def _ssd_prompt(zx, dtraw, cw, cb, dtb, alog, dexp, ng, eh, nb, t, hp, n, g_n, k_h, p):
    d = hp
    L = min(SSD_CHUNK, t)
    nc = t // L
    cc = cw.shape[1]
    kern = functools.partial(_ssd_prompt_kernel, d=d, g_n=g_n, n=n, k_h=k_h, p=p)
    const = lambda shape: pl.BlockSpec(shape, lambda b, c: (0,) * len(shape))
    return pl.pallas_call(
        kern,
        grid=(nb, nc),
        in_specs=[
            pl.BlockSpec((L, d + cc), lambda b, c: (b * nc + c, 0)),
            pl.BlockSpec((L, LANES), lambda b, c: (b * nc + c, 0)),
            const(cw.shape), const(cb.shape), const(dtb.shape), const(alog.shape),
            const(dexp.shape), const(ng.shape), const(eh.shape),
        ],
        out_specs=[
            pl.BlockSpec((L, d), lambda b, c: (b * nc + c, 0)),
            pl.BlockSpec((None, hp, n), lambda b, c: (b, 0, 0)),
        ],
        out_shape=[
            jax.ShapeDtypeStruct((nb * t, d), BF16),
            jax.ShapeDtypeStruct((nb, hp, n), F32),
        ],
        scratch_shapes=[pltpu.VMEM((L + SUBLANES, cc), F32), pltpu.VMEM((L, d), F32)],
        compiler_params=_cparams(("arbitrary", "arbitrary")),
        name="ssd_prompt",
    )(zx, dtraw, cw, cb, dtb, alog, dexp, ng, eh)


def _ssd_sample_pre_kernel(zx_ref, dt_ref, st_ref, cw_ref, cb_ref, dtb_ref, alog_ref, dexp_ref,
                           eh_ref, gh_ref, a_ref, ex_ref, zs_ref, xw_ref, c_ref, b_ref, dec_ref,
                           *, d, g_n, n):
    ts, bs, _ = zx_ref.shape
    kw = cw_ref.shape[0]
    gn = g_n * n
    rows = [st_ref[j] for j in range(kw - 1)]
    rows += [zx_ref[t][:, d:].astype(F32) for t in range(ts)]
    eh = eh_ref[...]
    gh = gh_ref[...]
    a = -jnp.exp(alog_ref[...])
    xs, bm, cm, dts, acs = [], [], [], [], []
    run = None
    for t in range(ts):
        conv = cb_ref[...] + cw_ref[0:1, :] * rows[t]
        for k in range(1, kw):
            conv = conv + cw_ref[k:k + 1, :] * rows[t + k]
        xbc = _silu(conv)
        xs.append(xbc[:, :d])
        bm.append(xbc[:, d:d + gn])
        cm.append(xbc[:, d + gn:])
        dtt = _softplus(dt_ref[t] + dtb_ref[...])
        dts.append(dtt)
        run = dtt * a if run is None else run + dtt * a
        acs.append(run)
    alast = acs[-1]
    dec_ref[...] = jnp.exp(alast)
    for t in range(ts):
        acc = dexp_ref[...] * xs[t]
        for j in range(t + 1):
            cbh = _dot_parts(_split2(cm[t] * bm[j]), gh)
            coef = cbh * dts[j]
            if j < t:
                coef = coef * jnp.exp(acs[t] - acs[j])
            acc = acc + _dot_parts(_split2(coef), eh) * xs[j]
        a_ref[t] = acc
        ex_ref[t] = _dot_parts(_split2(jnp.exp(acs[t])), eh)
        zs_ref[t] = _silu(zx_ref[t][:, :d].astype(F32))
        xw_ref[t] = xs[t] * _dot_parts(_split2(jnp.exp(alast - acs[t]) * dts[t]), eh)
        c_ref[t] = cm[t]
        b_ref[t] = bm[t]


def _ssd_sample_pre(zx3, dt3, tblk, ts, sconv_t, layer, cw, cb, dtb, alog, dexp, eh, gh, d, g_n, n):
    bs = zx3.shape[1]
    gn = g_n * n
    kern = functools.partial(_ssd_sample_pre_kernel, d=d, g_n=g_n, n=n)
    sb = min(SAMPLE_PRE_BLOCK, bs)
    kw = cw.shape[0]
    const = lambda a: pl.BlockSpec(a.shape, lambda i: (0,) * a.ndim)
    seq3 = lambda c: pl.BlockSpec((ts, sb, c), lambda i: (0, i, 0))
    src3 = lambda c: pl.BlockSpec((ts, sb, c), lambda i: (tblk, i, 0))
    big = jax.ShapeDtypeStruct((ts, bs, d), F32)
    small = jax.ShapeDtypeStruct((ts, bs, gn), F32)
    return pl.pallas_call(
        kern,
        grid=(bs // sb,),
        in_specs=[src3(zx3.shape[2]), src3(LANES),
                  pl.BlockSpec((None, kw - 1, sb, sconv_t.shape[3]), lambda i: (layer, 0, i, 0)),
                  const(cw), const(cb), const(dtb), const(alog), const(dexp), const(eh), const(gh)],
        out_specs=[seq3(d), seq3(d), seq3(d), seq3(d), seq3(gn), seq3(gn),
                   pl.BlockSpec((sb, LANES), lambda i: (i, 0))],
        out_shape=[big, big, big, big, small, small, jax.ShapeDtypeStruct((bs, LANES), F32)],
        compiler_params=_cparams(("arbitrary",)),
        name="ssd_sample_pre",
    )(zx3, dt3, sconv_t, cw, cb, dtb, alog, dexp, eh, gh)


def _ssd_sample_state_kernel(dec_ref, s0_ref, a_ref, ex_ref, zs_ref, xwt_ref, c_ref, b_ref, ng_ref,
                             prev_ref, y_ref, s1_ref, *, g_n, n, k_h, p, sb):
    del prev_ref
    i = pl.program_id(0)
    kp = k_h * p
    n_heads = g_n * k_h
    rows8 = sb * SUBLANES
    rid = lax.broadcasted_iota(jnp.int32, (rows8, 1), 0)
    ng = ng_ref[...]

    def body(s, carry):
        r0 = pl.multiple_of(s * SUBLANES, SUBLANES)
        in_s = (rid >= r0) & (rid < r0 + SUBLANES)
        for g in range(g_n):
            cg = c_ref[pl.ds(r0, SUBLANES), g * n:(g + 1) * n].astype(BF16)
            s_g = s0_ref[s, g * kp:(g + 1) * kp, :]
            yoff = _dot_nt(cg, s_g.astype(BF16))
            b_all = b_ref[:, g * n:(g + 1) * n]
            b_msk = jnp.where(in_s, b_all, 0.0).astype(BF16)
            upd = _dot(xwt_ref[g * kp:(g + 1) * kp, :].astype(BF16), b_msk)
            for hh in range(k_h):
                h = g * k_h + hh
                dsc = dec_ref[(i * sb + s) * n_heads + h]
                s1_ref[s, h * p:(h + 1) * p, :] = (dsc * s_g[hh * p:(hh + 1) * p, :]
                                                   + upd[hh * p:(hh + 1) * p, :])
            cols = slice(g * kp, (g + 1) * kp)
            yg = ((a_ref[pl.ds(r0, SUBLANES), cols] + ex_ref[pl.ds(r0, SUBLANES), cols] * yoff)
                  * zs_ref[pl.ds(r0, SUBLANES), cols])
            ms_ = jnp.mean(yg * yg, axis=-1, keepdims=True)
            y_ref[pl.ds(r0, SUBLANES), cols] = (yg * lax.rsqrt(ms_ + EPS) * ng[:, cols]).astype(y_ref.dtype)
        return carry

    lax.fori_loop(0, sb, body, 0)


def _ssd_sample_state(dec, s0_all, layer, s1_prev, a8, ex8, zs8, xwt, c8, b8, ng, g_n, n, k_h, p):
    depth, bs, hp, _ = s0_all.shape
    d = a8.shape[1]
    gn = g_n * n
    sb = min(SAMPLE_SEQ_BLOCK, bs)
    r8 = sb * SUBLANES
    kern = functools.partial(_ssd_sample_state_kernel, g_n=g_n, n=n, k_h=k_h, p=p, sb=sb)
    return pl.pallas_call(
        kern,
        grid=(bs // sb,),
        in_specs=[
            pl.BlockSpec(memory_space=pltpu.SMEM),
            pl.BlockSpec((None, sb, hp, n), lambda i: (layer, i, 0, 0)),
            pl.BlockSpec((r8, d), lambda i: (i, 0)),
            pl.BlockSpec((r8, d), lambda i: (i, 0)),
            pl.BlockSpec((r8, d), lambda i: (i, 0)),
            pl.BlockSpec((None, d, r8), lambda i: (i, 0, 0)),
            pl.BlockSpec((r8, gn), lambda i: (i, 0)),
            pl.BlockSpec((r8, gn), lambda i: (i, 0)),
            pl.BlockSpec((1, d), lambda i: (0, 0)),
            pl.BlockSpec(memory_space=pl.ANY),
        ],
        out_specs=[
            pl.BlockSpec((r8, d), lambda i: (i, 0)),
            pl.BlockSpec((None, sb, hp, n), lambda i: (layer, i, 0, 0)),
        ],
        out_shape=[
            jax.ShapeDtypeStruct((bs * SUBLANES, d), F32),
            jax.ShapeDtypeStruct((depth, bs, hp, n), F32),
        ],
        input_output_aliases={9: 1},
        compiler_params=_cparams(("arbitrary",)),
        name="ssd_sample_state",
    )(dec, s0_all, a8, ex8, zs8, xwt, c8, b8, ng, s1_prev)


def _ln_swish(v, g, b):
    mu = jnp.mean(v, axis=-1, keepdims=True)
    cen = v - mu
    var = jnp.mean(cen * cen, axis=-1, keepdims=True)
    return _silu(cen * lax.rsqrt(var + EPS) * g + b)


def _conf_prompt_kernel(u_ref, w_ref, b_ref, lg_ref, lb_ref, v_ref, upad, ush, *, halo):
    c = pl.program_id(1)
    L, d = u_ref.shape
    kw = w_ref.shape[0]
    span = L + halo - SUBLANES

    @pl.when(c == 0)
    def _():
        upad[0:halo, :] = jnp.zeros((halo, d), F32)

    u = u_ref[...].astype(F32)
    upad[halo:halo + L, :] = u
    for r in range(1, SUBLANES):
        ush[r - 1, 0:span, :] = upad[r:r + span, :]
    acc = jnp.broadcast_to(b_ref[...], (L, d))
    for k in range(kw):
        q, r = divmod(halo - (kw - 1) + k, SUBLANES)
        src = upad[q * SUBLANES:q * SUBLANES + L, :] if r == 0 else ush[r - 1, q * SUBLANES:q * SUBLANES + L, :]
        acc = acc + w_ref[k:k + 1, :] * src
    upad[0:halo, :] = u[L - halo:, :]
    v_ref[...] = _ln_swish(acc, lg_ref[...], lb_ref[...]).astype(v_ref.dtype)


def _conf_prompt(u, w, b, lg, lb, nb, t):
    d = u.shape[1]
    kw = w.shape[0]
    L = min(CONF_CHUNK, t)
    nc = t // L
    halo = -(-(kw - 1) // SUBLANES) * SUBLANES
    const = lambda a: pl.BlockSpec(a.shape, lambda bb, c: (0,) * a.ndim)
    return pl.pallas_call(
        functools.partial(_conf_prompt_kernel, halo=halo),
        grid=(nb, nc),
        in_specs=[pl.BlockSpec((L, d), lambda bb, c: (bb * nc + c, 0)), const(w), const(b), const(lg), const(lb)],
        out_specs=pl.BlockSpec((L, d), lambda bb, c: (bb * nc + c, 0)),
        out_shape=jax.ShapeDtypeStruct((nb * t, d), BF16),
        scratch_shapes=[pltpu.VMEM((L + halo, d), F32), pltpu.VMEM((SUBLANES - 1, L + halo, d), F32)],
        compiler_params=_cparams(("arbitrary", "arbitrary")),
        name="conf_prompt",
    )(u, w, b, lg, lb)


def _conf_sample_kernel(u_ref, st_ref, w_ref, b_ref, lg_ref, lb_ref, v_ref):
    ts, sb, d = u_ref.shape
    kw = w_ref.shape[0]
    rows = [st_ref[j] for j in range(kw - 1)]
    rows += [u_ref[t].astype(F32) for t in range(ts)]
    for t in range(ts):
        acc = b_ref[...] + w_ref[0:1, :] * rows[t]
        for k in range(1, kw):
            acc = acc + w_ref[k:k + 1, :] * rows[t + k]
        v_ref[t] = _ln_swish(acc, lg_ref[...], lb_ref[...]).astype(v_ref.dtype)


def _conf_sample(u3, tblk, ts, cconv_t, layer, w, b, lg, lb):
    _, bs, d = u3.shape
    kw = w.shape[0]
    sb = min(CONF_SAMPLE_BLOCK, bs)
    const = lambda a: pl.BlockSpec(a.shape, lambda i: (0,) * a.ndim)
    return pl.pallas_call(
        _conf_sample_kernel,
        grid=(bs // sb,),
        in_specs=[
            pl.BlockSpec((ts, sb, d), lambda i: (tblk, i, 0)),
            pl.BlockSpec((None, kw - 1, sb, d), lambda i: (layer, 0, i, 0)),
            const(w), const(b), const(lg), const(lb),
        ],
        out_specs=pl.BlockSpec((ts, sb, d), lambda i: (0, i, 0)),
        out_shape=jax.ShapeDtypeStruct((ts, bs, d), BF16),
        compiler_params=_cparams(("arbitrary",)),
        name="conf_sample",
    )(u3, cconv_t, w, b, lg, lb)


def _merge_kernel(yp_ref, ys_ref, vp_ref, vs_ref, ws_ref, wc_ref, gs_ref, gc_ref, o_ref, *, npt):
    is_p = pl.program_id(0) < npt
    y = jnp.where(is_p, yp_ref[...], ys_ref[...])
    v = jnp.where(is_p, vp_ref[...], vs_ref[...])
    a = _dot(y, ws_ref[...])
    b = _dot(v, wc_ref[...])
    o_ref[...] = (gs_ref[...].astype(F32) * a + gc_ref[...].astype(F32) * b).astype(o_ref.dtype)


def _merge(y_p, y_s, v_p, v_s, ws, wc, gates, tm):
    m, d = gates.shape[0], y_p.shape[1]
    npt = y_p.shape[0] // tm
    tn = _pick_tn(d)
    nj = d // tn
    p_spec = pl.BlockSpec((tm, d), lambda i, j: (jnp.minimum(i, npt - 1), 0))
    s_spec = pl.BlockSpec((tm, d), lambda i, j: (0, 0))
    return pl.pallas_call(
        functools.partial(_merge_kernel, npt=npt),
        grid=(m // tm, nj),
        in_specs=[
            p_spec, s_spec, p_spec, s_spec,
            pl.BlockSpec((d, tn), lambda i, j: (0, j)),
            pl.BlockSpec((d, tn), lambda i, j: (0, j)),
            pl.BlockSpec((tm, tn), lambda i, j: (i, j)),
            pl.BlockSpec((tm, tn), lambda i, j: (i, nj + j)),
        ],
        out_specs=pl.BlockSpec((tm, tn), lambda i, j: (i, j)),
        out_shape=jax.ShapeDtypeStruct((m, d), BF16),
        compiler_params=_cparams(("arbitrary", "arbitrary")),
        name="merge",
    )(y_p, y_s, v_p, v_s, ws, wc, gates, gates)


def _outproj_kernel(mx_ref, w_ref, x_ref, gp_ref, gs_ref, o_ref, *, npt, ts):
    gate = _tile_mod(pl.program_id(0), npt, ts, gp_ref, gs_ref)
    o_ref[...] = x_ref[...] + gate * _dot(mx_ref[...], w_ref[...])


def _outproj(mixed, wo, x, modp, mods, sec, dims):
    m, d = x.shape
    tm, npt, tps, nb, bs, ts = dims
    tn = _pick_tn(d)
    nj = d // tn
    colj = lambda s, i, j: s * nj + j
    return pl.pallas_call(
        functools.partial(_outproj_kernel, npt=npt, ts=ts),
        grid=(m // tm, nj),
        in_specs=[
            pl.BlockSpec((tm, d), lambda i, j: (i, 0)),
            pl.BlockSpec((d, tn), lambda i, j: (0, j)),
            pl.BlockSpec((tm, tn), lambda i, j: (i, j)),
            *_mod_specs(npt, tps, nb, bs, tn, sec, colj),
        ],
        out_specs=pl.BlockSpec((tm, tn), lambda i, j: (i, j)),
        out_shape=jax.ShapeDtypeStruct((m, d), F32),
        compiler_params=_cparams(("arbitrary", "arbitrary")),
        name="outproj",
    )(mixed, wo, x, modp, mods)


def _pack_bf16_pairs(v):
    half = v.shape[1] // 2
    bits = pltpu.bitcast(v.astype(BF16).astype(F32), jnp.uint32)
    return bits[:, :half] | (bits[:, half:] >> 16)


def _unpack_bf16_pairs(w):
    left = pltpu.bitcast(w & jnp.uint32(0xFFFF0000), F32)
    right = pltpu.bitcast(w << 16, F32)
    return jnp.concatenate([left, right], axis=1)


INFO_RANK1, INFO_RANK2, INFO_E1, INFO_E2, INFO_W1, INFO_W2 = range(6)


def _router_kernel(x_ref, g_ref, scp_ref, scs_ref, shp_ref, shs_ref, whi_ref, wlo_ref, rb_ref,
                   hp_ref, info_ref, cnt_ref, *, npt, ts, n_e, n_g):
    i = pl.program_id(0)
    epg = n_e // n_g

    @pl.when(i == 0)
    def _():
        cnt_ref[...] = jnp.zeros(cnt_ref.shape, F32)

    x = x_ref[...]
    xn = x * lax.rsqrt(jnp.mean(x * x, axis=-1, keepdims=True) + EPS) * g_ref[...]
    sc = _tile_mod(i, npt, ts, scp_ref, scs_ref)
    sh = _tile_mod(i, npt, ts, shp_ref, shs_ref)
    h = xn * (1.0 + sc) + sh
    hi, lo = _split2(h)
    hp_ref[...] = _pack_bf16_pairs(h)
    logit = _dot(hi, whi_ref[...]) + _dot(lo, whi_ref[...]) + _dot(hi, wlo_ref[...]) + rb_ref[...]
    tm = logit.shape[0]
    lane = lax.broadcasted_iota(jnp.int32, (tm, LANES), 1).astype(F32)
    first = lambda hit: jnp.min(jnp.where(hit, lane, float(LANES)), axis=-1, keepdims=True)
    gl = jnp.where(lane >= n_e, jnp.where(lane < n_e + n_g, logit, NEG_BIG), NEG_BIG)
    gmax = jnp.max(gl, axis=-1, keepdims=True)
    g_top = 1.0 / jnp.sum(jnp.exp(gl - gmax), axis=-1, keepdims=True)
    g_idx = first(gl == gmax) - n_e
    el = jnp.where(lane >= g_idx * epg, jnp.where(lane < (g_idx + 1.0) * epg, logit, NEG_BIG), NEG_BIG)
    m1 = jnp.max(el, axis=-1, keepdims=True)
    i1 = first(el == m1)
    el2 = jnp.where(lane == i1, NEG_BIG, el)
    m2 = jnp.max(el2, axis=-1, keepdims=True)
    i2 = first(el2 == m2)
    r = jnp.exp(m2 - m1)
    w1 = 1.0 / (1.0 + r)
    w2 = r * w1
    oh1 = jnp.where(lane == i1, 1.0, 0.0)
    oh2 = jnp.where(lane == i2, 1.0, 0.0)
    both = oh1 + oh2
    row = lax.broadcasted_iota(jnp.int32, (tm, tm), 0)
    col = lax.broadcasted_iota(jnp.int32, (tm, tm), 1)
    before = _dot(jnp.where(row > col, 1.0, 0.0).astype(BF16), both.astype(BF16)) + cnt_ref[0:1, :]
    rank1 = jnp.sum(before * oh1, axis=-1, keepdims=True)
    rank2 = jnp.sum(before * oh2, axis=-1, keepdims=True)
    cnt_ref[...] = cnt_ref[...] + jnp.sum(both, axis=0, keepdims=True)
    rec = [rank1, rank2, i1, i2, g_top * w1, g_top * w2]
    info = jnp.zeros((tm, LANES), F32)
    for k, val in enumerate(rec):
        info = jnp.where(lane == float(k), val, info)
    info_ref[...] = info


def _router(x, gain, modp, mods, sec_sc, sec_sh, whi, wlo, rb, n_e, n_g, dims):
    m, d = x.shape
    tm, npt, tps, nb, bs, ts = dims
    col0 = lambda sec, i: sec
    return pl.pallas_call(
        functools.partial(_router_kernel, npt=npt, ts=ts, n_e=n_e, n_g=n_g),
        grid=(m // tm,),
        in_specs=[
            pl.BlockSpec((tm, d), lambda i: (i, 0)),
            pl.BlockSpec((1, d), lambda i: (0, 0)),
            *_mod_specs(npt, tps, nb, bs, d, sec_sc, col0),
            *_mod_specs(npt, tps, nb, bs, d, sec_sh, col0),
            pl.BlockSpec((d, LANES), lambda i: (0, 0)),
            pl.BlockSpec((d, LANES), lambda i: (0, 0)),
            pl.BlockSpec((1, LANES), lambda i: (0, 0)),
        ],
        out_specs=[pl.BlockSpec((tm, d // 2), lambda i: (i, 0)), pl.BlockSpec((tm, LANES), lambda i: (i, 0)),
                   pl.BlockSpec((SUBLANES, LANES), lambda i: (0, 0))],
        out_shape=[jax.ShapeDtypeStruct((m, d // 2), jnp.uint32), jax.ShapeDtypeStruct((m, LANES), F32),
                   jax.ShapeDtypeStruct((SUBLANES, LANES), F32)],
        compiler_params=_cparams(("arbitrary",)),
        name="router",
    )(x, gain, modp, mods, modp, mods, whi, wlo, rb)


def _moe_plan(info, cnt, n_e, tg, nt_max):
    counts = cnt[0, :n_e].astype(jnp.int32)
    tiles = (counts + tg - 1) // tg
    tile_end = jnp.cumsum(tiles)
    offs = (tile_end - tiles) * tg
    n_used = tile_end[n_e - 1]
    e1 = info[:, INFO_E1].astype(jnp.int32)
    e2 = info[:, INFO_E2].astype(jnp.int32)
    pos1 = offs[e1] + info[:, INFO_RANK1].astype(jnp.int32)
    pos2 = offs[e2] + info[:, INFO_RANK2].astype(jnp.int32)
    tile = jnp.minimum(jnp.arange(nt_max, dtype=jnp.int32), n_used - 1)
    tile_expert = jnp.searchsorted(tile_end, tile, side="right").astype(jnp.int32)
    return pos1, pos2, tile_expert, n_used.reshape(1)


def _dispatch_kernel(pos1_ref, pos2_ref, hp_ref, xs0_ref, xs_ref, sem):
    del xs0_ref
    tm = hp_ref.shape[0]
    base = pl.program_id(0) * tm

    def issue(r, carry):
        src = hp_ref.at[pl.ds(r, 1)]
        pltpu.make_async_copy(src, xs_ref.at[pl.ds(pos1_ref[base + r], 1)], sem).start(priority=0)
        pltpu.make_async_copy(src, xs_ref.at[pl.ds(pos2_ref[base + r], 1)], sem).start(priority=1)
        return carry
    lax.fori_loop(0, tm, issue, 0, unroll=ROW_DMA_UNROLL)

    def drain(r, carry):
        for _ in range(2):
            pltpu.make_async_copy(hp_ref.at[pl.ds(0, 1)], xs_ref.at[pl.ds(0, 1)], sem).wait()
        return carry
    lax.fori_loop(0, tm, drain, 0, unroll=ROW_DMA_UNROLL)


def _dispatch(pos1, pos2, hp, n_slots, tm):
    m, half = hp.shape
    smem = pl.BlockSpec(memory_space=pltpu.SMEM)
    hbm = pl.BlockSpec(memory_space=pl.ANY)
    return pl.pallas_call(
        _dispatch_kernel,
        grid=(m // tm,),
        in_specs=[smem, smem, pl.BlockSpec((tm, half), lambda i: (i, 0)), hbm],
        out_specs=hbm,
        out_shape=jax.ShapeDtypeStruct((n_slots, half), jnp.uint32),
        scratch_shapes=[pltpu.SemaphoreType.DMA(())],
        input_output_aliases={3: 0},
        compiler_params=_cparams(("arbitrary",)),
        name="moe_dispatch",
    )(pos1, pos2, hp, jnp.zeros((n_slots, half), jnp.uint32))


def _experts_kernel(te_ref, nu_ref, xs_ref, wg_ref, wu_ref, wd_ref, ys_ref):
    del te_ref
    i = pl.program_id(0)

    @pl.when(i < nu_ref[0])
    def _():
        x = _unpack_bf16_pairs(xs_ref[...]).astype(BF16)
        hid = _silu(_dot(x, wg_ref[...].astype(BF16))) * _dot(x, wu_ref[...].astype(BF16))
        ys_ref[...] = _pack_bf16_pairs(_dot(hid.astype(BF16), wd_ref[...].astype(BF16)))

    @pl.when(i >= nu_ref[0])
    def _():
        ys_ref[...] = jnp.zeros(ys_ref.shape, jnp.uint32)


def _experts(tile_expert, n_used, xs, wg, wu, wd, layer, tg):
    n_slots, half = xs.shape
    _, _, d, f = wg.shape
    grid_spec = pltpu.PrefetchScalarGridSpec(
        num_scalar_prefetch=2,
        grid=(n_slots // tg,),
        in_specs=[
            pl.BlockSpec((tg, half), lambda i, te, nu: (i, 0)),
            pl.BlockSpec((None, None, d, f), lambda i, te, nu: (layer, te[i], 0, 0)),
            pl.BlockSpec((None, None, d, f), lambda i, te, nu: (layer, te[i], 0, 0)),
            pl.BlockSpec((None, None, f, d), lambda i, te, nu: (layer, te[i], 0, 0)),
        ],
        out_specs=pl.BlockSpec((tg, half), lambda i, te, nu: (i, 0)),
    )
    return pl.pallas_call(
        _experts_kernel,
        grid_spec=grid_spec,
        out_shape=jax.ShapeDtypeStruct((n_slots, half), jnp.uint32),
        compiler_params=_cparams(("arbitrary",)),
        name="moe_experts",
    )(tile_expert, n_used, xs, wg, wu, wd)


def _combine_kernel(pos1_ref, pos2_ref, info_ref, x_ref, gp_ref, gs_ref, ys_ref, o_ref, buf_a, buf_b, sem,
                    *, npt, ts):
    i = pl.program_id(0)
    tm = x_ref.shape[0]
    slot = lax.rem(i, 2)

    def issue(tile, s):
        def body(r, carry):
            t = tile * tm + r
            pltpu.make_async_copy(ys_ref.at[pl.ds(pos1_ref[t], 1)], buf_a.at[s, pl.ds(r, 1)], sem.at[s]).start(priority=0)
            pltpu.make_async_copy(ys_ref.at[pl.ds(pos2_ref[t], 1)], buf_b.at[s, pl.ds(r, 1)], sem.at[s]).start(priority=1)
            return carry
        lax.fori_loop(0, tm, body, 0, unroll=ROW_DMA_UNROLL)

    @pl.when(i == 0)
    def _():
        issue(0, 0)

    @pl.when(i + 1 < pl.num_programs(0))
    def _():
        issue(i + 1, 1 - slot)

    def wait_body(r, carry):
        pltpu.make_async_copy(ys_ref.at[pl.ds(0, 1)], buf_a.at[slot, pl.ds(0, 1)], sem.at[slot]).wait()
        pltpu.make_async_copy(ys_ref.at[pl.ds(0, 1)], buf_b.at[slot, pl.ds(0, 1)], sem.at[slot]).wait()
        return carry
    lax.fori_loop(0, tm, wait_body, 0, unroll=ROW_DMA_UNROLL)

    info = info_ref[...]
    w1 = info[:, INFO_W1:INFO_W1 + 1]
    w2 = info[:, INFO_W2:INFO_W2 + 1]
    moe = w1 * _unpack_bf16_pairs(buf_a[slot]) + w2 * _unpack_bf16_pairs(buf_b[slot])
    o_ref[...] = x_ref[...] + _tile_mod(i, npt, ts, gp_ref, gs_ref) * moe


def _combine(pos1, pos2, info, x, modp, mods, sec, ys, dims):
    m, d = x.shape
    tm, npt, tps, nb, bs, ts = dims
    col0 = lambda s, i: s
    smem = pl.BlockSpec(memory_space=pltpu.SMEM)
    return pl.pallas_call(
        functools.partial(_combine_kernel, npt=npt, ts=ts),
        grid=(m // tm,),
        in_specs=[
            smem, smem,
            pl.BlockSpec((tm, LANES), lambda i: (i, 0)),
            pl.BlockSpec((tm, d), lambda i: (i, 0)),
            *_mod_specs(npt, tps, nb, bs, d, sec, col0),
            pl.BlockSpec(memory_space=pl.ANY),
        ],
        out_specs=pl.BlockSpec((tm, d), lambda i: (i, 0)),
        out_shape=jax.ShapeDtypeStruct((m, d), F32),
        scratch_shapes=[pltpu.VMEM((2, tm, d // 2), jnp.uint32), pltpu.VMEM((2, tm, d // 2), jnp.uint32),
                        pltpu.SemaphoreType.DMA((2,))],
        compiler_params=_cparams(("arbitrary",)),
        name="moe_combine",
    )(pos1, pos2, info, x, modp, mods, ys)


def _moe(hp, info, cnt, wg, wu, wd, layer, x, modp, mods, sec, dims):
    m = x.shape[0]
    n_e = wg.shape[1]
    tg = MOE_TG
    nt_max = -(-2 * m // tg) + n_e
    pos1, pos2, tile_expert, n_used = _moe_plan(info, cnt, n_e, tg, nt_max)
    xs = _dispatch(pos1, pos2, hp, nt_max * tg, dims[0])
    ys = _experts(tile_expert, n_used, xs, wg, wu, wd, layer, tg)
    return _combine(pos1, pos2, info, x, modp, mods, sec, ys, dims)


def _final_norm_kernel(x_ref, g_ref, o_ref):
    x = x_ref[...]
    o_ref[...] = x * lax.rsqrt(jnp.mean(x * x, axis=-1, keepdims=True) + EPS) * g_ref[...]


def _final_norm(x, g, tm, tile0, ntiles):
    d = x.shape[1]
    return pl.pallas_call(
        _final_norm_kernel,
        grid=(ntiles,),
        in_specs=[pl.BlockSpec((tm, d), lambda i: (tile0 + i, 0)), pl.BlockSpec((1, d), lambda i: (0, 0))],
        out_specs=pl.BlockSpec((tm, d), lambda i: (i, 0)),
        out_shape=jax.ShapeDtypeStruct((ntiles * tm, d), F32),
        compiler_params=_cparams(("arbitrary",)),
        name="final_norm",
    )(x, g)


def _to_seq8(a):
    ts, bs, c = a.shape
    a = jnp.transpose(a, (1, 0, 2))
    a = jnp.pad(a, ((0, 0), (0, SUBLANES - ts), (0, 0)))
    return a.reshape(bs * SUBLANES, c)


def kernel(x_prompt, x_sample, c_prompt, c_sample, state_ssm, state_ssd_conv, state_conf_conv, norm_mix_g, norm_ffn_g, w_ada, b_ada, w_in, ssd_conv_w, ssd_conv_b, ssd_dt_bias, ssd_a_log, ssd_d, ssd_norm_g, w_ssd_out, conf_conv_w, conf_conv_b, conf_ln_g, conf_ln_b, w_conf_out, w_o, w_group_router, b_group_router, w_expert_router, b_expert_router, w_exp_gate, w_exp_up, w_exp_down, final_norm_g):
    nb, t, d = x_prompt.shape
    bs, ts, _ = x_sample.shape
    depth = w_in.shape[0]
    n_h = ssd_dt_bias.shape[1]
    p = d // n_h
    n = state_ssm.shape[-1]
    cc = ssd_conv_w.shape[-1]
    g_n = (cc - d) // (2 * n)
    k_h = n_h // g_n
    gn = g_n * n
    kc = conf_conv_w.shape[1]
    n_e = w_exp_gate.shape[1]
    n_g = w_group_router.shape[-1]
    tm = bs * ts
    assert t % tm == 0 and tm % SUBLANES == 0 and bs % SUBLANES == 0 and ts <= SUBLANES
    assert 2 * p == LANES and k_h % 2 == 0 and n_h <= LANES and n_e + n_g <= LANES
    npt = nb * t // tm
    dims = (tm, npt, t // tm, nb, bs, ts)
    mp = nb * t

    x = jnp.concatenate([x_prompt.reshape(mp, d), jnp.transpose(x_sample, (1, 0, 2)).reshape(tm, d)], axis=0)

    rc = -(-(nb + bs) // SUBLANES) * SUBLANES
    c_all = jnp.concatenate([c_prompt, c_sample, jnp.zeros((rc - nb - bs, d), F32)], axis=0)
    mod = _adaln(c_all, w_ada, b_ada)

    head_of = jnp.arange(d) // p
    eh = (jnp.arange(LANES)[:, None] == head_of[None, :]).astype(BF16)
    gh = ((jnp.arange(gn)[:, None] // n) == (jnp.arange(LANES)[None, :] // k_h)).astype(BF16)
    gh = gh * (jnp.arange(LANES)[None, :] < n_h).astype(BF16)
    pad_h = lambda v: jnp.pad(v, (0, LANES - n_h)).reshape(1, LANES)

    m = mp + tm
    tblk = mp // bs // ts
    assert tblk * bs * ts == mp
    sconv_t = jnp.transpose(state_ssd_conv, (0, 2, 1, 3))
    cconv_t = jnp.transpose(state_conf_conv, (0, 2, 1, 3))
    ssm0_all = state_ssm.reshape(depth, bs, n_h * p, n)
    ssm_s = jnp.zeros(ssm0_all.shape, F32)
    ssm_p, sconv_p, cconv_p, sconv_s, cconv_s = [], [], [], [], []
    for l in range(depth):
        modp = mod[l, :nb].reshape(nb, 1, 6 * d)
        mods = mod[l, nb:nb + bs]
        wl = w_in[l]
        o_dt, o_glu, o_gs = d + cc, d + cc + n_h, d + cc + n_h + 2 * d
        w_dt = jnp.pad(wl[:, o_dt:o_glu], ((0, 0), (0, LANES - n_h))).astype(BF16)
        w_ga = wl[:, o_glu:o_glu + d].astype(BF16)
        w_gg = wl[:, o_glu + d:o_gs].astype(BF16)
        w_gates = wl[:, o_gs:].astype(BF16)
        h1 = _modnorm(x, norm_mix_g[l].reshape(1, d), modp, mods, 1, 0, dims)
        zx = _proj(h1, [w_in], BF16, "plain", layer=l, ncols=o_dt)
        dtraw = _proj(h1, [w_dt], F32, "plain")
        u = _proj(h1, [w_ga, w_gg], BF16, "glu")
        gates = _proj(h1, [w_gates], BF16, "sigmoid")

        cw = ssd_conv_w[l]
        cb = ssd_conv_b[l].reshape(1, cc)
        dtb = pad_h(ssd_dt_bias[l])
        alog = pad_h(ssd_a_log[l])
        dexp = jnp.repeat(ssd_d[l], p).reshape(1, d)
        ng = ssd_norm_g[l].reshape(1, d)

        y_p, s_p = _ssd_prompt(zx, dtraw, cw, cb, dtb, alog, dexp, ng, eh, nb, t, n_h * p, n, g_n, k_h, p)
        zx_s = zx[mp:].reshape(ts, bs, d + cc)
        sconv0 = state_ssd_conv[l]
        a_t, ex_t, zs_t, xw_t, c_t, b_t, dec = _ssd_sample_pre(
            zx.reshape(m // bs, bs, d + cc), dtraw.reshape(m // bs, bs, LANES), tblk, ts, sconv_t, l,
            cw, cb, dtb, alog, dexp, eh, gh, d, g_n, n)
        sb = min(SAMPLE_SEQ_BLOCK, bs)
        xwt = _to_seq8(xw_t).reshape(bs // sb, sb * SUBLANES, d).transpose(0, 2, 1)
        dec = dec[:, :n_h].reshape(bs * n_h)
        y_s8, ssm_s = _ssd_sample_state(dec, ssm0_all, l, ssm_s, _to_seq8(a_t), _to_seq8(ex_t),
                                        _to_seq8(zs_t), xwt, _to_seq8(c_t), _to_seq8(b_t), ng, g_n, n, k_h, p)
        y_s = jnp.transpose(y_s8.reshape(bs, SUBLANES, d)[:, :ts], (1, 0, 2)).reshape(tm, d).astype(BF16)

        ccw = conf_conv_w[l]
        ccb = conf_conv_b[l].reshape(1, d)
        lg = conf_ln_g[l].reshape(1, d)
        lb = conf_ln_b[l].reshape(1, d)
        v_p = _conf_prompt(u, ccw, ccb, lg, lb, nb, t)
        u_s = u[mp:].reshape(ts, bs, d)
        cconv0 = state_conf_conv[l]
        v_s = _conf_sample(u.reshape(m // bs, bs, d), tblk, ts, cconv_t, l, ccw, ccb, lg, lb).reshape(tm, d)

        mixed = _merge(y_p, y_s, v_p, v_s, w_ssd_out[l].astype(BF16), w_conf_out[l].astype(BF16), gates, tm)
        x = _outproj(mixed, w_o[l].astype(BF16), x, modp, mods, 2, dims)

        w_r = jnp.concatenate([w_expert_router[l], w_group_router[l],
                               jnp.zeros((d, LANES - n_e - n_g), F32)], axis=1)
        w_r_hi = w_r.astype(BF16)
        w_r_lo = (w_r - w_r_hi.astype(F32)).astype(BF16)
        rb = jnp.concatenate([b_expert_router[l], b_group_router[l], jnp.zeros((LANES - n_e - n_g,), F32)]).reshape(1, LANES)
        hp, info, cnt = _router(x, norm_ffn_g[l].reshape(1, d), modp, mods, 4, 3, w_r_hi, w_r_lo, rb, n_e, n_g, dims)
        x = _moe(hp, info, cnt, w_exp_gate, w_exp_up, w_exp_down, l, x, modp, mods, 5, dims)

        ssm_p.append(s_p.reshape(nb, n_h, p, n))
        xraw_p = zx[:mp, d:].reshape(nb, t, cc)
        sconv_p.append(xraw_p[:, t - (sconv0.shape[1]):].astype(F32))
        cconv_p.append(u[:mp].reshape(nb, t, d)[:, t - (kc - 1):].astype(F32))
        xraw_s = jnp.transpose(zx_s[:, :, d:], (1, 0, 2)).astype(F32)
        sconv_s.append(jnp.concatenate([sconv0, xraw_s], axis=1)[:, ts:])
        cconv_s.append(jnp.concatenate([cconv0, jnp.transpose(u_s, (1, 0, 2)).astype(F32)], axis=1)[:, ts:])

    fg = final_norm_g.reshape(1, d)
    y_prompt = _final_norm(x, fg, tm, 0, npt).reshape(nb, t, d)
    y_sample = jnp.transpose(_final_norm(x, fg, tm, npt, 1).reshape(ts, bs, d), (1, 0, 2))
    return (y_prompt, y_sample, jnp.stack(ssm_p), jnp.stack(sconv_p), jnp.stack(cconv_p),
            ssm_s.reshape(depth, bs, n_h, p, n), jnp.stack(sconv_s), jnp.stack(cconv_s))
```

```python
import functools

import jax
import jax.numpy as jnp
from jax import lax
from jax.experimental import pallas as pl
from jax.experimental.pallas import tpu as pltpu

F32 = jnp.float32
BF16 = jnp.bfloat16
EPS = 1e-6
LANES = 128
SUBLANES = 8
VMEM_LIMIT = 52 * 1024 * 1024
NEG_BIG = -1e30
SSD_CHUNK = 128
CONF_CHUNK = 256
SAMPLE_SEQ_BLOCK = 8
SAMPLE_PRE_BLOCK = 32
CONF_SAMPLE_BLOCK = 16
MOE_TG = 256
ROW_DMA_UNROLL = 8
ADA_TN = 1024
PROJ_TN = 1024
PROJ_TM = 1088


def _pick_tn(ncols):
    assert ncols % LANES == 0
    return max(tn for tn in range(LANES, min(PROJ_TN, ncols) + 1, LANES) if ncols % tn == 0)


def _pick_rows(m):
    return max(tm for tm in range(2 * SUBLANES, min(PROJ_TM, m) + 1, 2 * SUBLANES) if m % tm == 0)


def _cparams(sem):
    return pltpu.CompilerParams(dimension_semantics=sem, vmem_limit_bytes=VMEM_LIMIT)


def _silu(x):
    return x * jax.nn.sigmoid(x)


def _softplus(x):
    return jnp.maximum(x, 0.0) + jnp.log1p(jnp.exp(-jnp.abs(x)))


def _split2(x):
    hi = x.astype(BF16)
    lo = (x - hi.astype(F32)).astype(BF16)
    return hi, lo


def _split3(x):
    hi = x.astype(BF16)
    r = x - hi.astype(F32)
    mid = r.astype(BF16)
    lo = (r - mid.astype(F32)).astype(BF16)
    return hi, mid, lo


def _dot(a, b):
    return jnp.dot(a, b, preferred_element_type=F32)


def _dot_nt(a, b):
    return lax.dot_general(a, b, (((1,), (1,)), ((), ())), preferred_element_type=F32)


def _dot_parts(parts, w):
    acc = _dot(parts[0], w)
    for p in parts[1:]:
        acc = acc + _dot(p, w)
    return acc


def _adaln_kernel(c_ref, w_ref, b_ref, o_ref):
    s = _silu(c_ref[...]).astype(BF16)
    o_ref[...] = _dot(s, w_ref[...].astype(BF16)) + b_ref[...]


def _adaln(c_all, w_ada, b_ada):
    depth, d, n6 = w_ada.shape
    rc = c_all.shape[0]
    tn = min(ADA_TN, n6)
    return pl.pallas_call(
        _adaln_kernel,
        grid=(depth, n6 // tn),
        in_specs=[
            pl.BlockSpec((rc, d), lambda l, j: (0, 0)),
            pl.BlockSpec((None, d, tn), lambda l, j: (l, 0, j)),
            pl.BlockSpec((None, 1, tn), lambda l, j: (l, 0, j)),
        ],
        out_specs=pl.BlockSpec((None, rc, tn), lambda l, j: (l, 0, j)),
        out_shape=jax.ShapeDtypeStruct((depth, rc, n6), F32),
        compiler_params=_cparams(("arbitrary", "arbitrary")),
        name="adaln",
    )(c_all, w_ada, b_ada.reshape(depth, 1, n6))


def _tile_mod(i, npt, ts, p_ref, s_ref):
    ms = s_ref[...]
    tiled = jnp.concatenate([ms] * ts, axis=0)
    return jnp.where(i < npt, p_ref[...], tiled)


def _mod_specs(npt, tiles_per_seq, nseq_p, bs, width, sec, col_of):
    def p_map(*idx):
        i = idx[0]
        return (jnp.minimum(i // tiles_per_seq, nseq_p - 1), 0, col_of(sec, *idx))

    def s_map(*idx):
        return (0, col_of(sec, *idx))

    return [pl.BlockSpec((None, 1, width), p_map), pl.BlockSpec((bs, width), s_map)]


def _modnorm_kernel(x_ref, g_ref, scp_ref, scs_ref, shp_ref, shs_ref, h_ref, *, npt, ts):
    i = pl.program_id(0)
    x = x_ref[...]
    xn = x * lax.rsqrt(jnp.mean(x * x, axis=-1, keepdims=True) + EPS) * g_ref[...]
    sc = _tile_mod(i, npt, ts, scp_ref, scs_ref)
    sh = _tile_mod(i, npt, ts, shp_ref, shs_ref)
    h_ref[...] = (xn * (1.0 + sc) + sh).astype(h_ref.dtype)


def _modnorm(x, gain, modp, mods, sec_sc, sec_sh, dims):
    m, d = x.shape
    tm, npt, tps, nb, bs, ts = dims
    col0 = lambda sec, i: sec
    return pl.pallas_call(
        functools.partial(_modnorm_kernel, npt=npt, ts=ts),
        grid=(m // tm,),
        in_specs=[
            pl.BlockSpec((tm, d), lambda i: (i, 0)),
            pl.BlockSpec((1, d), lambda i: (0, 0)),
            *_mod_specs(npt, tps, nb, bs, d, sec_sc, col0),
            *_mod_specs(npt, tps, nb, bs, d, sec_sh, col0),
        ],
        out_specs=pl.BlockSpec((tm, d), lambda i: (i, 0)),
        out_shape=jax.ShapeDtypeStruct((m, d), BF16),
        compiler_params=_cparams(("arbitrary",)),
        name="modnorm",
    )(x, gain, modp, mods, modp, mods)


def _proj_kernel(h_ref, *rest, nw, epilogue):
    w_refs = rest[:nw]
    o_ref = rest[nw]
    h = h_ref[...]
    acc = _dot(h, w_refs[0][...])
    if epilogue == "glu":
        acc = acc * jax.nn.sigmoid(_dot(h, w_refs[1][...]))
    elif epilogue == "sigmoid":
        acc = jax.nn.sigmoid(acc)
    o_ref[...] = acc.astype(o_ref.dtype)


def _proj(h, w, col0s, ncols, layer, out_dtype, epilogue):
    m, d = h.shape
    tn = _pick_tn(ncols)
    tm = _pick_rows(m)
    assert all(c0 % tn == 0 for c0 in col0s)
    w_specs = [pl.BlockSpec((None, d, tn), lambda j, i, off=c0 // tn: (layer, 0, off + j)) for c0 in col0s]
    return pl.pallas_call(
        functools.partial(_proj_kernel, nw=len(col0s), epilogue=epilogue),
        grid=(ncols // tn, m // tm),
        in_specs=[pl.BlockSpec((tm, d), lambda j, i: (i, 0))] + w_specs,
        out_specs=pl.BlockSpec((tm, tn), lambda j, i: (i, j)),
        out_shape=jax.ShapeDtypeStruct((m, ncols), out_dtype),
        compiler_params=_cparams(("arbitrary", "arbitrary")),
        name="proj_" + epilogue,
    )(h, *([w] * len(col0s)))


def _ssd_prompt_kernel(zx_ref, dt_ref, cw_ref, cb_ref, dtb_ref, alog_ref, dexp_ref, ng_ref, eh_ref,
                       y_ref, st_ref, xpad, y_scr, *, d, g_n, n, k_h, p):
    c = pl.program_id(1)
    L = zx_ref.shape[0]
    kp = k_h * p
    cc = cw_ref.shape[1]
    kw = cw_ref.shape[0]

    @pl.when(c == 0)
    def _():
        xpad[0:SUBLANES, :] = jnp.zeros((SUBLANES, cc), F32)
        st_ref[...] = jnp.zeros(st_ref.shape, F32)

    zx = zx_ref[...]
    z = zx[:, :d].astype(F32)
    xr = zx[:, d:].astype(F32)
    xpad[SUBLANES:SUBLANES + L, :] = xr
    conv = jnp.broadcast_to(cb_ref[...], (L, cc))
    for k in range(kw):
        off = SUBLANES - (kw - 1) + k
        conv = conv + cw_ref[k:k + 1, :] * xpad[off:off + L, :]
    xpad[0:SUBLANES, :] = xr[L - SUBLANES:, :]
    xbc = _silu(conv)
    xs = xbc[:, :d]
    bm = xbc[:, d:d + g_n * n]
    cm = xbc[:, d + g_n * n:]

    dt = _softplus(dt_ref[...] + dtb_ref[...])
    a = -jnp.exp(alog_ref[...])
    dta = dt * a
    row = lax.broadcasted_iota(jnp.int32, (L, L), 0)
    col = lax.broadcasted_iota(jnp.int32, (L, L), 1)
    causal = row >= col
    tril = jnp.where(causal, 1.0, 0.0).astype(BF16)
    a_hi, a_mid, a_lo = _split3(dta)
    acum = _dot(tril, a_hi) + _dot(tril, a_mid) + _dot(tril, a_lo)
    acum_t = acum.T
    dt_t = dt.T
    alast = acum[L - 1:L, :]
    wend = jnp.exp(alast - acum) * dt
    eac = jnp.exp(acum)
    eh = eh_ref[...]
    wend_x = _dot_parts(_split2(wend), eh)
    eac_x = _dot_parts(_split2(eac), eh)
    xw = xs * wend_x
    dec = jnp.exp(alast)
    lane = lax.broadcasted_iota(jnp.int32, (L, 2 * p), 1)

    for g in range(g_n):
        bg = bm[:, g * n:(g + 1) * n].astype(BF16)
        cg = cm[:, g * n:(g + 1) * n].astype(BF16)
        cbm = _dot_nt(cg, bg)
        s_g = st_ref[g * kp:(g + 1) * kp, :]
        yoff = _dot_nt(cg, s_g.astype(BF16))
        for j in range(k_h // 2):
            h0 = g * k_h + 2 * j
            ms = []
            for h in (h0, h0 + 1):
                seg = acum[:, h:h + 1] - acum_t[h:h + 1, :]
                dcy = jnp.exp(jnp.where(causal, seg, NEG_BIG))
                ms.append((cbm * dcy * dt_t[h:h + 1, :]).astype(BF16))
            lhs = jnp.concatenate(ms, axis=1)
            xp = xs[:, h0 * p:(h0 + 2) * p]
            rhs = jnp.concatenate([jnp.where(lane < p, xp, 0.0), jnp.where(lane >= p, xp, 0.0)],
                                  axis=0).astype(BF16)
            lo = h0 * p
            y_scr[:, lo:lo + 2 * p] = (_dot(lhs, rhs)
                                       + yoff[:, 2 * j * p:(2 * j + 2) * p] * eac_x[:, lo:lo + 2 * p])
        upd = _dot(xw[:, g * kp:(g + 1) * kp].T.astype(BF16), bg)
        for hh in range(k_h):
            h = g * k_h + hh
            r0 = h * p
            dsc = jnp.broadcast_to(dec[:, h:h + 1], (p, n))
            st_ref[r0:r0 + p, :] = dsc * s_g[hh * p:(hh + 1) * p, :] + upd[hh * p:(hh + 1) * p, :]

    y = (y_scr[...] + dexp_ref[...] * xs) * _silu(z)
    ng = ng_ref[...]
    for g in range(g_n):
        yg = y[:, g * kp:(g + 1) * kp]
        ms_ = jnp.mean(yg * yg, axis=-1, keepdims=True)
        y_ref[:, g * kp:(g + 1) * kp] = (yg * lax.rsqrt(ms_ + EPS) * ng[:, g * kp:(g + 1) * kp]).astype(y_ref.dtype)


def _ssd_prompt(zx, dtraw, cw, cb, dtb, alog, dexp, ng, eh, nb, t, hp, n, g_n, k_h, p):
    d = hp
    L = min(SSD_CHUNK, t)
    nc = t // L
    cc = cw.shape[1]
    kern = functools.partial(_ssd_prompt_kernel, d=d, g_n=g_n, n=n, k_h=k_h, p=p)
    const = lambda shape: pl.BlockSpec(shape, lambda b, c: (0,) * len(shape))
    return pl.pallas_call(
        kern,
        grid=(nb, nc),
        in_specs=[
            pl.BlockSpec((L, d + cc), lambda b, c: (b * nc + c, 0)),
            pl.BlockSpec((L, LANES), lambda b, c: (b * nc + c, 0)),
            const(cw.shape), const(cb.shape), const(dtb.shape), const(alog.shape),
            const(dexp.shape), const(ng.shape), const(eh.shape),
        ],
        out_specs=[
            pl.BlockSpec((L, d), lambda b, c: (b * nc + c, 0)),
            pl.BlockSpec((None, hp, n), lambda b, c: (b, 0, 0)),
        ],
        out_shape=[
            jax.ShapeDtypeStruct((nb * t, d), BF16),
            jax.ShapeDtypeStruct((nb, hp, n), F32),
        ],
        scratch_shapes=[pltpu.VMEM((L + SUBLANES, cc), F32), pltpu.VMEM((L, d), F32)],
        compiler_params=_cparams(("arbitrary", "arbitrary")),
        name="ssd_prompt",
    )(zx, dtraw, cw, cb, dtb, alog, dexp, ng, eh)


def _ssd_sample_pre_kernel(zx_ref, dt_ref, st_ref, cw_ref, cb_ref, dtb_ref, alog_ref, dexp_ref,
                           eh_ref, gh_ref, a_ref, ex_ref, zs_ref, xw_ref, c_ref, b_ref, dec_ref,
                           *, d, g_n, n):
    ts, bs, _ = zx_ref.shape
    kw = cw_ref.shape[0]
    gn = g_n * n
    rows = [st_ref[j] for j in range(kw - 1)]
    rows += [zx_ref[t][:, d:].astype(F32) for t in range(ts)]
    eh = eh_ref[...]
    gh = gh_ref[...]
    a = -jnp.exp(alog_ref[...])
    xs, bm, cm, dts, acs = [], [], [], [], []
    run = None
    for t in range(ts):
        conv = cb_ref[...] + cw_ref[0:1, :] * rows[t]
        for k in range(1, kw):
            conv = conv + cw_ref[k:k + 1, :] * rows[t + k]
        xbc = _silu(conv)
        xs.append(xbc[:, :d])
        bm.append(xbc[:, d:d + gn])
        cm.append(xbc[:, d + gn:])
        dtt = _softplus(dt_ref[t] + dtb_ref[...])
        dts.append(dtt)
        run = dtt * a if run is None else run + dtt * a
        acs.append(run)
    alast = acs[-1]
    dec_ref[...] = jnp.exp(alast)
    for t in range(ts):
        acc = dexp_ref[...] * xs[t]
        for j in range(t + 1):
            cbh = _dot_parts(_split2(cm[t] * bm[j]), gh)
            coef = cbh * dts[j]
            if j < t:
                coef = coef * jnp.exp(acs[t] - acs[j])
            acc = acc + _dot_parts(_split2(coef), eh) * xs[j]
        a_ref[t] = acc
        ex_ref[t] = _dot_parts(_split2(jnp.exp(acs[t])), eh)
        zs_ref[t] = _silu(zx_ref[t][:, :d].astype(F32))
        xw_ref[t] = xs[t] * _dot_parts(_split2(jnp.exp(alast - acs[t]) * dts[t]), eh)
        c_ref[t] = cm[t]
        b_ref[t] = bm[t]


def _ssd_sample_pre(zx3, dt3, tblk, ts, sconv_t, layer, cw, cb, dtb, alog, dexp, eh, gh, d, g_n, n):
    bs = zx3.shape[1]
    gn = g_n * n
    kern = functools.partial(_ssd_sample_pre_kernel, d=d, g_n=g_n, n=n)
    sb = min(SAMPLE_PRE_BLOCK, bs)
    kw = cw.shape[0]
    const = lambda a: pl.BlockSpec(a.shape, lambda i: (0,) * a.ndim)
    seq3 = lambda c: pl.BlockSpec((ts, sb, c), lambda i: (0, i, 0))
    src3 = lambda c: pl.BlockSpec((ts, sb, c), lambda i: (tblk, i, 0))
    big = jax.ShapeDtypeStruct((ts, bs, d), F32)
    small = jax.ShapeDtypeStruct((ts, bs, gn), F32)
    return pl.pallas_call(
        kern,
        grid=(bs // sb,),
        in_specs=[src3(zx3.shape[2]), src3(LANES),
                  pl.BlockSpec((None, kw - 1, sb, sconv_t.shape[3]), lambda i: (layer, 0, i, 0)),
                  const(cw), const(cb), const(dtb), const(alog), const(dexp), const(eh), const(gh)],
        out_specs=[seq3(d), seq3(d), seq3(d), seq3(d), seq3(gn), seq3(gn),
                   pl.BlockSpec((sb, LANES), lambda i: (i, 0))],
        out_shape=[big, big, big, big, small, small, jax.ShapeDtypeStruct((bs, LANES), F32)],
        compiler_params=_cparams(("arbitrary",)),
        name="ssd_sample_pre",
    )(zx3, dt3, sconv_t, cw, cb, dtb, alog, dexp, eh, gh)


def _ssd_sample_state_kernel(dec_ref, s0_ref, a_ref, ex_ref, zs_ref, xwt_ref, c_ref, b_ref, ng_ref,
                             prev_ref, y_ref, s1_ref, *, g_n, n, k_h, p, sb):
    del prev_ref
    i = pl.program_id(0)
    kp = k_h * p
    n_heads = g_n * k_h
    rows8 = sb * SUBLANES
    rid = lax.broadcasted_iota(jnp.int32, (rows8, 1), 0)
    ng = ng_ref[...]

    def body(s, carry):
        r0 = pl.multiple_of(s * SUBLANES, SUBLANES)
        in_s = (rid >= r0) & (rid < r0 + SUBLANES)
        for g in range(g_n):
            cg = c_ref[pl.ds(r0, SUBLANES), g * n:(g + 1) * n].astype(BF16)
            s_g = s0_ref[s, g * kp:(g + 1) * kp, :]
            yoff = _dot_nt(cg, s_g.astype(BF16))
            b_all = b_ref[:, g * n:(g + 1) * n]
            b_msk = jnp.where(in_s, b_all, 0.0).astype(BF16)
            upd = _dot(xwt_ref[g * kp:(g + 1) * kp, :].astype(BF16), b_msk)
            for hh in range(k_h):
                h = g * k_h + hh
                dsc = dec_ref[(i * sb + s) * n_heads + h]
                s1_ref[s, h * p:(h + 1) * p, :] = (dsc * s_g[hh * p:(hh + 1) * p, :]
                                                   + upd[hh * p:(hh + 1) * p, :])
            cols = slice(g * kp, (g + 1) * kp)
            yg = ((a_ref[pl.ds(r0, SUBLANES), cols] + ex_ref[pl.ds(r0, SUBLANES), cols] * yoff)
                  * zs_ref[pl.ds(r0, SUBLANES), cols])
            ms_ = jnp.mean(yg * yg, axis=-1, keepdims=True)
            y_ref[pl.ds(r0, SUBLANES), cols] = (yg * lax.rsqrt(ms_ + EPS) * ng[:, cols]).astype(y_ref.dtype)
        return carry

    lax.fori_loop(0, sb, body, 0)


def _ssd_sample_state(dec, s0_all, layer, s1_prev, a8, ex8, zs8, xwt, c8, b8, ng, g_n, n, k_h, p):
    depth, bs, hp, _ = s0_all.shape
    d = a8.shape[1]
    gn = g_n * n
    sb = min(SAMPLE_SEQ_BLOCK, bs)
    r8 = sb * SUBLANES
    kern = functools.partial(_ssd_sample_state_kernel, g_n=g_n, n=n, k_h=k_h, p=p, sb=sb)
    return pl.pallas_call(
        kern,
        grid=(bs // sb,),
        in_specs=[
            pl.BlockSpec(memory_space=pltpu.SMEM),
            pl.BlockSpec((None, sb, hp, n), lambda i: (layer, i, 0, 0)),
            pl.BlockSpec((r8, d), lambda i: (i, 0)),
            pl.BlockSpec((r8, d), lambda i: (i, 0)),
            pl.BlockSpec((r8, d), lambda i: (i, 0)),
            pl.BlockSpec((None, d, r8), lambda i: (i, 0, 0)),
            pl.BlockSpec((r8, gn), lambda i: (i, 0)),
            pl.BlockSpec((r8, gn), lambda i: (i, 0)),
            pl.BlockSpec((1, d), lambda i: (0, 0)),
            pl.BlockSpec(memory_space=pl.ANY),
        ],
        out_specs=[
            pl.BlockSpec((r8, d), lambda i: (i, 0)),
            pl.BlockSpec((None, sb, hp, n), lambda i: (layer, i, 0, 0)),
        ],
        out_shape=[
            jax.ShapeDtypeStruct((bs * SUBLANES, d), F32),
            jax.ShapeDtypeStruct((depth, bs, hp, n), F32),
        ],
        input_output_aliases={9: 1},
        compiler_params=_cparams(("arbitrary",)),
        name="ssd_sample_state",
    )(dec, s0_all, a8, ex8, zs8, xwt, c8, b8, ng, s1_prev)


def _ln_swish(v, g, b):
    mu = jnp.mean(v, axis=-1, keepdims=True)
    cen = v - mu
    var = jnp.mean(cen * cen, axis=-1, keepdims=True)
    return _silu(cen * lax.rsqrt(var + EPS) * g + b)


def _conf_prompt_kernel(u_ref, w_ref, b_ref, lg_ref, lb_ref, v_ref, upad, ush, *, halo):
    c = pl.program_id(1)
    L, d = u_ref.shape
    kw = w_ref.shape[0]
    span = L + halo - SUBLANES

    @pl.when(c == 0)
    def _():
        upad[0:halo, :] = jnp.zeros((halo, d), F32)

    u = u_ref[...].astype(F32)
    upad[halo:halo + L, :] = u
    for r in range(1, SUBLANES):
        ush[r - 1, 0:span, :] = upad[r:r + span, :]
    acc = jnp.broadcast_to(b_ref[...], (L, d))
    for k in range(kw):
        q, r = divmod(halo - (kw - 1) + k, SUBLANES)
        src = upad[q * SUBLANES:q * SUBLANES + L, :] if r == 0 else ush[r - 1, q * SUBLANES:q * SUBLANES + L, :]
        acc = acc + w_ref[k:k + 1, :] * src
    upad[0:halo, :] = u[L - halo:, :]
    v_ref[...] = _ln_swish(acc, lg_ref[...], lb_ref[...]).astype(v_ref.dtype)


def _conf_prompt(u, w, b, lg, lb, nb, t):
    d = u.shape[1]
    kw = w.shape[0]
    L = min(CONF_CHUNK, t)
    nc = t // L
    halo = -(-(kw - 1) // SUBLANES) * SUBLANES
    const = lambda a: pl.BlockSpec(a.shape, lambda bb, c: (0,) * a.ndim)
    return pl.pallas_call(
        functools.partial(_conf_prompt_kernel, halo=halo),
        grid=(nb, nc),
        in_specs=[pl.BlockSpec((L, d), lambda bb, c: (bb * nc + c, 0)), const(w), const(b), const(lg), const(lb)],
        out_specs=pl.BlockSpec((L, d), lambda bb, c: (bb * nc + c, 0)),
        out_shape=jax.ShapeDtypeStruct((nb * t, d), BF16),
        scratch_shapes=[pltpu.VMEM((L + halo, d), F32), pltpu.VMEM((SUBLANES - 1, L + halo, d), F32)],
        compiler_params=_cparams(("arbitrary", "arbitrary")),
        name="conf_prompt",
    )(u, w, b, lg, lb)


def _conf_sample_kernel(u_ref, st_ref, w_ref, b_ref, lg_ref, lb_ref, v_ref):
    ts, sb, d = u_ref.shape
    kw = w_ref.shape[0]
    rows = [st_ref[j] for j in range(kw - 1)]
    rows += [u_ref[t].astype(F32) for t in range(ts)]
    for t in range(ts):
        acc = b_ref[...] + w_ref[0:1, :] * rows[t]
        for k in range(1, kw):
            acc = acc + w_ref[k:k + 1, :] * rows[t + k]
        v_ref[t] = _ln_swish(acc, lg_ref[...], lb_ref[...]).astype(v_ref.dtype)


def _conf_sample(u3, tblk, ts, cconv_t, layer, w, b, lg, lb):
    _, bs, d = u3.shape
    kw = w.shape[0]
    sb = min(CONF_SAMPLE_BLOCK, bs)
    const = lambda a: pl.BlockSpec(a.shape, lambda i: (0,) * a.ndim)
    return pl.pallas_call(
        _conf_sample_kernel,
        grid=(bs // sb,),
        in_specs=[
            pl.BlockSpec((ts, sb, d), lambda i: (tblk, i, 0)),
            pl.BlockSpec((None, kw - 1, sb, d), lambda i: (layer, 0, i, 0)),
            const(w), const(b), const(lg), const(lb),
        ],
        out_specs=pl.BlockSpec((ts, sb, d), lambda i: (0, i, 0)),
        out_shape=jax.ShapeDtypeStruct((ts, bs, d), BF16),
        compiler_params=_cparams(("arbitrary",)),
        name="conf_sample",
    )(u3, cconv_t, w, b, lg, lb)


def _merge_kernel(yp_ref, ys_ref, vp_ref, vs_ref, ws_ref, wc_ref, gs_ref, gc_ref, o_ref, *, npt):
    is_p = pl.program_id(0) < npt
    y = jnp.where(is_p, yp_ref[...], ys_ref[...])
    v = jnp.where(is_p, vp_ref[...], vs_ref[...])
    a = _dot(y, ws_ref[...])
    b = _dot(v, wc_ref[...])
    o_ref[...] = (gs_ref[...].astype(F32) * a + gc_ref[...].astype(F32) * b).astype(o_ref.dtype)


def _merge(y_p, y_s, v_p, v_s, ws, wc, layer, gates, tm):
    m, d = gates.shape[0], y_p.shape[1]
    npt = y_p.shape[0] // tm
    tn = _pick_tn(d)
    nj = d // tn
    p_spec = pl.BlockSpec((tm, d), lambda i, j: (jnp.minimum(i, npt - 1), 0))
    s_spec = pl.BlockSpec((tm, d), lambda i, j: (0, 0))
    return pl.pallas_call(
        functools.partial(_merge_kernel, npt=npt),
        grid=(m // tm, nj),
        in_specs=[
            p_spec, s_spec, p_spec, s_spec,
            pl.BlockSpec((None, d, tn), lambda i, j: (layer, 0, j)),
            pl.BlockSpec((None, d, tn), lambda i, j: (layer, 0, j)),
            pl.BlockSpec((tm, tn), lambda i, j: (i, j)),
            pl.BlockSpec((tm, tn), lambda i, j: (i, nj + j)),
        ],
        out_specs=pl.BlockSpec((tm, tn), lambda i, j: (i, j)),
        out_shape=jax.ShapeDtypeStruct((m, d), BF16),
        compiler_params=_cparams(("arbitrary", "arbitrary")),
        name="merge",
    )(y_p, y_s, v_p, v_s, ws, wc, gates, gates)


def _outproj_kernel(mx_ref, w_ref, x_ref, gp_ref, gs_ref, o_ref, *, npt, ts):
    gate = _tile_mod(pl.program_id(0), npt, ts, gp_ref, gs_ref)
    o_ref[...] = x_ref[...] + gate * _dot(mx_ref[...], w_ref[...])


def _outproj(mixed, wo, layer, x, modp, mods, sec, dims):
    m, d = x.shape
    tm, npt, tps, nb, bs, ts = dims
    tn = _pick_tn(d)
    nj = d // tn
    colj = lambda s, i, j: s * nj + j
    return pl.pallas_call(
        functools.partial(_outproj_kernel, npt=npt, ts=ts),
        grid=(m // tm, nj),
        in_specs=[
            pl.BlockSpec((tm, d), lambda i, j: (i, 0)),
            pl.BlockSpec((None, d, tn), lambda i, j: (layer, 0, j)),
            pl.BlockSpec((tm, tn), lambda i, j: (i, j)),
            *_mod_specs(npt, tps, nb, bs, tn, sec, colj),
        ],
        out_specs=pl.BlockSpec((tm, tn), lambda i, j: (i, j)),
        out_shape=jax.ShapeDtypeStruct((m, d), F32),
        compiler_params=_cparams(("arbitrary", "arbitrary")),
        name="outproj",
    )(mixed, wo, x, modp, mods)


def _pack_bf16_pairs(v):
    half = v.shape[1] // 2
    bits = pltpu.bitcast(v.astype(BF16).astype(F32), jnp.uint32)
    return bits[:, :half] | (bits[:, half:] >> 16)


def _unpack_bf16_pairs(w):
    left = pltpu.bitcast(w & jnp.uint32(0xFFFF0000), F32)
    right = pltpu.bitcast(w << 16, F32)
    return jnp.concatenate([left, right], axis=1)


INFO_RANK1, INFO_RANK2, INFO_E1, INFO_E2, INFO_W1, INFO_W2 = range(6)


def _router_kernel(x_ref, g_ref, scp_ref, scs_ref, shp_ref, shs_ref, whi_ref, wlo_ref, rb_ref,
                   hp_ref, info_ref, cnt_ref, *, npt, ts, n_e, n_g):
    i = pl.program_id(0)
    epg = n_e // n_g

    @pl.when(i == 0)
    def _():
        cnt_ref[...] = jnp.zeros(cnt_ref.shape, F32)

    x = x_ref[...]
    xn = x * lax.rsqrt(jnp.mean(x * x, axis=-1, keepdims=True) + EPS) * g_ref[...]
    sc = _tile_mod(i, npt, ts, scp_ref, scs_ref)
    sh = _tile_mod(i, npt, ts, shp_ref, shs_ref)
    h = xn * (1.0 + sc) + sh
    hi, lo = _split2(h)
    hp_ref[...] = _pack_bf16_pairs(h)
    logit = _dot(hi, whi_ref[...]) + _dot(lo, whi_ref[...]) + _dot(hi, wlo_ref[...]) + rb_ref[...]
    tm = logit.shape[0]
    lane = lax.broadcasted_iota(jnp.int32, (tm, LANES), 1).astype(F32)
    first = lambda hit: jnp.min(jnp.where(hit, lane, float(LANES)), axis=-1, keepdims=True)
    gl = jnp.where(lane >= n_e, jnp.where(lane < n_e + n_g, logit, NEG_BIG), NEG_BIG)
    gmax = jnp.max(gl, axis=-1, keepdims=True)
    g_top = 1.0 / jnp.sum(jnp.exp(gl - gmax), axis=-1, keepdims=True)
    g_idx = first(gl == gmax) - n_e
    el = jnp.where(lane >= g_idx * epg, jnp.where(lane < (g_idx + 1.0) * epg, logit, NEG_BIG), NEG_BIG)
    m1 = jnp.max(el, axis=-1, keepdims=True)
    i1 = first(el == m1)
    el2 = jnp.where(lane == i1, NEG_BIG, el)
    m2 = jnp.max(el2, axis=-1, keepdims=True)
    i2 = first(el2 == m2)
    r = jnp.exp(m2 - m1)
    w1 = 1.0 / (1.0 + r)
    w2 = r * w1
    oh1 = jnp.where(lane == i1, 1.0, 0.0)
    oh2 = jnp.where(lane == i2, 1.0, 0.0)
    both = oh1 + oh2
    row = lax.broadcasted_iota(jnp.int32, (tm, tm), 0)
    col = lax.broadcasted_iota(jnp.int32, (tm, tm), 1)
    before = _dot(jnp.where(row > col, 1.0, 0.0).astype(BF16), both.astype(BF16)) + cnt_ref[0:1, :]
    rank1 = jnp.sum(before * oh1, axis=-1, keepdims=True)
    rank2 = jnp.sum(before * oh2, axis=-1, keepdims=True)
    cnt_ref[...] = cnt_ref[...] + jnp.sum(both, axis=0, keepdims=True)
    rec = [rank1, rank2, i1, i2, g_top * w1, g_top * w2]
    info = jnp.zeros((tm, LANES), F32)
    for k, val in enumerate(rec):
        info = jnp.where(lane == float(k), val, info)
    info_ref[...] = info


def _router(x, gain, modp, mods, sec_sc, sec_sh, whi, wlo, rb, n_e, n_g, dims):
    m, d = x.shape
    tm, npt, tps, nb, bs, ts = dims
    col0 = lambda sec, i: sec
    return pl.pallas_call(
        functools.partial(_router_kernel, npt=npt, ts=ts, n_e=n_e, n_g=n_g),
        grid=(m // tm,),
        in_specs=[
            pl.BlockSpec((tm, d), lambda i: (i, 0)),
            pl.BlockSpec((1, d), lambda i: (0, 0)),
            *_mod_specs(npt, tps, nb, bs, d, sec_sc, col0),
            *_mod_specs(npt, tps, nb, bs, d, sec_sh, col0),
            pl.BlockSpec((d, LANES), lambda i: (0, 0)),
            pl.BlockSpec((d, LANES), lambda i: (0, 0)),
            pl.BlockSpec((1, LANES), lambda i: (0, 0)),
        ],
        out_specs=[pl.BlockSpec((tm, d // 2), lambda i: (i, 0)), pl.BlockSpec((tm, LANES), lambda i: (i, 0)),
                   pl.BlockSpec((SUBLANES, LANES), lambda i: (0, 0))],
        out_shape=[jax.ShapeDtypeStruct((m, d // 2), jnp.uint32), jax.ShapeDtypeStruct((m, LANES), F32),
                   jax.ShapeDtypeStruct((SUBLANES, LANES), F32)],
        compiler_params=_cparams(("arbitrary",)),
        name="router",
    )(x, gain, modp, mods, modp, mods, whi, wlo, rb)


def _slots_kernel(info_ref, offs_ref, pos_ref):
    info = info_ref[...]
    lane = lax.broadcasted_iota(jnp.int32, info.shape, 1).astype(F32)
    offs = offs_ref[...]
    base1 = jnp.sum(jnp.where(lane == info[:, INFO_E1:INFO_E1 + 1], offs, 0.0), axis=-1, keepdims=True)
    base2 = jnp.sum(jnp.where(lane == info[:, INFO_E2:INFO_E2 + 1], offs, 0.0), axis=-1, keepdims=True)
    rec = jnp.where(lane == 0.0, info[:, INFO_RANK1:INFO_RANK1 + 1] + base1,
                    jnp.where(lane == 1.0, info[:, INFO_RANK2:INFO_RANK2 + 1] + base2, 0.0))
    pos_ref[...] = rec.T[0:SUBLANES, :].astype(jnp.int32)


def _slots(info, offs_row, tm):
    m = info.shape[0]
    return pl.pallas_call(
        _slots_kernel,
        grid=(m // tm,),
        in_specs=[pl.BlockSpec((tm, LANES), lambda i: (i, 0)), pl.BlockSpec((1, LANES), lambda i: (0, 0))],
        out_specs=pl.BlockSpec((SUBLANES, tm), lambda i: (0, i)),
        out_shape=jax.ShapeDtypeStruct((SUBLANES, m), jnp.int32),
        compiler_params=_cparams(("arbitrary",)),
        name="moe_slots",
    )(info, offs_row)


def _moe_plan(info, cnt, n_e, tg, nt_max, tm):
    counts = cnt[0, :n_e].astype(jnp.int32)
    tiles = (counts + tg - 1) // tg
    tile_end = jnp.cumsum(tiles)
    offs = (tile_end - tiles) * tg
    n_used = tile_end[n_e - 1]
    offs_row = jnp.zeros((1, LANES), F32).at[0, :n_e].set(offs.astype(F32))
    pos = _slots(info, offs_row, tm)
    pos1, pos2 = pos[0], pos[1]
    tile = jnp.minimum(jnp.arange(nt_max, dtype=jnp.int32), n_used - 1)
    tile_expert = jnp.searchsorted(tile_end, tile, side="right").astype(jnp.int32)
    return pos1, pos2, tile_expert, n_used.reshape(1)


def _dispatch_kernel(pos1_ref, pos2_ref, hp_ref, xs0_ref, xs_ref, sem):
    del xs0_ref
    tm = hp_ref.shape[0]
    base = pl.program_id(0) * tm

    def issue(r, carry):
        src = hp_ref.at[pl.ds(r, 1)]
        pltpu.make_async_copy(src, xs_ref.at[pl.ds(pos1_ref[base + r], 1)], sem).start(priority=0)
        pltpu.make_async_copy(src, xs_ref.at[pl.ds(pos2_ref[base + r], 1)], sem).start(priority=1)
        return carry
    lax.fori_loop(0, tm, issue, 0, unroll=ROW_DMA_UNROLL)

    def drain(r, carry):
        for _ in range(2):
            pltpu.make_async_copy(hp_ref.at[pl.ds(0, 1)], xs_ref.at[pl.ds(0, 1)], sem).wait()
        return carry
    lax.fori_loop(0, tm, drain, 0, unroll=ROW_DMA_UNROLL)


def _dispatch(pos1, pos2, hp, n_slots, tm):
    m, half = hp.shape
    smem = pl.BlockSpec(memory_space=pltpu.SMEM)
    hbm = pl.BlockSpec(memory_space=pl.ANY)
    return pl.pallas_call(
        _dispatch_kernel,
        grid=(m // tm,),
        in_specs=[smem, smem, pl.BlockSpec((tm, half), lambda i: (i, 0)), hbm],
        out_specs=hbm,
        out_shape=jax.ShapeDtypeStruct((n_slots, half), jnp.uint32),
        scratch_shapes=[pltpu.SemaphoreType.DMA(())],
        input_output_aliases={3: 0},
        compiler_params=_cparams(("arbitrary",)),
        name="moe_dispatch",
    )(pos1, pos2, hp, jnp.zeros((n_slots, half), jnp.uint32))


def _experts_kernel(te_ref, nu_ref, xs_ref, wg_ref, wu_ref, wd_ref, ys_ref):
    del te_ref
    i = pl.program_id(0)

    @pl.when(i < nu_ref[0])
    def _():
        x = _unpack_bf16_pairs(xs_ref[...]).astype(BF16)
        hid = _silu(_dot(x, wg_ref[...].astype(BF16))) * _dot(x, wu_ref[...].astype(BF16))
        ys_ref[...] = _pack_bf16_pairs(_dot(hid.astype(BF16), wd_ref[...].astype(BF16)))

    @pl.when(i >= nu_ref[0])
    def _():
        ys_ref[...] = jnp.zeros(ys_ref.shape, jnp.uint32)


def _experts(tile_expert, n_used, xs, wg, wu, wd, layer, tg):
    n_slots, half = xs.shape
    _, _, d, f = wg.shape
    grid_spec = pltpu.PrefetchScalarGridSpec(
        num_scalar_prefetch=2,
        grid=(n_slots // tg,),
        in_specs=[
            pl.BlockSpec((tg, half), lambda i, te, nu: (i, 0)),
            pl.BlockSpec((None, None, d, f), lambda i, te, nu: (layer, te[i], 0, 0)),
            pl.BlockSpec((None, None, d, f), lambda i, te, nu: (layer, te[i], 0, 0)),
            pl.BlockSpec((None, None, f, d), lambda i, te, nu: (layer, te[i], 0, 0)),
        ],
        out_specs=pl.BlockSpec((tg, half), lambda i, te, nu: (i, 0)),
    )
    return pl.pallas_call(
        _experts_kernel,
        grid_spec=grid_spec,
        out_shape=jax.ShapeDtypeStruct((n_slots, half), jnp.uint32),
        compiler_params=_cparams(("arbitrary",)),
        name="moe_experts",
    )(tile_expert, n_used, xs, wg, wu, wd)


def _combine_kernel(pos1_ref, pos2_ref, info_ref, x_ref, gp_ref, gs_ref, ys_ref, o_ref, buf_a, buf_b, sem,
                    *, npt, ts):
    i = pl.program_id(0)
    tm = x_ref.shape[0]
    slot = lax.rem(i, 2)

    def issue(tile, s):
        def body(r, carry):
            t = tile * tm + r
            pltpu.make_async_copy(ys_ref.at[pl.ds(pos1_ref[t], 1)], buf_a.at[s, pl.ds(r, 1)], sem.at[s]).start(priority=0)
            pltpu.make_async_copy(ys_ref.at[pl.ds(pos2_ref[t], 1)], buf_b.at[s, pl.ds(r, 1)], sem.at[s]).start(priority=1)
            return carry
        lax.fori_loop(0, tm, body, 0, unroll=ROW_DMA_UNROLL)

    @pl.when(i == 0)
    def _():
        issue(0, 0)

    @pl.when(i + 1 < pl.num_programs(0))
    def _():
        issue(i + 1, 1 - slot)

    def wait_body(r, carry):
        pltpu.make_async_copy(ys_ref.at[pl.ds(0, 1)], buf_a.at[slot, pl.ds(0, 1)], sem.at[slot]).wait()
        pltpu.make_async_copy(ys_ref.at[pl.ds(0, 1)], buf_b.at[slot, pl.ds(0, 1)], sem.at[slot]).wait()
        return carry
    lax.fori_loop(0, tm, wait_body, 0, unroll=ROW_DMA_UNROLL)

    info = info_ref[...]
    w1 = info[:, INFO_W1:INFO_W1 + 1]
    w2 = info[:, INFO_W2:INFO_W2 + 1]
    moe = w1 * _unpack_bf16_pairs(buf_a[slot]) + w2 * _unpack_bf16_pairs(buf_b[slot])
    o_ref[...] = x_ref[...] + _tile_mod(i, npt, ts, gp_ref, gs_ref) * moe


def _combine(pos1, pos2, info, x, modp, mods, sec, ys, dims):
    m, d = x.shape
    tm, npt, tps, nb, bs, ts = dims
    col0 = lambda s, i: s
    smem = pl.BlockSpec(memory_space=pltpu.SMEM)
    return pl.pallas_call(
        functools.partial(_combine_kernel, npt=npt, ts=ts),
        grid=(m // tm,),
        in_specs=[
            smem, smem,
            pl.BlockSpec((tm, LANES), lambda i: (i, 0)),
            pl.BlockSpec((tm, d), lambda i: (i, 0)),
            *_mod_specs(npt, tps, nb, bs, d, sec, col0),
            pl.BlockSpec(memory_space=pl.ANY),
        ],
        out_specs=pl.BlockSpec((tm, d), lambda i: (i, 0)),
        out_shape=jax.ShapeDtypeStruct((m, d), F32),
        scratch_shapes=[pltpu.VMEM((2, tm, d // 2), jnp.uint32), pltpu.VMEM((2, tm, d // 2), jnp.uint32),
                        pltpu.SemaphoreType.DMA((2,))],
        compiler_params=_cparams(("arbitrary",)),
        name="moe_combine",
    )(pos1, pos2, info, x, modp, mods, ys)


def _moe(hp, info, cnt, wg, wu, wd, layer, x, modp, mods, sec, dims):
    m = x.shape[0]
    n_e = wg.shape[1]
    tg = MOE_TG
    nt_max = -(-2 * m // tg) + n_e
    pos1, pos2, tile_expert, n_used = _moe_plan(info, cnt, n_e, tg, nt_max, dims[0])
    xs = _dispatch(pos1, pos2, hp, nt_max * tg, dims[0])
    ys = _experts(tile_expert, n_used, xs, wg, wu, wd, layer, tg)
    return _combine(pos1, pos2, info, x, modp, mods, sec, ys, dims)


def _final_norm_kernel(x_ref, g_ref, o_ref):
    x = x_ref[...]
    o_ref[...] = x * lax.rsqrt(jnp.mean(x * x, axis=-1, keepdims=True) + EPS) * g_ref[...]


def _final_norm(x, g, tm, tile0, ntiles):
    d = x.shape[1]
    return pl.pallas_call(
        _final_norm_kernel,
        grid=(ntiles,),
        in_specs=[pl.BlockSpec((tm, d), lambda i: (tile0 + i, 0)), pl.BlockSpec((1, d), lambda i: (0, 0))],
        out_specs=pl.BlockSpec((tm, d), lambda i: (i, 0)),
        out_shape=jax.ShapeDtypeStruct((ntiles * tm, d), F32),
        compiler_params=_cparams(("arbitrary",)),
        name="final_norm",
    )(x, g)


def _to_seq8(a):
    ts, bs, c = a.shape
    a = jnp.transpose(a, (1, 0, 2))
    a = jnp.pad(a, ((0, 0), (0, SUBLANES - ts), (0, 0)))
    return a.reshape(bs * SUBLANES, c)


def kernel(x_prompt, x_sample, c_prompt, c_sample, state_ssm, state_ssd_conv, state_conf_conv, norm_mix_g, norm_ffn_g, w_ada, b_ada, w_in, ssd_conv_w, ssd_conv_b, ssd_dt_bias, ssd_a_log, ssd_d, ssd_norm_g, w_ssd_out, conf_conv_w, conf_conv_b, conf_ln_g, conf_ln_b, w_conf_out, w_o, w_group_router, b_group_router, w_expert_router, b_expert_router, w_exp_gate, w_exp_up, w_exp_down, final_norm_g):
    nb, t, d = x_prompt.shape
    bs, ts, _ = x_sample.shape
    depth = w_in.shape[0]
    n_h = ssd_dt_bias.shape[1]
    p = d // n_h
    n = state_ssm.shape[-1]
    cc = ssd_conv_w.shape[-1]
    g_n = (cc - d) // (2 * n)
    k_h = n_h // g_n
    gn = g_n * n
    kc = conf_conv_w.shape[1]
    n_e = w_exp_gate.shape[1]
    n_g = w_group_router.shape[-1]
    tm = bs * ts
    assert t % tm == 0 and tm % SUBLANES == 0 and bs % SUBLANES == 0 and ts <= SUBLANES
    assert 2 * p == LANES and k_h % 2 == 0 and n_h <= LANES and n_e + n_g <= LANES
    npt = nb * t // tm
    dims = (tm, npt, t // tm, nb, bs, ts)
    mp = nb * t

    x = jnp.concatenate([x_prompt.reshape(mp, d), jnp.transpose(x_sample, (1, 0, 2)).reshape(tm, d)], axis=0)

    rc = -(-(nb + bs) // SUBLANES) * SUBLANES
    c_all = jnp.concatenate([c_prompt, c_sample, jnp.zeros((rc - nb - bs, d), F32)], axis=0)
    mod = _adaln(c_all, w_ada, b_ada)

    head_of = jnp.arange(d) // p
    eh = (jnp.arange(LANES)[:, None] == head_of[None, :]).astype(BF16)
    gh = ((jnp.arange(gn)[:, None] // n) == (jnp.arange(LANES)[None, :] // k_h)).astype(BF16)
    gh = gh * (jnp.arange(LANES)[None, :] < n_h).astype(BF16)
    pad_h = lambda v: jnp.pad(v, (0, LANES - n_h)).reshape(1, LANES)

    m = mp + tm
    tblk = mp // bs // ts
    assert tblk * bs * ts == mp
    sconv_t = jnp.transpose(state_ssd_conv, (0, 2, 1, 3))
    cconv_t = jnp.transpose(state_conf_conv, (0, 2, 1, 3))
    ssm0_all = state_ssm.reshape(depth, bs, n_h * p, n)
    ssm_s = jnp.zeros(ssm0_all.shape, F32)
    o_dt, o_glu = d + cc, d + cc + n_h
    w_main = w_in[:, :, :o_dt].astype(BF16)
    w_dt = jnp.pad(w_in[:, :, o_dt:o_glu], ((0, 0), (0, 0), (0, LANES - n_h))).astype(BF16)
    w_rest = w_in[:, :, o_glu:].astype(BF16)
    w_so, w_co, w_oo = w_ssd_out.astype(BF16), w_conf_out.astype(BF16), w_o.astype(BF16)
    ssm_p, sconv_p, cconv_p, sconv_s, cconv_s = [], [], [], [], []
    for l in range(depth):
        modp = mod[l, :nb].reshape(nb, 1, 6 * d)
        mods = mod[l, nb:nb + bs]
        h1 = _modnorm(x, norm_mix_g[l].reshape(1, d), modp, mods, 1, 0, dims)
        zx = _proj(h1, w_main, [0], o_dt, l, BF16, "plain")
        dtraw = _proj(h1, w_dt, [0], LANES, l, F32, "plain")
        u = _proj(h1, w_rest, [0, d], d, l, BF16, "glu")
        gates = _proj(h1, w_rest, [2 * d], 2 * d, l, BF16, "sigmoid")

        cw = ssd_conv_w[l]
        cb = ssd_conv_b[l].reshape(1, cc)
        dtb = pad_h(ssd_dt_bias[l])
        alog = pad_h(ssd_a_log[l])
        dexp = jnp.repeat(ssd_d[l], p).reshape(1, d)
        ng = ssd_norm_g[l].reshape(1, d)

        y_p, s_p = _ssd_prompt(zx, dtraw, cw, cb, dtb, alog, dexp, ng, eh, nb, t, n_h * p, n, g_n, k_h, p)
        zx_s = zx[mp:].reshape(ts, bs, d + cc)
        sconv0 = state_ssd_conv[l]
        a_t, ex_t, zs_t, xw_t, c_t, b_t, dec = _ssd_sample_pre(
            zx.reshape(m // bs, bs, d + cc), dtraw.reshape(m // bs, bs, LANES), tblk, ts, sconv_t, l,
            cw, cb, dtb, alog, dexp, eh, gh, d, g_n, n)
        sb = min(SAMPLE_SEQ_BLOCK, bs)
        xwt = _to_seq8(xw_t).reshape(bs // sb, sb * SUBLANES, d).transpose(0, 2, 1)
        dec = dec[:, :n_h].reshape(bs * n_h)
        y_s8, ssm_s = _ssd_sample_state(dec, ssm0_all, l, ssm_s, _to_seq8(a_t), _to_seq8(ex_t),
                                        _to_seq8(zs_t), xwt, _to_seq8(c_t), _to_seq8(b_t), ng, g_n, n, k_h, p)
        y_s = jnp.transpose(y_s8.reshape(bs, SUBLANES, d)[:, :ts], (1, 0, 2)).reshape(tm, d).astype(BF16)

        ccw = conf_conv_w[l]
        ccb = conf_conv_b[l].reshape(1, d)
        lg = conf_ln_g[l].reshape(1, d)
        lb = conf_ln_b[l].reshape(1, d)
        v_p = _conf_prompt(u, ccw, ccb, lg, lb, nb, t)
        u_s = u[mp:].reshape(ts, bs, d)
        cconv0 = state_conf_conv[l]
        v_s = _conf_sample(u.reshape(m // bs, bs, d), tblk, ts, cconv_t, l, ccw, ccb, lg, lb).reshape(tm, d)

        mixed = _merge(y_p, y_s, v_p, v_s, w_so, w_co, l, gates, tm)
        x = _outproj(mixed, w_oo, l, x, modp, mods, 2, dims)

        w_r = jnp.concatenate([w_expert_router[l], w_group_router[l],
                               jnp.zeros((d, LANES - n_e - n_g), F32)], axis=1)
        w_r_hi = w_r.astype(BF16)
        w_r_lo = (w_r - w_r_hi.astype(F32)).astype(BF16)
        rb = jnp.concatenate([b_expert_router[l], b_group_router[l], jnp.zeros((LANES - n_e - n_g,), F32)]).reshape(1, LANES)
        hp, info, cnt = _router(x, norm_ffn_g[l].reshape(1, d), modp, mods, 4, 3, w_r_hi, w_r_lo, rb, n_e, n_g, dims)
        x = _moe(hp, info, cnt, w_exp_gate, w_exp_up, w_exp_down, l, x, modp, mods, 5, dims)

        ssm_p.append(s_p.reshape(nb, n_h, p, n))
        k3 = sconv0.shape[1]
        sconv_p.append(jnp.stack([zx[(b + 1) * t - k3:(b + 1) * t, d:] for b in range(nb)]).astype(F32))
        cconv_p.append(jnp.stack([u[(b + 1) * t - (kc - 1):(b + 1) * t] for b in range(nb)]).astype(F32))
        xraw_s = jnp.transpose(zx_s[:, :, d:], (1, 0, 2)).astype(F32)
        sconv_s.append(jnp.concatenate([sconv0, xraw_s], axis=1)[:, ts:])
        cconv_s.append(jnp.concatenate([cconv0, jnp.transpose(u_s, (1, 0, 2)).astype(F32)], axis=1)[:, ts:])

    fg = final_norm_g.reshape(1, d)
    y_prompt = _final_norm(x, fg, tm, 0, npt).reshape(nb, t, d)
    y_sample = jnp.transpose(_final_norm(x, fg, tm, npt, 1).reshape(ts, bs, d), (1, 0, 2))
    return (y_prompt, y_sample, jnp.stack(ssm_p), jnp.stack(sconv_p), jnp.stack(cconv_p),
            ssm_s.reshape(depth, bs, n_h, p, n), jnp.stack(sconv_s), jnp.stack(cconv_s))
```

```python
import functools
import math

import jax
import jax.numpy as jnp
from jax import lax
from jax.experimental import pallas as pl
from jax.experimental.pallas import tpu as pltpu

F32 = jnp.float32
BF16 = jnp.bfloat16
EPS = 1e-6
LANES = 128
SUBLANES = 8
VMEM_LIMIT = 52 * 1024 * 1024
NEG_BIG = -1e30
SSD_CHUNK = 128
CONF_CHUNK = 256
SAMPLE_SEQ_BLOCK = 8
SAMPLE_PRE_BLOCK = 32
CONF_SAMPLE_BLOCK = 16
MOE_TG = 256
WCAST_ROWS = 512
ROW_DMA_UNROLL = 8
ADA_TN = 1024
PROJ_TN = 1024
PROJ_TM = 1088


def _pick_tn(ncols):
    assert ncols % LANES == 0
    return max(tn for tn in range(LANES, min(PROJ_TN, ncols) + 1, LANES) if ncols % tn == 0)


def _pick_rows(m):
    return max(tm for tm in range(2 * SUBLANES, min(PROJ_TM, m) + 1, 2 * SUBLANES) if m % tm == 0)


def _cparams(sem):
    return pltpu.CompilerParams(dimension_semantics=sem, vmem_limit_bytes=VMEM_LIMIT)


def _silu(x):
    return x * jax.nn.sigmoid(x)


def _softplus(x):
    return jnp.maximum(x, 0.0) + jnp.log1p(jnp.exp(-jnp.abs(x)))


def _split2(x):
    hi = x.astype(BF16)
    lo = (x - hi.astype(F32)).astype(BF16)
    return hi, lo


def _split3(x):
    hi = x.astype(BF16)
    r = x - hi.astype(F32)
    mid = r.astype(BF16)
    lo = (r - mid.astype(F32)).astype(BF16)
    return hi, mid, lo


def _dot(a, b):
    return jnp.dot(a, b, preferred_element_type=F32)


def _dot_nt(a, b):
    return lax.dot_general(a, b, (((1,), (1,)), ((), ())), preferred_element_type=F32)


def _dot_parts(parts, w):
    acc = _dot(parts[0], w)
    for p in parts[1:]:
        acc = acc + _dot(p, w)
    return acc


def _adaln_kernel(c_ref, w_ref, b_ref, o_ref):
    s = _silu(c_ref[...]).astype(BF16)
    o_ref[...] = _dot(s, w_ref[...].astype(BF16)) + b_ref[...]


def _adaln(c_all, w_ada, b_ada):
    depth, d, n6 = w_ada.shape
    rc = c_all.shape[0]
    tn = min(ADA_TN, n6)
    return pl.pallas_call(
        _adaln_kernel,
        grid=(depth, n6 // tn),
        in_specs=[
            pl.BlockSpec((rc, d), lambda l, j: (0, 0)),
            pl.BlockSpec((None, d, tn), lambda l, j: (l, 0, j)),
            pl.BlockSpec((None, 1, tn), lambda l, j: (l, 0, j)),
        ],
        out_specs=pl.BlockSpec((None, rc, tn), lambda l, j: (l, 0, j)),
        out_shape=jax.ShapeDtypeStruct((depth, rc, n6), F32),
        compiler_params=_cparams(("arbitrary", "arbitrary")),
        name="adaln",
    )(c_all, w_ada, b_ada.reshape(depth, 1, n6))


def _tile_mod(i, npt, ts, p_ref, s_ref):
    ms = s_ref[...]
    tiled = jnp.concatenate([ms] * ts, axis=0)
    return jnp.where(i < npt, p_ref[...], tiled)


def _mod_specs(npt, tiles_per_seq, nseq_p, bs, width, sec, col_of):
    def p_map(*idx):
        i = idx[0]
        return (jnp.minimum(i // tiles_per_seq, nseq_p - 1), 0, col_of(sec, *idx))

    def s_map(*idx):
        return (0, col_of(sec, *idx))

    return [pl.BlockSpec((None, 1, width), p_map), pl.BlockSpec((bs, width), s_map)]


def _wcast_kernel(a_ref, b_ref, o_ref, *, shift):
    a = a_ref[...]
    if shift:
        a = jnp.concatenate([a[:, shift:], b_ref[:, :shift]], axis=1)
    o_ref[...] = a.astype(o_ref.dtype)


def _wcast(w, col0, ncols):
    depth, rows, _ = w.shape
    base = col0 // LANES * LANES
    shift = col0 - base
    tn = LANES * math.gcd(base // LANES, ncols // LANES) if base else _pick_tn(ncols)
    tn = min(tn, PROJ_TN)
    assert ncols % tn == 0 and base % tn == 0
    rt = min(WCAST_ROWS, rows)
    assert rows % rt == 0
    return pl.pallas_call(
        functools.partial(_wcast_kernel, shift=shift),
        grid=(depth, rows // rt, ncols // tn),
        in_specs=[
            pl.BlockSpec((None, rt, tn), lambda l, r, j: (l, r, base // tn + j)),
            pl.BlockSpec((None, rt, LANES), lambda l, r, j: (l, r, (base + (j + 1) * tn) // LANES)),
        ],
        out_specs=pl.BlockSpec((None, rt, tn), lambda l, r, j: (l, r, j)),
        out_shape=jax.ShapeDtypeStruct((depth, rows, ncols), BF16),
        compiler_params=_cparams(("arbitrary", "arbitrary", "arbitrary")),
        name="wcast",
    )(w, w)


def _modnorm_kernel(x_ref, g_ref, scp_ref, scs_ref, shp_ref, shs_ref, h_ref, *, npt, ts):
    i = pl.program_id(0)
    x = x_ref[...]
    xn = x * lax.rsqrt(jnp.mean(x * x, axis=-1, keepdims=True) + EPS) * g_ref[...]
    sc = _tile_mod(i, npt, ts, scp_ref, scs_ref)
    sh = _tile_mod(i, npt, ts, shp_ref, shs_ref)
    h_ref[...] = (xn * (1.0 + sc) + sh).astype(h_ref.dtype)


def _modnorm(x, gain, modp, mods, sec_sc, sec_sh, dims):
    m, d = x.shape
    tm, npt, tps, nb, bs, ts = dims
    col0 = lambda sec, i: sec
    return pl.pallas_call(
        functools.partial(_modnorm_kernel, npt=npt, ts=ts),
        grid=(m // tm,),
        in_specs=[
            pl.BlockSpec((tm, d), lambda i: (i, 0)),
            pl.BlockSpec((1, d), lambda i: (0, 0)),
            *_mod_specs(npt, tps, nb, bs, d, sec_sc, col0),
            *_mod_specs(npt, tps, nb, bs, d, sec_sh, col0),
        ],
        out_specs=pl.BlockSpec((tm, d), lambda i: (i, 0)),
        out_shape=jax.ShapeDtypeStruct((m, d), BF16),
        compiler_params=_cparams(("arbitrary",)),
        name="modnorm",
    )(x, gain, modp, mods, modp, mods)


def _proj_kernel(h_ref, *rest, nw, epilogue):
    w_refs = rest[:nw]
    o_ref = rest[nw]
    h = h_ref[...]
    acc = _dot(h, w_refs[0][...])
    if epilogue == "glu":
        acc = acc * jax.nn.sigmoid(_dot(h, w_refs[1][...]))
    elif epilogue == "sigmoid":
        acc = jax.nn.sigmoid(acc)
    o_ref[...] = acc.astype(o_ref.dtype)


def _proj(h, w, col0s, ncols, layer, out_dtype, epilogue):
    m, d = h.shape
    tn = _pick_tn(ncols)
    tm = _pick_rows(m)
    assert all(c0 % tn == 0 for c0 in col0s)
    w_specs = [pl.BlockSpec((None, d, tn), lambda j, i, off=c0 // tn: (layer, 0, off + j)) for c0 in col0s]
    return pl.pallas_call(
        functools.partial(_proj_kernel, nw=len(col0s), epilogue=epilogue),
        grid=(ncols // tn, m // tm),
        in_specs=[pl.BlockSpec((tm, d), lambda j, i: (i, 0))] + w_specs,
        out_specs=pl.BlockSpec((tm, tn), lambda j, i: (i, j)),
        out_shape=jax.ShapeDtypeStruct((m, ncols), out_dtype),
        compiler_params=_cparams(("arbitrary", "arbitrary")),
        name="proj_" + epilogue,
    )(h, *([w] * len(col0s)))


def _ssd_prompt_kernel(zx_ref, dt_ref, cw_ref, cb_ref, dtb_ref, alog_ref, dexp_ref, ng_ref, eh_ref,
                       y_ref, st_ref, xpad, y_scr, *, d, g_n, n, k_h, p):
    c = pl.program_id(1)
    L = zx_ref.shape[0]
    kp = k_h * p
    cc = cw_ref.shape[1]
    kw = cw_ref.shape[0]

    @pl.when(c == 0)
    def _():
        xpad[0:SUBLANES, :] = jnp.zeros((SUBLANES, cc), F32)
        st_ref[...] = jnp.zeros(st_ref.shape, F32)

    zx = zx_ref[...]
    z = zx[:, :d].astype(F32)
    xr = zx[:, d:].astype(F32)
    xpad[SUBLANES:SUBLANES + L, :] = xr
    conv = jnp.broadcast_to(cb_ref[...], (L, cc))
    for k in range(kw):
        off = SUBLANES - (kw - 1) + k
        conv = conv + cw_ref[k:k + 1, :] * xpad[off:off + L, :]
    xpad[0:SUBLANES, :] = xr[L - SUBLANES:, :]
    xbc = _silu(conv)
    xs = xbc[:, :d]
    bm = xbc[:, d:d + g_n * n]
    cm = xbc[:, d + g_n * n:]

    dt = _softplus(dt_ref[...] + dtb_ref[...])
    a = -jnp.exp(alog_ref[...])
    dta = dt * a
    row = lax.broadcasted_iota(jnp.int32, (L, L), 0)
    col = lax.broadcasted_iota(jnp.int32, (L, L), 1)
    causal = row >= col
    tril = jnp.where(causal, 1.0, 0.0).astype(BF16)
    a_hi, a_mid, a_lo = _split3(dta)
    acum = _dot(tril, a_hi) + _dot(tril, a_mid) + _dot(tril, a_lo)
    acum_t = acum.T
    dt_t = dt.T
    alast = acum[L - 1:L, :]
    wend = jnp.exp(alast - acum) * dt
    eac = jnp.exp(acum)
    eh = eh_ref[...]
    wend_x = _dot_parts(_split2(wend), eh)
    eac_x = _dot_parts(_split2(eac), eh)
    xw = xs * wend_x
    dec = jnp.exp(alast)
    lane = lax.broadcasted_iota(jnp.int32, (L, 2 * p), 1)

    for g in range(g_n):
        bg = bm[:, g * n:(g + 1) * n].astype(BF16)
        cg = cm[:, g * n:(g + 1) * n].astype(BF16)
        cbm = _dot_nt(cg, bg)
        s_g = st_ref[g * kp:(g + 1) * kp, :]
        yoff = _dot_nt(cg, s_g.astype(BF16))
        for j in range(k_h // 2):
            h0 = g * k_h + 2 * j
            ms = []
            for h in (h0, h0 + 1):
                seg = acum[:, h:h + 1] - acum_t[h:h + 1, :]
                dcy = jnp.exp(jnp.where(causal, seg, NEG_BIG))
                ms.append((cbm * dcy * dt_t[h:h + 1, :]).astype(BF16))
            lhs = jnp.concatenate(ms, axis=1)
            xp = xs[:, h0 * p:(h0 + 2) * p]
            rhs = jnp.concatenate([jnp.where(lane < p, xp, 0.0), jnp.where(lane >= p, xp, 0.0)],
                                  axis=0).astype(BF16)
            lo = h0 * p
            y_scr[:, lo:lo + 2 * p] = (_dot(lhs, rhs)
                                       + yoff[:, 2 * j * p:(2 * j + 2) * p] * eac_x[:, lo:lo + 2 * p])
        upd = _dot(xw[:, g * kp:(g + 1) * kp].T.astype(BF16), bg)
        for hh in range(k_h):
            h = g * k_h + hh
            r0 = h * p
            dsc = jnp.broadcast_to(dec[:, h:h + 1], (p, n))
            st_ref[r0:r0 + p, :] = dsc * s_g[hh * p:(hh + 1) * p, :] + upd[hh * p:(hh + 1) * p, :]

    y = (y_scr[...] + dexp_ref[...] * xs) * _silu(z)
    ng = ng_ref[...]
    for g in range(g_n):
        yg = y[:, g * kp:(g + 1) * kp]
        ms_ = jnp.mean(yg * yg, axis=-1, keepdims=True)
        y_ref[:, g * kp:(g + 1) * kp] = (yg * lax.rsqrt(ms_ + EPS) * ng[:, g * kp:(g + 1) * kp]).astype(y_ref.dtype)


def _ssd_prompt(zx, dtraw, cw, cb, dtb, alog, dexp, ng, eh, nb, t, hp, n, g_n, k_h, p):
    d = hp
    L = min(SSD_CHUNK, t)
    nc = t // L
    cc = cw.shape[1]
    kern = functools.partial(_ssd_prompt_kernel, d=d, g_n=g_n, n=n, k_h=k_h, p=p)
    const = lambda shape: pl.BlockSpec(shape, lambda b, c: (0,) * len(shape))
    return pl.pallas_call(
        kern,
        grid=(nb, nc),
        in_specs=[
            pl.BlockSpec((L, d + cc), lambda b, c: (b * nc + c, 0)),
            pl.BlockSpec((L, LANES), lambda b, c: (b * nc + c, 0)),
            const(cw.shape), const(cb.shape), const(dtb.shape), const(alog.shape),
            const(dexp.shape), const(ng.shape), const(eh.shape),
        ],
        out_specs=[
            pl.BlockSpec((L, d), lambda b, c: (b * nc + c, 0)),
            pl.BlockSpec((None, hp, n), lambda b, c: (b, 0, 0)),
        ],
        out_shape=[
            jax.ShapeDtypeStruct((nb * t, d), BF16),
            jax.ShapeDtypeStruct((nb, hp, n), F32),
        ],
        scratch_shapes=[pltpu.VMEM((L + SUBLANES, cc), F32), pltpu.VMEM((L, d), F32)],
        compiler_params=_cparams(("arbitrary", "arbitrary")),
        name="ssd_prompt",
    )(zx, dtraw, cw, cb, dtb, alog, dexp, ng, eh)


def _ssd_sample_pre_kernel(zx_ref, dt_ref, st_ref, cw_ref, cb_ref, dtb_ref, alog_ref, dexp_ref,
                           eh_ref, gh_ref, a_ref, ex_ref, zs_ref, xw_ref, c_ref, b_ref, dec_ref,
                           *, d, g_n, n):
    ts, bs, _ = zx_ref.shape
    kw = cw_ref.shape[0]
    gn = g_n * n
    rows = [st_ref[j] for j in range(kw - 1)]
    rows += [zx_ref[t][:, d:].astype(F32) for t in range(ts)]
    eh = eh_ref[...]
    gh = gh_ref[...]
    a = -jnp.exp(alog_ref[...])
    xs, bm, cm, dts, acs = [], [], [], [], []
    run = None
    for t in range(ts):
        conv = cb_ref[...] + cw_ref[0:1, :] * rows[t]
        for k in range(1, kw):
            conv = conv + cw_ref[k:k + 1, :] * rows[t + k]
        xbc = _silu(conv)
        xs.append(xbc[:, :d])
        bm.append(xbc[:, d:d + gn])
        cm.append(xbc[:, d + gn:])
        dtt = _softplus(dt_ref[t] + dtb_ref[...])
        dts.append(dtt)
        run = dtt * a if run is None else run + dtt * a
        acs.append(run)
    alast = acs[-1]
    dec_ref[...] = jnp.exp(alast)
    for t in range(ts):
        acc = dexp_ref[...] * xs[t]
        for j in range(t + 1):
            cbh = _dot_parts(_split2(cm[t] * bm[j]), gh)
            coef = cbh * dts[j]
            if j < t:
                coef = coef * jnp.exp(acs[t] - acs[j])
            acc = acc + _dot_parts(_split2(coef), eh) * xs[j]
        a_ref[t] = acc
        ex_ref[t] = _dot_parts(_split2(jnp.exp(acs[t])), eh)
        zs_ref[t] = _silu(zx_ref[t][:, :d].astype(F32))
        xw_ref[t] = xs[t] * _dot_parts(_split2(jnp.exp(alast - acs[t]) * dts[t]), eh)
        c_ref[t] = cm[t]
        b_ref[t] = bm[t]


def _ssd_sample_pre(zx3, dt3, tblk, ts, sconv_t, layer, cw, cb, dtb, alog, dexp, eh, gh, d, g_n, n):
    bs = zx3.shape[1]
    gn = g_n * n
    kern = functools.partial(_ssd_sample_pre_kernel, d=d, g_n=g_n, n=n)
    sb = min(SAMPLE_PRE_BLOCK, bs)
    kw = cw.shape[0]
    const = lambda a: pl.BlockSpec(a.shape, lambda i: (0,) * a.ndim)
    seq3 = lambda c: pl.BlockSpec((ts, sb, c), lambda i: (0, i, 0))
    src3 = lambda c: pl.BlockSpec((ts, sb, c), lambda i: (tblk, i, 0))
    big = jax.ShapeDtypeStruct((ts, bs, d), F32)
    small = jax.ShapeDtypeStruct((ts, bs, gn), F32)
    return pl.pallas_call(
        kern,
        grid=(bs // sb,),
        in_specs=[src3(zx3.shape[2]), src3(LANES),
                  pl.BlockSpec((None, kw - 1, sb, sconv_t.shape[3]), lambda i: (layer, 0, i, 0)),
                  const(cw), const(cb), const(dtb), const(alog), const(dexp), const(eh), const(gh)],
        out_specs=[seq3(d), seq3(d), seq3(d), seq3(d), seq3(gn), seq3(gn),
                   pl.BlockSpec((sb, LANES), lambda i: (i, 0))],
        out_shape=[big, big, big, big, small, small, jax.ShapeDtypeStruct((bs, LANES), F32)],
        compiler_params=_cparams(("arbitrary",)),
        name="ssd_sample_pre",
    )(zx3, dt3, sconv_t, cw, cb, dtb, alog, dexp, eh, gh)


def _ssd_sample_state_kernel(dec_ref, s0_ref, a_ref, ex_ref, zs_ref, xwt_ref, c_ref, b_ref, ng_ref,
                             prev_ref, y_ref, s1_ref, *, g_n, n, k_h, p, sb):
    del prev_ref
    i = pl.program_id(0)
    kp = k_h * p
    n_heads = g_n * k_h
    rows8 = sb * SUBLANES
    rid = lax.broadcasted_iota(jnp.int32, (rows8, 1), 0)
    ng = ng_ref[...]

    def body(s, carry):
        r0 = pl.multiple_of(s * SUBLANES, SUBLANES)
        in_s = (rid >= r0) & (rid < r0 + SUBLANES)
        for g in range(g_n):
            cg = c_ref[pl.ds(r0, SUBLANES), g * n:(g + 1) * n].astype(BF16)
            s_g = s0_ref[s, g * kp:(g + 1) * kp, :]
            yoff = _dot_nt(cg, s_g.astype(BF16))
            b_all = b_ref[:, g * n:(g + 1) * n]
            b_msk = jnp.where(in_s, b_all, 0.0).astype(BF16)
            upd = _dot(xwt_ref[g * kp:(g + 1) * kp, :].astype(BF16), b_msk)
            for hh in range(k_h):
                h = g * k_h + hh
                dsc = dec_ref[(i * sb + s) * n_heads + h]
                s1_ref[s, h * p:(h + 1) * p, :] = (dsc * s_g[hh * p:(hh + 1) * p, :]
                                                   + upd[hh * p:(hh + 1) * p, :])
            cols = slice(g * kp, (g + 1) * kp)
            yg = ((a_ref[pl.ds(r0, SUBLANES), cols] + ex_ref[pl.ds(r0, SUBLANES), cols] * yoff)
                  * zs_ref[pl.ds(r0, SUBLANES), cols])
            ms_ = jnp.mean(yg * yg, axis=-1, keepdims=True)
            y_ref[pl.ds(r0, SUBLANES), cols] = (yg * lax.rsqrt(ms_ + EPS) * ng[:, cols]).astype(y_ref.dtype)
        return carry

    lax.fori_loop(0, sb, body, 0)


def _ssd_sample_state(dec, s0_all, layer, s1_prev, a8, ex8, zs8, xwt, c8, b8, ng, g_n, n, k_h, p):
    depth, bs, hp, _ = s0_all.shape
    d = a8.shape[1]
    gn = g_n * n
    sb = min(SAMPLE_SEQ_BLOCK, bs)
    r8 = sb * SUBLANES
    kern = functools.partial(_ssd_sample_state_kernel, g_n=g_n, n=n, k_h=k_h, p=p, sb=sb)
    return pl.pallas_call(
        kern,
        grid=(bs // sb,),
        in_specs=[
            pl.BlockSpec(memory_space=pltpu.SMEM),
            pl.BlockSpec((None, sb, hp, n), lambda i: (layer, i, 0, 0)),
            pl.BlockSpec((r8, d), lambda i: (i, 0)),
            pl.BlockSpec((r8, d), lambda i: (i, 0)),
            pl.BlockSpec((r8, d), lambda i: (i, 0)),
            pl.BlockSpec((None, d, r8), lambda i: (i, 0, 0)),
            pl.BlockSpec((r8, gn), lambda i: (i, 0)),
            pl.BlockSpec((r8, gn), lambda i: (i, 0)),
            pl.BlockSpec((1, d), lambda i: (0, 0)),
            pl.BlockSpec(memory_space=pl.ANY),
        ],
        out_specs=[
            pl.BlockSpec((r8, d), lambda i: (i, 0)),
            pl.BlockSpec((None, sb, hp, n), lambda i: (layer, i, 0, 0)),
        ],
        out_shape=[
            jax.ShapeDtypeStruct((bs * SUBLANES, d), F32),
            jax.ShapeDtypeStruct((depth, bs, hp, n), F32),
        ],
        input_output_aliases={9: 1},
        compiler_params=_cparams(("arbitrary",)),
        name="ssd_sample_state",
    )(dec, s0_all, a8, ex8, zs8, xwt, c8, b8, ng, s1_prev)


def _ln_swish(v, g, b):
    mu = jnp.mean(v, axis=-1, keepdims=True)
    cen = v - mu
    var = jnp.mean(cen * cen, axis=-1, keepdims=True)
    return _silu(cen * lax.rsqrt(var + EPS) * g + b)


def _conf_prompt_kernel(u_ref, w_ref, b_ref, lg_ref, lb_ref, v_ref, upad, ush, *, halo):
    c = pl.program_id(1)
    L, d = u_ref.shape
    kw = w_ref.shape[0]
    span = L + halo - SUBLANES

    @pl.when(c == 0)
    def _():
        upad[0:halo, :] = jnp.zeros((halo, d), F32)

    u_b = u_ref[...]
    u = u_b.astype(F32)
    upad[halo:halo + L, :] = u
    row = lax.broadcasted_iota(jnp.int32, (L, L), 0)
    col = lax.broadcasted_iota(jnp.int32, (L, L), 1)
    for r in range(1, SUBLANES):
        ush[r - 1, 0:halo, :] = upad[r:r + halo, :]
        shifted = _dot(jnp.where(col == row + r, 1.0, 0.0).astype(BF16), u_b)
        ush[r - 1, halo:span, :] = shifted[0:span - halo, :]
    acc = jnp.broadcast_to(b_ref[...], (L, d))
    for k in range(kw):
        q, r = divmod(halo - (kw - 1) + k, SUBLANES)
        src = upad[q * SUBLANES:q * SUBLANES + L, :] if r == 0 else ush[r - 1, q * SUBLANES:q * SUBLANES + L, :]
        acc = acc + w_ref[k:k + 1, :] * src
    upad[0:halo, :] = u[L - halo:, :]
    v_ref[...] = _ln_swish(acc, lg_ref[...], lb_ref[...]).astype(v_ref.dtype)


def _conf_prompt(u, w, b, lg, lb, nb, t):
    assert u.dtype == BF16
    d = u.shape[1]
    kw = w.shape[0]
    L = min(CONF_CHUNK, t)
    nc = t // L
    halo = -(-(kw - 1) // SUBLANES) * SUBLANES
    const = lambda a: pl.BlockSpec(a.shape, lambda bb, c: (0,) * a.ndim)
    return pl.pallas_call(
        functools.partial(_conf_prompt_kernel, halo=halo),
        grid=(nb, nc),
        in_specs=[pl.BlockSpec((L, d), lambda bb, c: (bb * nc + c, 0)), const(w), const(b), const(lg), const(lb)],
        out_specs=pl.BlockSpec((L, d), lambda bb, c: (bb * nc + c, 0)),
        out_shape=jax.ShapeDtypeStruct((nb * t, d), BF16),
        scratch_shapes=[pltpu.VMEM((L + halo, d), F32), pltpu.VMEM((SUBLANES - 1, L + halo, d), F32)],
        compiler_params=_cparams(("arbitrary", "arbitrary")),
        name="conf_prompt",
    )(u, w, b, lg, lb)


def _conf_sample_kernel(u_ref, st_ref, w_ref, b_ref, lg_ref, lb_ref, v_ref):
    ts, sb, d = u_ref.shape
    kw = w_ref.shape[0]
    rows = [st_ref[j] for j in range(kw - 1)]
    rows += [u_ref[t].astype(F32) for t in range(ts)]
    for t in range(ts):
        acc = b_ref[...] + w_ref[0:1, :] * rows[t]
        for k in range(1, kw):
            acc = acc + w_ref[k:k + 1, :] * rows[t + k]
        v_ref[t] = _ln_swish(acc, lg_ref[...], lb_ref[...]).astype(v_ref.dtype)


def _conf_sample(u3, tblk, ts, cconv_t, layer, w, b, lg, lb):
    _, bs, d = u3.shape
    kw = w.shape[0]
    sb = min(CONF_SAMPLE_BLOCK, bs)
    const = lambda a: pl.BlockSpec(a.shape, lambda i: (0,) * a.ndim)
    return pl.pallas_call(
        _conf_sample_kernel,
        grid=(bs // sb,),
        in_specs=[
            pl.BlockSpec((ts, sb, d), lambda i: (tblk, i, 0)),
            pl.BlockSpec((None, kw - 1, sb, d), lambda i: (layer, 0, i, 0)),
            const(w), const(b), const(lg), const(lb),
        ],
        out_specs=pl.BlockSpec((ts, sb, d), lambda i: (0, i, 0)),
        out_shape=jax.ShapeDtypeStruct((ts, bs, d), BF16),
        compiler_params=_cparams(("arbitrary",)),
        name="conf_sample",
    )(u3, cconv_t, w, b, lg, lb)


def _merge_kernel(yp_ref, ys_ref, vp_ref, vs_ref, ws_ref, wc_ref, gs_ref, gc_ref, o_ref, *, npt):
    is_p = pl.program_id(0) < npt
    y = jnp.where(is_p, yp_ref[...], ys_ref[...])
    v = jnp.where(is_p, vp_ref[...], vs_ref[...])
    a = _dot(y, ws_ref[...])
    b = _dot(v, wc_ref[...])
    o_ref[...] = (gs_ref[...].astype(F32) * a + gc_ref[...].astype(F32) * b).astype(o_ref.dtype)


def _merge(y_p, y_s, v_p, v_s, ws, wc, layer, gates, tm):
    m, d = gates.shape[0], y_p.shape[1]
    npt = y_p.shape[0] // tm
    tn = _pick_tn(d)
    nj = d // tn
    p_spec = pl.BlockSpec((tm, d), lambda i, j: (jnp.minimum(i, npt - 1), 0))
    s_spec = pl.BlockSpec((tm, d), lambda i, j: (0, 0))
    return pl.pallas_call(
        functools.partial(_merge_kernel, npt=npt),
        grid=(m // tm, nj),
        in_specs=[
            p_spec, s_spec, p_spec, s_spec,
            pl.BlockSpec((None, d, tn), lambda i, j: (layer, 0, j)),
            pl.BlockSpec((None, d, tn), lambda i, j: (layer, 0, j)),
            pl.BlockSpec((tm, tn), lambda i, j: (i, j)),
            pl.BlockSpec((tm, tn), lambda i, j: (i, nj + j)),
        ],
        out_specs=pl.BlockSpec((tm, tn), lambda i, j: (i, j)),
        out_shape=jax.ShapeDtypeStruct((m, d), BF16),
        compiler_params=_cparams(("arbitrary", "arbitrary")),
        name="merge",
    )(y_p, y_s, v_p, v_s, ws, wc, gates, gates)


def _outproj_kernel(mx_ref, w_ref, x_ref, gp_ref, gs_ref, o_ref, *, npt, ts):
    gate = _tile_mod(pl.program_id(0), npt, ts, gp_ref, gs_ref)
    o_ref[...] = x_ref[...] + gate * _dot(mx_ref[...], w_ref[...])


def _outproj(mixed, wo, layer, x, modp, mods, sec, dims):
    m, d = x.shape
    tm, npt, tps, nb, bs, ts = dims
    tn = _pick_tn(d)
    nj = d // tn
    colj = lambda s, i, j: s * nj + j
    return pl.pallas_call(
        functools.partial(_outproj_kernel, npt=npt, ts=ts),
        grid=(m // tm, nj),
        in_specs=[
            pl.BlockSpec((tm, d), lambda i, j: (i, 0)),
            pl.BlockSpec((None, d, tn), lambda i, j: (layer, 0, j)),
            pl.BlockSpec((tm, tn), lambda i, j: (i, j)),
            *_mod_specs(npt, tps, nb, bs, tn, sec, colj),
        ],
        out_specs=pl.BlockSpec((tm, tn), lambda i, j: (i, j)),
        out_shape=jax.ShapeDtypeStruct((m, d), F32),
        compiler_params=_cparams(("arbitrary", "arbitrary")),
        name="outproj",
    )(mixed, wo, x, modp, mods)


def _pack_bf16_pairs(v):
    half = v.shape[1] // 2
    bits = pltpu.bitcast(v.astype(BF16).astype(F32), jnp.uint32)
    return bits[:, :half] | (bits[:, half:] >> 16)


def _unpack_bf16_pairs(w):
    left = pltpu.bitcast(w & jnp.uint32(0xFFFF0000), F32)
    right = pltpu.bitcast(w << 16, F32)
    return jnp.concatenate([left, right], axis=1)


INFO_RANK1, INFO_RANK2, INFO_E1, INFO_E2, INFO_W1, INFO_W2 = range(6)


def _router_kernel(x_ref, g_ref, scp_ref, scs_ref, shp_ref, shs_ref, whi_ref, wlo_ref, rb_ref,
                   hp_ref, info_ref, cnt_ref, *, npt, ts, n_e, n_g):
    i = pl.program_id(0)
    epg = n_e // n_g

    @pl.when(i == 0)
    def _():
        cnt_ref[...] = jnp.zeros(cnt_ref.shape, F32)

    x = x_ref[...]
    xn = x * lax.rsqrt(jnp.mean(x * x, axis=-1, keepdims=True) + EPS) * g_ref[...]
    sc = _tile_mod(i, npt, ts, scp_ref, scs_ref)
    sh = _tile_mod(i, npt, ts, shp_ref, shs_ref)
    h = xn * (1.0 + sc) + sh
    hi, lo = _split2(h)
    hp_ref[...] = _pack_bf16_pairs(h)
    logit = _dot(hi, whi_ref[...]) + _dot(lo, whi_ref[...]) + _dot(hi, wlo_ref[...]) + rb_ref[...]
    tm = logit.shape[0]
    lane = lax.broadcasted_iota(jnp.int32, (tm, LANES), 1).astype(F32)
    first = lambda hit: jnp.min(jnp.where(hit, lane, float(LANES)), axis=-1, keepdims=True)
    gl = jnp.where(lane >= n_e, jnp.where(lane < n_e + n_g, logit, NEG_BIG), NEG_BIG)
    gmax = jnp.max(gl, axis=-1, keepdims=True)
    g_top = 1.0 / jnp.sum(jnp.exp(gl - gmax), axis=-1, keepdims=True)
    g_idx = first(gl == gmax) - n_e
    el = jnp.where(lane >= g_idx * epg, jnp.where(lane < (g_idx + 1.0) * epg, logit, NEG_BIG), NEG_BIG)
    m1 = jnp.max(el, axis=-1, keepdims=True)
    i1 = first(el == m1)
    el2 = jnp.where(lane == i1, NEG_BIG, el)
    m2 = jnp.max(el2, axis=-1, keepdims=True)
    i2 = first(el2 == m2)
    r = jnp.exp(m2 - m1)
    w1 = 1.0 / (1.0 + r)
    w2 = r * w1
    oh1 = jnp.where(lane == i1, 1.0, 0.0)
    oh2 = jnp.where(lane == i2, 1.0, 0.0)
    both = oh1 + oh2
    row = lax.broadcasted_iota(jnp.int32, (tm, tm), 0)
    col = lax.broadcasted_iota(jnp.int32, (tm, tm), 1)
    before = _dot(jnp.where(row > col, 1.0, 0.0).astype(BF16), both.astype(BF16)) + cnt_ref[0:1, :]
    rank1 = jnp.sum(before * oh1, axis=-1, keepdims=True)
    rank2 = jnp.sum(before * oh2, axis=-1, keepdims=True)
    cnt_ref[...] = cnt_ref[...] + jnp.sum(both, axis=0, keepdims=True)
    rec = [rank1, rank2, i1, i2, g_top * w1, g_top * w2]
    info = jnp.zeros((tm, LANES), F32)
    for k, val in enumerate(rec):
        info = jnp.where(lane == float(k), val, info)
    info_ref[...] = info


def _router(x, gain, modp, mods, sec_sc, sec_sh, whi, wlo, rb, n_e, n_g, dims):
    m, d = x.shape
    tm, npt, tps, nb, bs, ts = dims
    col0 = lambda sec, i: sec
    return pl.pallas_call(
        functools.partial(_router_kernel, npt=npt, ts=ts, n_e=n_e, n_g=n_g),
        grid=(m // tm,),
        in_specs=[
            pl.BlockSpec((tm, d), lambda i: (i, 0)),
            pl.BlockSpec((1, d), lambda i: (0, 0)),
            *_mod_specs(npt, tps, nb, bs, d, sec_sc, col0),
            *_mod_specs(npt, tps, nb, bs, d, sec_sh, col0),
            pl.BlockSpec((d, LANES), lambda i: (0, 0)),
            pl.BlockSpec((d, LANES), lambda i: (0, 0)),
            pl.BlockSpec((1, LANES), lambda i: (0, 0)),
        ],
        out_specs=[pl.BlockSpec((tm, d // 2), lambda i: (i, 0)), pl.BlockSpec((tm, LANES), lambda i: (i, 0)),
                   pl.BlockSpec((SUBLANES, LANES), lambda i: (0, 0))],
        out_shape=[jax.ShapeDtypeStruct((m, d // 2), jnp.uint32), jax.ShapeDtypeStruct((m, LANES), F32),
                   jax.ShapeDtypeStruct((SUBLANES, LANES), F32)],
        compiler_params=_cparams(("arbitrary",)),
        name="router",
    )(x, gain, modp, mods, modp, mods, whi, wlo, rb)


def _slots_kernel(info_ref, offs_ref, pos_ref):
    info = info_ref[...]
    lane = lax.broadcasted_iota(jnp.int32, info.shape, 1).astype(F32)
    offs = offs_ref[...]
    base1 = jnp.sum(jnp.where(lane == info[:, INFO_E1:INFO_E1 + 1], offs, 0.0), axis=-1, keepdims=True)
    base2 = jnp.sum(jnp.where(lane == info[:, INFO_E2:INFO_E2 + 1], offs, 0.0), axis=-1, keepdims=True)
    rec = jnp.where(lane == 0.0, info[:, INFO_RANK1:INFO_RANK1 + 1] + base1,
                    jnp.where(lane == 1.0, info[:, INFO_RANK2:INFO_RANK2 + 1] + base2, 0.0))
    pos_ref[...] = rec.T[0:SUBLANES, :].astype(jnp.int32)


def _slots(info, offs_row, tm):
    m = info.shape[0]
    return pl.pallas_call(
        _slots_kernel,
        grid=(m // tm,),
        in_specs=[pl.BlockSpec((tm, LANES), lambda i: (i, 0)), pl.BlockSpec((1, LANES), lambda i: (0, 0))],
        out_specs=pl.BlockSpec((SUBLANES, tm), lambda i: (0, i)),
        out_shape=jax.ShapeDtypeStruct((SUBLANES, m), jnp.int32),
        compiler_params=_cparams(("arbitrary",)),
        name="moe_slots",
    )(info, offs_row)


def _moe_plan(info, cnt, n_e, tg, nt_max, tm):
    counts = cnt[0, :n_e].astype(jnp.int32)
    tiles = (counts + tg - 1) // tg
    tile_end = jnp.cumsum(tiles)
    offs = (tile_end - tiles) * tg
    n_used = tile_end[n_e - 1]
    offs_row = jnp.zeros((1, LANES), F32).at[0, :n_e].set(offs.astype(F32))
    pos = _slots(info, offs_row, tm)
    pos1, pos2 = pos[0], pos[1]
    tile = jnp.minimum(jnp.arange(nt_max, dtype=jnp.int32), n_used - 1)
    tile_expert = jnp.searchsorted(tile_end, tile, side="right").astype(jnp.int32)
    return pos1, pos2, tile_expert, n_used.reshape(1)


def _dispatch_kernel(pos1_ref, pos2_ref, hp_ref, xs0_ref, xs_ref, sem):
    del xs0_ref
    tm = hp_ref.shape[0]
    base = pl.program_id(0) * tm

    def issue(r, carry):
        src = hp_ref.at[pl.ds(r, 1)]
        pltpu.make_async_copy(src, xs_ref.at[pl.ds(pos1_ref[base + r], 1)], sem).start(priority=0)
        pltpu.make_async_copy(src, xs_ref.at[pl.ds(pos2_ref[base + r], 1)], sem).start(priority=1)
        return carry
    lax.fori_loop(0, tm, issue, 0, unroll=ROW_DMA_UNROLL)

    def drain(r, carry):
        for _ in range(2):
            pltpu.make_async_copy(hp_ref.at[pl.ds(0, 1)], xs_ref.at[pl.ds(0, 1)], sem).wait()
        return carry
    lax.fori_loop(0, tm, drain, 0, unroll=ROW_DMA_UNROLL)


def _dispatch(pos1, pos2, hp, n_slots, tm):
    m, half = hp.shape
    smem = pl.BlockSpec(memory_space=pltpu.SMEM)
    hbm = pl.BlockSpec(memory_space=pl.ANY)
    return pl.pallas_call(
        _dispatch_kernel,
        grid=(m // tm,),
        in_specs=[smem, smem, pl.BlockSpec((tm, half), lambda i: (i, 0)), hbm],
        out_specs=hbm,
        out_shape=jax.ShapeDtypeStruct((n_slots, half), jnp.uint32),
        scratch_shapes=[pltpu.SemaphoreType.DMA(())],
        input_output_aliases={3: 0},
        compiler_params=_cparams(("arbitrary",)),
        name="moe_dispatch",
    )(pos1, pos2, hp, jnp.zeros((n_slots, half), jnp.uint32))


def _experts_kernel(te_ref, nu_ref, xs_ref, wg_ref, wu_ref, wd_ref, ys_ref):
    del te_ref
    i = pl.program_id(0)

    @pl.when(i < nu_ref[0])
    def _():
        x = _unpack_bf16_pairs(xs_ref[...]).astype(BF16)
        hid = _silu(_dot(x, wg_ref[...].astype(BF16))) * _dot(x, wu_ref[...].astype(BF16))
        ys_ref[...] = _pack_bf16_pairs(_dot(hid.astype(BF16), wd_ref[...].astype(BF16)))

    @pl.when(i >= nu_ref[0])
    def _():
        ys_ref[...] = jnp.zeros(ys_ref.shape, jnp.uint32)


def _experts(tile_expert, n_used, xs, wg, wu, wd, layer, tg):
    n_slots, half = xs.shape
    _, _, d, f = wg.shape
    grid_spec = pltpu.PrefetchScalarGridSpec(
        num_scalar_prefetch=2,
        grid=(n_slots // tg,),
        in_specs=[
            pl.BlockSpec((tg, half), lambda i, te, nu: (i, 0)),
            pl.BlockSpec((None, None, d, f), lambda i, te, nu: (layer, te[i], 0, 0)),
            pl.BlockSpec((None, None, d, f), lambda i, te, nu: (layer, te[i], 0, 0)),
            pl.BlockSpec((None, None, f, d), lambda i, te, nu: (layer, te[i], 0, 0)),
        ],
        out_specs=pl.BlockSpec((tg, half), lambda i, te, nu: (i, 0)),
    )
    return pl.pallas_call(
        _experts_kernel,
        grid_spec=grid_spec,
        out_shape=jax.ShapeDtypeStruct((n_slots, half), jnp.uint32),
        compiler_params=_cparams(("arbitrary",)),
        name="moe_experts",
    )(tile_expert, n_used, xs, wg, wu, wd)


def _combine_kernel(pos1_ref, pos2_ref, info_ref, x_ref, gp_ref, gs_ref, ys_ref, o_ref, buf_a, buf_b, sem,
                    *, npt, ts):
    i = pl.program_id(0)
    tm = x_ref.shape[0]
    slot = lax.rem(i, 2)

    def issue(tile, s):
        def body(r, carry):
            t = tile * tm + r
            pltpu.make_async_copy(ys_ref.at[pl.ds(pos1_ref[t], 1)], buf_a.at[s, pl.ds(r, 1)], sem.at[s]).start(priority=0)
            pltpu.make_async_copy(ys_ref.at[pl.ds(pos2_ref[t], 1)], buf_b.at[s, pl.ds(r, 1)], sem.at[s]).start(priority=1)
            return carry
        lax.fori_loop(0, tm, body, 0, unroll=ROW_DMA_UNROLL)

    @pl.when(i == 0)
    def _():
        issue(0, 0)

    @pl.when(i + 1 < pl.num_programs(0))
    def _():
        issue(i + 1, 1 - slot)

    def wait_body(r, carry):
        pltpu.make_async_copy(ys_ref.at[pl.ds(0, 1)], buf_a.at[slot, pl.ds(0, 1)], sem.at[slot]).wait()
        pltpu.make_async_copy(ys_ref.at[pl.ds(0, 1)], buf_b.at[slot, pl.ds(0, 1)], sem.at[slot]).wait()
        return carry
    lax.fori_loop(0, tm, wait_body, 0, unroll=ROW_DMA_UNROLL)

    info = info_ref[...]
    w1 = info[:, INFO_W1:INFO_W1 + 1]
    w2 = info[:, INFO_W2:INFO_W2 + 1]
    moe = w1 * _unpack_bf16_pairs(buf_a[slot]) + w2 * _unpack_bf16_pairs(buf_b[slot])
    o_ref[...] = x_ref[...] + _tile_mod(i, npt, ts, gp_ref, gs_ref) * moe


def _combine(pos1, pos2, info, x, modp, mods, sec, ys, dims):
    m, d = x.shape
    tm, npt, tps, nb, bs, ts = dims
    col0 = lambda s, i: s
    smem = pl.BlockSpec(memory_space=pltpu.SMEM)
    return pl.pallas_call(
        functools.partial(_combine_kernel, npt=npt, ts=ts),
        grid=(m // tm,),
        in_specs=[
            smem, smem,
            pl.BlockSpec((tm, LANES), lambda i: (i, 0)),
            pl.BlockSpec((tm, d), lambda i: (i, 0)),
            *_mod_specs(npt, tps, nb, bs, d, sec, col0),
            pl.BlockSpec(memory_space=pl.ANY),
        ],
        out_specs=pl.BlockSpec((tm, d), lambda i: (i, 0)),
        out_shape=jax.ShapeDtypeStruct((m, d), F32),
        scratch_shapes=[pltpu.VMEM((2, tm, d // 2), jnp.uint32), pltpu.VMEM((2, tm, d // 2), jnp.uint32),
                        pltpu.SemaphoreType.DMA((2,))],
        compiler_params=_cparams(("arbitrary",)),
        name="moe_combine",
    )(pos1, pos2, info, x, modp, mods, ys)


def _moe(hp, info, cnt, wg, wu, wd, layer, x, modp, mods, sec, dims):
    m = x.shape[0]
    n_e = wg.shape[1]
    tg = MOE_TG
    nt_max = -(-2 * m // tg) + n_e
    pos1, pos2, tile_expert, n_used = _moe_plan(info, cnt, n_e, tg, nt_max, dims[0])
    xs = _dispatch(pos1, pos2, hp, nt_max * tg, dims[0])
    ys = _experts(tile_expert, n_used, xs, wg, wu, wd, layer, tg)
    return _combine(pos1, pos2, info, x, modp, mods, sec, ys, dims)


def _final_norm_kernel(x_ref, g_ref, o_ref):
    x = x_ref[...]
    o_ref[...] = x * lax.rsqrt(jnp.mean(x * x, axis=-1, keepdims=True) + EPS) * g_ref[...]


def _final_norm(x, g, tm, tile0, ntiles):
    d = x.shape[1]
    return pl.pallas_call(
        _final_norm_kernel,
        grid=(ntiles,),
        in_specs=[pl.BlockSpec((tm, d), lambda i: (tile0 + i, 0)), pl.BlockSpec((1, d), lambda i: (0, 0))],
        out_specs=pl.BlockSpec((tm, d), lambda i: (i, 0)),
        out_shape=jax.ShapeDtypeStruct((ntiles * tm, d), F32),
        compiler_params=_cparams(("arbitrary",)),
        name="final_norm",
    )(x, g)


def _to_seq8(a):
    ts, bs, c = a.shape
    a = jnp.transpose(a, (1, 0, 2))
    a = jnp.pad(a, ((0, 0), (0, SUBLANES - ts), (0, 0)))
    return a.reshape(bs * SUBLANES, c)


def kernel(x_prompt, x_sample, c_prompt, c_sample, state_ssm, state_ssd_conv, state_conf_conv, norm_mix_g, norm_ffn_g, w_ada, b_ada, w_in, ssd_conv_w, ssd_conv_b, ssd_dt_bias, ssd_a_log, ssd_d, ssd_norm_g, w_ssd_out, conf_conv_w, conf_conv_b, conf_ln_g, conf_ln_b, w_conf_out, w_o, w_group_router, b_group_router, w_expert_router, b_expert_router, w_exp_gate, w_exp_up, w_exp_down, final_norm_g):
    nb, t, d = x_prompt.shape
    bs, ts, _ = x_sample.shape
    depth = w_in.shape[0]
    n_h = ssd_dt_bias.shape[1]
    p = d // n_h
    n = state_ssm.shape[-1]
    cc = ssd_conv_w.shape[-1]
    g_n = (cc - d) // (2 * n)
    k_h = n_h // g_n
    gn = g_n * n
    kc = conf_conv_w.shape[1]
    n_e = w_exp_gate.shape[1]
    n_g = w_group_router.shape[-1]
    tm = bs * ts
    assert t % tm == 0 and tm % SUBLANES == 0 and bs % SUBLANES == 0 and ts <= SUBLANES
    assert 2 * p == LANES and k_h % 2 == 0 and n_h <= LANES and n_e + n_g <= LANES
    npt = nb * t // tm
    dims = (tm, npt, t // tm, nb, bs, ts)
    mp = nb * t

    x = jnp.concatenate([x_prompt.reshape(mp, d), jnp.transpose(x_sample, (1, 0, 2)).reshape(tm, d)], axis=0)

    rc = -(-(nb + bs) // SUBLANES) * SUBLANES
    c_all = jnp.concatenate([c_prompt, c_sample, jnp.zeros((rc - nb - bs, d), F32)], axis=0)
    mod = _adaln(c_all, w_ada, b_ada)

    head_of = jnp.arange(d) // p
    eh = (jnp.arange(LANES)[:, None] == head_of[None, :]).astype(BF16)
    gh = ((jnp.arange(gn)[:, None] // n) == (jnp.arange(LANES)[None, :] // k_h)).astype(BF16)
    gh = gh * (jnp.arange(LANES)[None, :] < n_h).astype(BF16)
    pad_h = lambda v: jnp.pad(v, (0, LANES - n_h)).reshape(1, LANES)

    m = mp + tm
    tblk = mp // bs // ts
    assert tblk * bs * ts == mp
    sconv_t = jnp.transpose(state_ssd_conv, (0, 2, 1, 3))
    cconv_t = jnp.transpose(state_conf_conv, (0, 2, 1, 3))
    ssm0_all = state_ssm.reshape(depth, bs, n_h * p, n)
    ssm_s = jnp.zeros(ssm0_all.shape, F32)
    o_dt, o_glu = d + cc, d + cc + n_h
    w_main = _wcast(w_in, 0, o_dt)
    w_dt = jnp.pad(w_in[:, :, o_dt:o_glu], ((0, 0), (0, 0), (0, LANES - n_h))).astype(BF16)
    w_rest = _wcast(w_in, o_glu, 4 * d)
    w_so, w_co, w_oo = w_ssd_out.astype(BF16), w_conf_out.astype(BF16), w_o.astype(BF16)
    ssm_p, sconv_p, cconv_p, sconv_s, cconv_s = [], [], [], [], []
    for l in range(depth):
        modp = mod[l, :nb].reshape(nb, 1, 6 * d)
        mods = mod[l, nb:nb + bs]
        h1 = _modnorm(x, norm_mix_g[l].reshape(1, d), modp, mods, 1, 0, dims)
        zx = _proj(h1, w_main, [0], o_dt, l, BF16, "plain")
        dtraw = _proj(h1, w_dt, [0], LANES, l, F32, "plain")
        u = _proj(h1, w_rest, [0, d], d, l, BF16, "glu")
        gates = _proj(h1, w_rest, [2 * d], 2 * d, l, BF16, "sigmoid")

        cw = ssd_conv_w[l]
        cb = ssd_conv_b[l].reshape(1, cc)
        dtb = pad_h(ssd_dt_bias[l])
        alog = pad_h(ssd_a_log[l])
        dexp = jnp.repeat(ssd_d[l], p).reshape(1, d)
        ng = ssd_norm_g[l].reshape(1, d)

        y_p, s_p = _ssd_prompt(zx, dtraw, cw, cb, dtb, alog, dexp, ng, eh, nb, t, n_h * p, n, g_n, k_h, p)
        zx_s = zx[mp:].reshape(ts, bs, d + cc)
        sconv0 = state_ssd_conv[l]
        a_t, ex_t, zs_t, xw_t, c_t, b_t, dec = _ssd_sample_pre(
            zx.reshape(m // bs, bs, d + cc), dtraw.reshape(m // bs, bs, LANES), tblk, ts, sconv_t, l,
            cw, cb, dtb, alog, dexp, eh, gh, d, g_n, n)
        sb = min(SAMPLE_SEQ_BLOCK, bs)
        xwt = _to_seq8(xw_t).reshape(bs // sb, sb * SUBLANES, d).transpose(0, 2, 1)
        dec = dec[:, :n_h].reshape(bs * n_h)
        y_s8, ssm_s = _ssd_sample_state(dec, ssm0_all, l, ssm_s, _to_seq8(a_t), _to_seq8(ex_t),
                                        _to_seq8(zs_t), xwt, _to_seq8(c_t), _to_seq8(b_t), ng, g_n, n, k_h, p)
        y_s = jnp.transpose(y_s8.reshape(bs, SUBLANES, d)[:, :ts], (1, 0, 2)).reshape(tm, d).astype(BF16)

        ccw = conf_conv_w[l]
        ccb = conf_conv_b[l].reshape(1, d)
        lg = conf_ln_g[l].reshape(1, d)
        lb = conf_ln_b[l].reshape(1, d)
        v_p = _conf_prompt(u, ccw, ccb, lg, lb, nb, t)
        u_s = u[mp:].reshape(ts, bs, d)
        cconv0 = state_conf_conv[l]
        v_s = _conf_sample(u.reshape(m // bs, bs, d), tblk, ts, cconv_t, l, ccw, ccb, lg, lb).reshape(tm, d)

        mixed = _merge(y_p, y_s, v_p, v_s, w_so, w_co, l, gates, tm)
        x = _outproj(mixed, w_oo, l, x, modp, mods, 2, dims)

        w_r = jnp.concatenate([w_expert_router[l], w_group_router[l],
                               jnp.zeros((d, LANES - n_e - n_g), F32)], axis=1)
        w_r_hi = w_r.astype(BF16)
        w_r_lo = (w_r - w_r_hi.astype(F32)).astype(BF16)
        rb = jnp.concatenate([b_expert_router[l], b_group_router[l], jnp.zeros((LANES - n_e - n_g,), F32)]).reshape(1, LANES)
        hp, info, cnt = _router(x, norm_ffn_g[l].reshape(1, d), modp, mods, 4, 3, w_r_hi, w_r_lo, rb, n_e, n_g, dims)
        x = _moe(hp, info, cnt, w_exp_gate, w_exp_up, w_exp_down, l, x, modp, mods, 5, dims)

        ssm_p.append(s_p.reshape(nb, n_h, p, n))
        k3 = sconv0.shape[1]
        sconv_p.append(jnp.stack([zx[(b + 1) * t - k3:(b + 1) * t, d:] for b in range(nb)]).astype(F32))
        cconv_p.append(jnp.stack([u[(b + 1) * t - (kc - 1):(b + 1) * t] for b in range(nb)]).astype(F32))
        xraw_s = jnp.transpose(zx_s[:, :, d:], (1, 0, 2)).astype(F32)
        sconv_s.append(jnp.concatenate([sconv0, xraw_s], axis=1)[:, ts:])
        cconv_s.append(jnp.concatenate([cconv0, jnp.transpose(u_s, (1, 0, 2)).astype(F32)], axis=1)[:, ts:])

    fg = final_norm_g.reshape(1, d)
    y_prompt = _final_norm(x, fg, tm, 0, npt).reshape(nb, t, d)
    y_sample = jnp.transpose(_final_norm(x, fg, tm, npt, 1).reshape(ts, bs, d), (1, 0, 2))
    return (y_prompt, y_sample, jnp.stack(ssm_p), jnp.stack(sconv_p), jnp.stack(cconv_p),
            ssm_s.reshape(depth, bs, n_h, p, n), jnp.stack(sconv_s), jnp.stack(cconv_s))
```

```python
import functools
import math

import jax
import jax.numpy as jnp
from jax import lax
from jax.experimental import pallas as pl
from jax.experimental.pallas import tpu as pltpu

F32 = jnp.float32
BF16 = jnp.bfloat16
EPS = 1e-6
LANES = 128
SUBLANES = 8
VMEM_LIMIT = 52 * 1024 * 1024
NEG_BIG = -1e30
SSD_CHUNK = 128
CONF_CHUNK = 256
SAMPLE_SEQ_BLOCK = 8
SAMPLE_PRE_BLOCK = 32
CONF_SAMPLE_BLOCK = 16
MOE_TG = 256
ROW_DMA_UNROLL = 8
ADA_TN = 1024
PROJ_TN = 1024
PROJ_TM = 1088


def _pick_tn(ncols):
    assert ncols % LANES == 0
    return max(tn for tn in range(LANES, min(PROJ_TN, ncols) + 1, LANES) if ncols % tn == 0)


def _pick_rows(m):
    return max(tm for tm in range(2 * SUBLANES, min(PROJ_TM, m) + 1, 2 * SUBLANES) if m % tm == 0)


def _cparams(sem):
    return pltpu.CompilerParams(dimension_semantics=sem, vmem_limit_bytes=VMEM_LIMIT)


def _silu(x):
    return x * jax.nn.sigmoid(x)


def _softplus(x):
    return jnp.maximum(x, 0.0) + jnp.log1p(jnp.exp(-jnp.abs(x)))


def _split2(x):
    hi = x.astype(BF16)
    lo = (x - hi.astype(F32)).astype(BF16)
    return hi, lo


def _split3(x):
    hi = x.astype(BF16)
    r = x - hi.astype(F32)
    mid = r.astype(BF16)
    lo = (r - mid.astype(F32)).astype(BF16)
    return hi, mid, lo


def _dot(a, b):
    return jnp.dot(a, b, preferred_element_type=F32)


def _dot_nt(a, b):
    return lax.dot_general(a, b, (((1,), (1,)), ((), ())), preferred_element_type=F32)


def _dot_parts(parts, w):
    acc = _dot(parts[0], w)
    for p in parts[1:]:
        acc = acc + _dot(p, w)
    return acc


def _adaln_kernel(c_ref, w_ref, b_ref, o_ref):
    s = _silu(c_ref[...]).astype(BF16)
    o_ref[...] = _dot(s, w_ref[...].astype(BF16)) + b_ref[...]


def _adaln(c_all, w_ada, b_ada):
    depth, d, n6 = w_ada.shape
    rc = c_all.shape[0]
    tn = min(ADA_TN, n6)
    return pl.pallas_call(
        _adaln_kernel,
        grid=(depth, n6 // tn),
        in_specs=[
            pl.BlockSpec((rc, d), lambda l, j: (0, 0)),
            pl.BlockSpec((None, d, tn), lambda l, j: (l, 0, j)),
            pl.BlockSpec((None, 1, tn), lambda l, j: (l, 0, j)),
        ],
        out_specs=pl.BlockSpec((None, rc, tn), lambda l, j: (l, 0, j)),
        out_shape=jax.ShapeDtypeStruct((depth, rc, n6), F32),
        compiler_params=_cparams(("arbitrary", "arbitrary")),
        name="adaln",
    )(c_all, w_ada, b_ada.reshape(depth, 1, n6))


def _tile_mod(i, npt, ts, p_ref, s_ref):
    ms = s_ref[...]
    tiled = jnp.concatenate([ms] * ts, axis=0)
    return jnp.where(i < npt, p_ref[...], tiled)


def _mod_specs(npt, tiles_per_seq, nseq_p, bs, width, sec, col_of):
    def p_map(*idx):
        i = idx[0]
        return (jnp.minimum(i // tiles_per_seq, nseq_p - 1), 0, col_of(sec, *idx))

    def s_map(*idx):
        return (0, col_of(sec, *idx))

    return [pl.BlockSpec((None, 1, width), p_map), pl.BlockSpec((bs, width), s_map)]


def _modnorm_kernel(x_ref, g_ref, scp_ref, scs_ref, shp_ref, shs_ref, h_ref, *, npt, ts):
    i = pl.program_id(0)
    x = x_ref[...]
    xn = x * lax.rsqrt(jnp.mean(x * x, axis=-1, keepdims=True) + EPS) * g_ref[...]
    sc = _tile_mod(i, npt, ts, scp_ref, scs_ref)
    sh = _tile_mod(i, npt, ts, shp_ref, shs_ref)
    h_ref[...] = (xn * (1.0 + sc) + sh).astype(h_ref.dtype)


def _modnorm(x, gain, modp, mods, sec_sc, sec_sh, dims):
    m, d = x.shape
    tm, npt, tps, nb, bs, ts = dims
    col0 = lambda sec, i: sec
    return pl.pallas_call(
        functools.partial(_modnorm_kernel, npt=npt, ts=ts),
        grid=(m // tm,),
        in_specs=[
            pl.BlockSpec((tm, d), lambda i: (i, 0)),
            pl.BlockSpec((1, d), lambda i: (0, 0)),
            *_mod_specs(npt, tps, nb, bs, d, sec_sc, col0),
            *_mod_specs(npt, tps, nb, bs, d, sec_sh, col0),
        ],
        out_specs=pl.BlockSpec((tm, d), lambda i: (i, 0)),
        out_shape=jax.ShapeDtypeStruct((m, d), BF16),
        compiler_params=_cparams(("arbitrary",)),
        name="modnorm",
    )(x, gain, modp, mods, modp, mods)


def _proj_kernel(h_ref, *rest, shifts, epilogue):
    h = h_ref[...]
    accs, k = [], 0
    for s in shifts:
        acc = _dot(h, rest[k][...])
        k += 1
        if s:
            acc = jnp.concatenate([acc[:, s:], _dot(h, rest[k][...])[:, :s]], axis=1)
            k += 1
        accs.append(acc)
    o_ref = rest[k]
    acc = accs[0]
    if epilogue == "glu":
        acc = acc * jax.nn.sigmoid(accs[1])
    elif epilogue == "sigmoid":
        acc = jax.nn.sigmoid(acc)
    o_ref[...] = acc.astype(o_ref.dtype)


def _proj(h, w, col0s, ncols, layer, out_dtype, epilogue):
    m, d = h.shape
    bases = [c0 // LANES * LANES for c0 in col0s]
    shifts = tuple(c0 - b for c0, b in zip(col0s, bases))
    tn = _pick_tn(LANES * functools.reduce(math.gcd, [b // LANES for b in bases] + [ncols // LANES]))
    tm = _pick_rows(m)
    w_specs = []
    for b, s in zip(bases, shifts):
        w_specs.append(pl.BlockSpec((None, d, tn), lambda j, i, off=b // tn: (layer, 0, off + j)))
        if s:
            w_specs.append(pl.BlockSpec((None, d, LANES),
                                        lambda j, i, b=b: (layer, 0, (b + (j + 1) * tn) // LANES)))
    return pl.pallas_call(
        functools.partial(_proj_kernel, shifts=shifts, epilogue=epilogue),
        grid=(ncols // tn, m // tm),
        in_specs=[pl.BlockSpec((tm, d), lambda j, i: (i, 0))] + w_specs,
        out_specs=pl.BlockSpec((tm, tn), lambda j, i: (i, j)),
        out_shape=jax.ShapeDtypeStruct((m, ncols), out_dtype),
        compiler_params=_cparams(("arbitrary", "arbitrary")),
        name="proj_" + epilogue,
    )(h, *([w] * len(w_specs)))


def _ssd_prompt_kernel(zx_ref, dt_ref, cw_ref, cb_ref, dtb_ref, alog_ref, dexp_ref, ng_ref, eh_ref,
                       y_ref, st_ref, xpad, y_scr, *, d, g_n, n, k_h, p):
    c = pl.program_id(1)
    L = zx_ref.shape[0]
    kp = k_h * p
    cc = cw_ref.shape[1]
    kw = cw_ref.shape[0]

    @pl.when(c == 0)
    def _():
        xpad[0:SUBLANES, :] = jnp.zeros((SUBLANES, cc), F32)
        st_ref[...] = jnp.zeros(st_ref.shape, F32)

    zx = zx_ref[...]
    z = zx[:, :d].astype(F32)
    xr_b = zx[:, d:]
    xr = xr_b.astype(F32)
    xpad[SUBLANES:2 * SUBLANES, :] = xr[0:SUBLANES, :]
    row = lax.broadcasted_iota(jnp.int32, (L, L), 0)
    col = lax.broadcasted_iota(jnp.int32, (L, L), 1)
    conv = cb_ref[...] + cw_ref[kw - 1:kw, :] * xr
    head = cb_ref[...] + cw_ref[kw - 1:kw, :] * xr[0:SUBLANES, :]
    for s in range(1, kw):
        back = _dot(jnp.where(col == row - s, 1.0, 0.0).astype(BF16), xr_b)
        conv = conv + cw_ref[kw - 1 - s:kw - s, :] * back
        head = head + cw_ref[kw - 1 - s:kw - s, :] * xpad[SUBLANES - s:2 * SUBLANES - s, :]
    conv = jnp.concatenate([head, conv[SUBLANES:, :]], axis=0)
    xpad[0:SUBLANES, :] = xr[L - SUBLANES:, :]
    xbc = _silu(conv)
    xs = xbc[:, :d]
    bm = xbc[:, d:d + g_n * n]
    cm = xbc[:, d + g_n * n:]

    dt = _softplus(dt_ref[...] + dtb_ref[...])
    a = -jnp.exp(alog_ref[...])
    dta = dt * a
    row = lax.broadcasted_iota(jnp.int32, (L, L), 0)
    col = lax.broadcasted_iota(jnp.int32, (L, L), 1)
    causal = row >= col
    tril = jnp.where(causal, 1.0, 0.0).astype(BF16)
    a_hi, a_mid, a_lo = _split3(dta)
    acum = _dot(tril, a_hi) + _dot(tril, a_mid) + _dot(tril, a_lo)
    acum_t = acum.T
    dt_t = dt.T
    alast = acum[L - 1:L, :]
    wend = jnp.exp(alast - acum) * dt
    eac = jnp.exp(acum)
    eh = eh_ref[...]
    wend_x = _dot_parts(_split2(wend), eh)
    eac_x = _dot_parts(_split2(eac), eh)
    xw = xs * wend_x
    dec = jnp.exp(alast)
    lane = lax.broadcasted_iota(jnp.int32, (L, 2 * p), 1)

    for g in range(g_n):
        bg = bm[:, g * n:(g + 1) * n].astype(BF16)
        cg = cm[:, g * n:(g + 1) * n].astype(BF16)
        cbm = _dot_nt(cg, bg)
        s_g = st_ref[g * kp:(g + 1) * kp, :]
        yoff = _dot_nt(cg, s_g.astype(BF16))
        for j in range(k_h // 2):
            h0 = g * k_h + 2 * j
            ms = []
            for h in (h0, h0 + 1):
                seg = acum[:, h:h + 1] - acum_t[h:h + 1, :]
                dcy = jnp.exp(jnp.where(causal, seg, NEG_BIG))
                ms.append((cbm * dcy * dt_t[h:h + 1, :]).astype(BF16))
            lhs = jnp.concatenate(ms, axis=1)
            xp = xs[:, h0 * p:(h0 + 2) * p]
            rhs = jnp.concatenate([jnp.where(lane < p, xp, 0.0), jnp.where(lane >= p, xp, 0.0)],
                                  axis=0).astype(BF16)
            lo = h0 * p
            y_scr[:, lo:lo + 2 * p] = (_dot(lhs, rhs)
                                       + yoff[:, 2 * j * p:(2 * j + 2) * p] * eac_x[:, lo:lo + 2 * p])
        upd = _dot(xw[:, g * kp:(g + 1) * kp].T.astype(BF16), bg)
        for hh in range(k_h):
            h = g * k_h + hh
            r0 = h * p
            dsc = jnp.broadcast_to(dec[:, h:h + 1], (p, n))
            st_ref[r0:r0 + p, :] = dsc * s_g[hh * p:(hh + 1) * p, :] + upd[hh * p:(hh + 1) * p, :]

    y = (y_scr[...] + dexp_ref[...] * xs) * _silu(z)
    ng = ng_ref[...]
    for g in range(g_n):
        yg = y[:, g * kp:(g + 1) * kp]
        ms_ = jnp.mean(yg * yg, axis=-1, keepdims=True)
        y_ref[:, g * kp:(g + 1) * kp] = (yg * lax.rsqrt(ms_ + EPS) * ng[:, g * kp:(g + 1) * kp]).astype(y_ref.dtype)


def _ssd_prompt(zx, dtraw, cw, cb, dtb, alog, dexp, ng, eh, nb, t, hp, n, g_n, k_h, p):
    d = hp
    L = min(SSD_CHUNK, t)
    nc = t // L
    cc = cw.shape[1]
    kern = functools.partial(_ssd_prompt_kernel, d=d, g_n=g_n, n=n, k_h=k_h, p=p)
    const = lambda shape: pl.BlockSpec(shape, lambda b, c: (0,) * len(shape))
    return pl.pallas_call(
        kern,
        grid=(nb, nc),
        in_specs=[
            pl.BlockSpec((L, d + cc), lambda b, c: (b * nc + c, 0)),
            pl.BlockSpec((L, LANES), lambda b, c: (b * nc + c, 0)),
            const(cw.shape), const(cb.shape), const(dtb.shape), const(alog.shape),
            const(dexp.shape), const(ng.shape), const(eh.shape),
        ],
        out_specs=[
            pl.BlockSpec((L, d), lambda b, c: (b * nc + c, 0)),
            pl.BlockSpec((None, hp, n), lambda b, c: (b, 0, 0)),
        ],
        out_shape=[
            jax.ShapeDtypeStruct((nb * t, d), BF16),
            jax.ShapeDtypeStruct((nb, hp, n), F32),
        ],
        scratch_shapes=[pltpu.VMEM((2 * SUBLANES, cc), F32), pltpu.VMEM((L, d), F32)],
        compiler_params=_cparams(("arbitrary", "arbitrary")),
        name="ssd_prompt",
    )(zx, dtraw, cw, cb, dtb, alog, dexp, ng, eh)


def _ssd_sample_pre_kernel(zx_ref, dt_ref, st_ref, cw_ref, cb_ref, dtb_ref, alog_ref, dexp_ref,
                           eh_ref, gh_ref, a_ref, ex_ref, zs_ref, xw_ref, c_ref, b_ref, dec_ref,
                           *, d, g_n, n):
    ts, bs, _ = zx_ref.shape
    kw = cw_ref.shape[0]
    gn = g_n * n
    rows = [st_ref[j] for j in range(kw - 1)]
    rows += [zx_ref[t][:, d:].astype(F32) for t in range(ts)]
    eh = eh_ref[...]
    gh = gh_ref[...]
    a = -jnp.exp(alog_ref[...])
    xs, bm, cm, dts, acs = [], [], [], [], []
    run = None
    for t in range(ts):
        conv = cb_ref[...] + cw_ref[0:1, :] * rows[t]
        for k in range(1, kw):
            conv = conv + cw_ref[k:k + 1, :] * rows[t + k]
        xbc = _silu(conv)
        xs.append(xbc[:, :d])
        bm.append(xbc[:, d:d + gn])
        cm.append(xbc[:, d + gn:])
        dtt = _softplus(dt_ref[t] + dtb_ref[...])
        dts.append(dtt)
        run = dtt * a if run is None else run + dtt * a
        acs.append(run)
    alast = acs[-1]
    dec_ref[...] = jnp.exp(alast)
    for t in range(ts):
        acc = dexp_ref[...] * xs[t]
        for j in range(t + 1):
            cbh = _dot_parts(_split2(cm[t] * bm[j]), gh)
            coef = cbh * dts[j]
            if j < t:
                coef = coef * jnp.exp(acs[t] - acs[j])
            acc = acc + _dot_parts(_split2(coef), eh) * xs[j]
        a_ref[t] = acc
        ex_ref[t] = _dot_parts(_split2(jnp.exp(acs[t])), eh)
        zs_ref[t] = _silu(zx_ref[t][:, :d].astype(F32))
        xw_ref[t] = xs[t] * _dot_parts(_split2(jnp.exp(alast - acs[t]) * dts[t]), eh)
        c_ref[t] = cm[t]
        b_ref[t] = bm[t]


def _ssd_sample_pre(zx3, dt3, tblk, ts, sconv_t, layer, cw, cb, dtb, alog, dexp, eh, gh, d, g_n, n):
    bs = zx3.shape[1]
    gn = g_n * n
    kern = functools.partial(_ssd_sample_pre_kernel, d=d, g_n=g_n, n=n)
    sb = min(SAMPLE_PRE_BLOCK, bs)
    kw = cw.shape[0]
    const = lambda a: pl.BlockSpec(a.shape, lambda i: (0,) * a.ndim)
    seq3 = lambda c: pl.BlockSpec((ts, sb, c), lambda i: (0, i, 0))
    src3 = lambda c: pl.BlockSpec((ts, sb, c), lambda i: (tblk, i, 0))
    big = jax.ShapeDtypeStruct((ts, bs, d), F32)
    small = jax.ShapeDtypeStruct((ts, bs, gn), F32)
    return pl.pallas_call(
        kern,
        grid=(bs // sb,),
        in_specs=[src3(zx3.shape[2]), src3(LANES),
                  pl.BlockSpec((None, kw - 1, sb, sconv_t.shape[3]), lambda i: (layer, 0, i, 0)),
                  const(cw), const(cb), const(dtb), const(alog), const(dexp), const(eh), const(gh)],
        out_specs=[seq3(d), seq3(d), seq3(d), seq3(d), seq3(gn), seq3(gn),
                   pl.BlockSpec((sb, LANES), lambda i: (i, 0))],
        out_shape=[big, big, big, big, small, small, jax.ShapeDtypeStruct((bs, LANES), F32)],
        compiler_params=_cparams(("arbitrary",)),
        name="ssd_sample_pre",
    )(zx3, dt3, sconv_t, cw, cb, dtb, alog, dexp, eh, gh)


def _ssd_sample_state_kernel(dec_ref, s0_ref, a_ref, ex_ref, zs_ref, xwt_ref, c_ref, b_ref, ng_ref,
                             prev_ref, y_ref, s1_ref, *, g_n, n, k_h, p, sb):
    del prev_ref
    i = pl.program_id(0)
    kp = k_h * p
    n_heads = g_n * k_h
    rows8 = sb * SUBLANES
    rid = lax.broadcasted_iota(jnp.int32, (rows8, 1), 0)
    ng = ng_ref[...]

    def body(s, carry):
        r0 = pl.multiple_of(s * SUBLANES, SUBLANES)
        in_s = (rid >= r0) & (rid < r0 + SUBLANES)
        for g in range(g_n):
            cg = c_ref[pl.ds(r0, SUBLANES), g * n:(g + 1) * n].astype(BF16)
            s_g = s0_ref[s, g * kp:(g + 1) * kp, :]
            yoff = _dot_nt(cg, s_g.astype(BF16))
            b_all = b_ref[:, g * n:(g + 1) * n]
            b_msk = jnp.where(in_s, b_all, 0.0).astype(BF16)
            upd = _dot(xwt_ref[g * kp:(g + 1) * kp, :].astype(BF16), b_msk)
            for hh in range(k_h):
                h = g * k_h + hh
                dsc = dec_ref[(i * sb + s) * n_heads + h]
                s1_ref[s, h * p:(h + 1) * p, :] = (dsc * s_g[hh * p:(hh + 1) * p, :]
                                                   + upd[hh * p:(hh + 1) * p, :])
            cols = slice(g * kp, (g + 1) * kp)
            yg = ((a_ref[pl.ds(r0, SUBLANES), cols] + ex_ref[pl.ds(r0, SUBLANES), cols] * yoff)
                  * zs_ref[pl.ds(r0, SUBLANES), cols])
            ms_ = jnp.mean(yg * yg, axis=-1, keepdims=True)
            y_ref[pl.ds(r0, SUBLANES), cols] = (yg * lax.rsqrt(ms_ + EPS) * ng[:, cols]).astype(y_ref.dtype)
        return carry

    lax.fori_loop(0, sb, body, 0)


def _ssd_sample_state(dec, s0_all, layer, s1_prev, a8, ex8, zs8, xwt, c8, b8, ng, g_n, n, k_h, p):
    depth, bs, hp, _ = s0_all.shape
    d = a8.shape[1]
    gn = g_n * n
    sb = min(SAMPLE_SEQ_BLOCK, bs)
    r8 = sb * SUBLANES
    kern = functools.partial(_ssd_sample_state_kernel, g_n=g_n, n=n, k_h=k_h, p=p, sb=sb)
    return pl.pallas_call(
        kern,
        grid=(bs // sb,),
        in_specs=[
            pl.BlockSpec(memory_space=pltpu.SMEM),
            pl.BlockSpec((None, sb, hp, n), lambda i: (layer, i, 0, 0)),
            pl.BlockSpec((r8, d), lambda i: (i, 0)),
            pl.BlockSpec((r8, d), lambda i: (i, 0)),
            pl.BlockSpec((r8, d), lambda i: (i, 0)),
            pl.BlockSpec((None, d, r8), lambda i: (i, 0, 0)),
            pl.BlockSpec((r8, gn), lambda i: (i, 0)),
            pl.BlockSpec((r8, gn), lambda i: (i, 0)),
            pl.BlockSpec((1, d), lambda i: (0, 0)),
            pl.BlockSpec(memory_space=pl.ANY),
        ],
        out_specs=[
            pl.BlockSpec((r8, d), lambda i: (i, 0)),
            pl.BlockSpec((None, sb, hp, n), lambda i: (layer, i, 0, 0)),
        ],
        out_shape=[
            jax.ShapeDtypeStruct((bs * SUBLANES, d), F32),
            jax.ShapeDtypeStruct((depth, bs, hp, n), F32),
        ],
        input_output_aliases={9: 1},
        compiler_params=_cparams(("arbitrary",)),
        name="ssd_sample_state",
    )(dec, s0_all, a8, ex8, zs8, xwt, c8, b8, ng, s1_prev)


def _ln_swish(v, g, b):
    mu = jnp.mean(v, axis=-1, keepdims=True)
    cen = v - mu
    var = jnp.mean(cen * cen, axis=-1, keepdims=True)
    return _silu(cen * lax.rsqrt(var + EPS) * g + b)


def _conf_prompt_kernel(u_ref, w_ref, b_ref, lg_ref, lb_ref, v_ref, upad, ush, *, halo):
    c = pl.program_id(1)
    L, d = u_ref.shape
    kw = w_ref.shape[0]
    span = L + halo - SUBLANES

    @pl.when(c == 0)
    def _():
        upad[0:halo, :] = jnp.zeros((halo, d), F32)

    u_b = u_ref[...]
    u = u_b.astype(F32)
    upad[halo:halo + L, :] = u
    row = lax.broadcasted_iota(jnp.int32, (L, L), 0)
    col = lax.broadcasted_iota(jnp.int32, (L, L), 1)
    for r in range(1, SUBLANES):
        ush[r - 1, 0:halo, :] = upad[r:r + halo, :]
        shifted = _dot(jnp.where(col == row + r, 1.0, 0.0).astype(BF16), u_b)
        ush[r - 1, halo:span, :] = shifted[0:span - halo, :]
    acc = jnp.broadcast_to(b_ref[...], (L, d))
    for k in range(kw):
        q, r = divmod(halo - (kw - 1) + k, SUBLANES)
        src = upad[q * SUBLANES:q * SUBLANES + L, :] if r == 0 else ush[r - 1, q * SUBLANES:q * SUBLANES + L, :]
        acc = acc + w_ref[k:k + 1, :] * src
    upad[0:halo, :] = u[L - halo:, :]
    v_ref[...] = _ln_swish(acc, lg_ref[...], lb_ref[...]).astype(v_ref.dtype)


def _conf_prompt(u, w, b, lg, lb, nb, t):
    assert u.dtype == BF16
    d = u.shape[1]
    kw = w.shape[0]
    L = min(CONF_CHUNK, t)
    nc = t // L
    halo = -(-(kw - 1) // SUBLANES) * SUBLANES
    const = lambda a: pl.BlockSpec(a.shape, lambda bb, c: (0,) * a.ndim)
    return pl.pallas_call(
        functools.partial(_conf_prompt_kernel, halo=halo),
        grid=(nb, nc),
        in_specs=[pl.BlockSpec((L, d), lambda bb, c: (bb * nc + c, 0)), const(w), const(b), const(lg), const(lb)],
        out_specs=pl.BlockSpec((L, d), lambda bb, c: (bb * nc + c, 0)),
        out_shape=jax.ShapeDtypeStruct((nb * t, d), BF16),
        scratch_shapes=[pltpu.VMEM((L + halo, d), F32), pltpu.VMEM((SUBLANES - 1, L + halo, d), F32)],
        compiler_params=_cparams(("arbitrary", "arbitrary")),
        name="conf_prompt",
    )(u, w, b, lg, lb)


def _conf_sample_kernel(u_ref, st_ref, w_ref, b_ref, lg_ref, lb_ref, v_ref):
    ts, sb, d = u_ref.shape
    kw = w_ref.shape[0]
    rows = [st_ref[j] for j in range(kw - 1)]
    rows += [u_ref[t].astype(F32) for t in range(ts)]
    for t in range(ts):
        acc = b_ref[...] + w_ref[0:1, :] * rows[t]
        for k in range(1, kw):
            acc = acc + w_ref[k:k + 1, :] * rows[t + k]
        v_ref[t] = _ln_swish(acc, lg_ref[...], lb_ref[...]).astype(v_ref.dtype)


def _conf_sample(u3, tblk, ts, cconv_t, layer, w, b, lg, lb):
    _, bs, d = u3.shape
    kw = w.shape[0]
    sb = min(CONF_SAMPLE_BLOCK, bs)
    const = lambda a: pl.BlockSpec(a.shape, lambda i: (0,) * a.ndim)
    return pl.pallas_call(
        _conf_sample_kernel,
        grid=(bs // sb,),
        in_specs=[
            pl.BlockSpec((ts, sb, d), lambda i: (tblk, i, 0)),
            pl.BlockSpec((None, kw - 1, sb, d), lambda i: (layer, 0, i, 0)),
            const(w), const(b), const(lg), const(lb),
        ],
        out_specs=pl.BlockSpec((ts, sb, d), lambda i: (0, i, 0)),
        out_shape=jax.ShapeDtypeStruct((ts, bs, d), BF16),
        compiler_params=_cparams(("arbitrary",)),
        name="conf_sample",
    )(u3, cconv_t, w, b, lg, lb)


def _merge_kernel(yp_ref, ys_ref, vp_ref, vs_ref, ws_ref, wc_ref, gs_ref, gc_ref, o_ref, *, npt):
    is_p = pl.program_id(0) < npt
    y = jnp.where(is_p, yp_ref[...], ys_ref[...])
    v = jnp.where(is_p, vp_ref[...], vs_ref[...])
    a = _dot(y, ws_ref[...])
    b = _dot(v, wc_ref[...])
    o_ref[...] = (gs_ref[...].astype(F32) * a + gc_ref[...].astype(F32) * b).astype(o_ref.dtype)


def _merge(y_p, y_s, v_p, v_s, ws, wc, layer, gates, tm):
    m, d = gates.shape[0], y_p.shape[1]
    npt = y_p.shape[0] // tm
    tn = _pick_tn(d)
    nj = d // tn
    p_spec = pl.BlockSpec((tm, d), lambda i, j: (jnp.minimum(i, npt - 1), 0))
    s_spec = pl.BlockSpec((tm, d), lambda i, j: (0, 0))
    return pl.pallas_call(
        functools.partial(_merge_kernel, npt=npt),
        grid=(m // tm, nj),
        in_specs=[
            p_spec, s_spec, p_spec, s_spec,
            pl.BlockSpec((None, d, tn), lambda i, j: (layer, 0, j)),
            pl.BlockSpec((None, d, tn), lambda i, j: (layer, 0, j)),
            pl.BlockSpec((tm, tn), lambda i, j: (i, j)),
            pl.BlockSpec((tm, tn), lambda i, j: (i, nj + j)),
        ],
        out_specs=pl.BlockSpec((tm, tn), lambda i, j: (i, j)),
        out_shape=jax.ShapeDtypeStruct((m, d), BF16),
        compiler_params=_cparams(("arbitrary", "arbitrary")),
        name="merge",
    )(y_p, y_s, v_p, v_s, ws, wc, gates, gates)


def _outproj_kernel(mx_ref, w_ref, x_ref, gp_ref, gs_ref, o_ref, *, npt, ts):
    gate = _tile_mod(pl.program_id(0), npt, ts, gp_ref, gs_ref)
    o_ref[...] = x_ref[...] + gate * _dot(mx_ref[...], w_ref[...])


def _outproj(mixed, wo, layer, x, modp, mods, sec, dims):
    m, d = x.shape
    tm, npt, tps, nb, bs, ts = dims
    tn = _pick_tn(d)
    nj = d // tn
    colj = lambda s, i, j: s * nj + j
    return pl.pallas_call(
        functools.partial(_outproj_kernel, npt=npt, ts=ts),
        grid=(m // tm, nj),
        in_specs=[
            pl.BlockSpec((tm, d), lambda i, j: (i, 0)),
            pl.BlockSpec((None, d, tn), lambda i, j: (layer, 0, j)),
            pl.BlockSpec((tm, tn), lambda i, j: (i, j)),
            *_mod_specs(npt, tps, nb, bs, tn, sec, colj),
        ],
        out_specs=pl.BlockSpec((tm, tn), lambda i, j: (i, j)),
        out_shape=jax.ShapeDtypeStruct((m, d), F32),
        compiler_params=_cparams(("arbitrary", "arbitrary")),
        name="outproj",
    )(mixed, wo, x, modp, mods)


def _pack_bf16_pairs(v):
    half = v.shape[1] // 2
    bits = pltpu.bitcast(v.astype(BF16).astype(F32), jnp.uint32)
    return bits[:, :half] | (bits[:, half:] >> 16)


def _unpack_bf16_pairs(w):
    left = pltpu.bitcast(w & jnp.uint32(0xFFFF0000), F32)
    right = pltpu.bitcast(w << 16, F32)
    return jnp.concatenate([left, right], axis=1)


INFO_RANK1, INFO_RANK2, INFO_E1, INFO_E2, INFO_W1, INFO_W2 = range(6)


def _router_kernel(x_ref, g_ref, scp_ref, scs_ref, shp_ref, shs_ref, whi_ref, wlo_ref, rb_ref,
                   hp_ref, info_ref, cnt_ref, *, npt, ts, n_e, n_g):
    i = pl.program_id(0)
    epg = n_e // n_g

    @pl.when(i == 0)
    def _():
        cnt_ref[...] = jnp.zeros(cnt_ref.shape, F32)

    x = x_ref[...]
    xn = x * lax.rsqrt(jnp.mean(x * x, axis=-1, keepdims=True) + EPS) * g_ref[...]
    sc = _tile_mod(i, npt, ts, scp_ref, scs_ref)
    sh = _tile_mod(i, npt, ts, shp_ref, shs_ref)
    h = xn * (1.0 + sc) + sh
    hi, lo = _split2(h)
    hp_ref[...] = _pack_bf16_pairs(h)
    logit = _dot(hi, whi_ref[...]) + _dot(lo, whi_ref[...]) + _dot(hi, wlo_ref[...]) + rb_ref[...]
    tm = logit.shape[0]
    lane = lax.broadcasted_iota(jnp.int32, (tm, LANES), 1).astype(F32)
    first = lambda hit: jnp.min(jnp.where(hit, lane, float(LANES)), axis=-1, keepdims=True)
    gl = jnp.where(lane >= n_e, jnp.where(lane < n_e + n_g, logit, NEG_BIG), NEG_BIG)
    gmax = jnp.max(gl, axis=-1, keepdims=True)
    g_top = 1.0 / jnp.sum(jnp.exp(gl - gmax), axis=-1, keepdims=True)
    g_idx = first(gl == gmax) - n_e
    el = jnp.where(lane >= g_idx * epg, jnp.where(lane < (g_idx + 1.0) * epg, logit, NEG_BIG), NEG_BIG)
    m1 = jnp.max(el, axis=-1, keepdims=True)
    i1 = first(el == m1)
    el2 = jnp.where(lane == i1, NEG_BIG, el)
    m2 = jnp.max(el2, axis=-1, keepdims=True)
    i2 = first(el2 == m2)
    r = jnp.exp(m2 - m1)
    w1 = 1.0 / (1.0 + r)
    w2 = r * w1
    oh1 = jnp.where(lane == i1, 1.0, 0.0)
    oh2 = jnp.where(lane == i2, 1.0, 0.0)
    both = oh1 + oh2
    row = lax.broadcasted_iota(jnp.int32, (tm, tm), 0)
    col = lax.broadcasted_iota(jnp.int32, (tm, tm), 1)
    before = _dot(jnp.where(row > col, 1.0, 0.0).astype(BF16), both.astype(BF16)) + cnt_ref[0:1, :]
    rank1 = jnp.sum(before * oh1, axis=-1, keepdims=True)
    rank2 = jnp.sum(before * oh2, axis=-1, keepdims=True)
    cnt_ref[...] = cnt_ref[...] + jnp.sum(both, axis=0, keepdims=True)
    rec = [rank1, rank2, i1, i2, g_top * w1, g_top * w2]
    info = jnp.zeros((tm, LANES), F32)
    for k, val in enumerate(rec):
        info = jnp.where(lane == float(k), val, info)
    info_ref[...] = info


def _router(x, gain, modp, mods, sec_sc, sec_sh, whi, wlo, rb, n_e, n_g, dims):
    m, d = x.shape
    tm, npt, tps, nb, bs, ts = dims
    col0 = lambda sec, i: sec
    return pl.pallas_call(
        functools.partial(_router_kernel, npt=npt, ts=ts, n_e=n_e, n_g=n_g),
        grid=(m // tm,),
        in_specs=[
            pl.BlockSpec((tm, d), lambda i: (i, 0)),
            pl.BlockSpec((1, d), lambda i: (0, 0)),
            *_mod_specs(npt, tps, nb, bs, d, sec_sc, col0),
            *_mod_specs(npt, tps, nb, bs, d, sec_sh, col0),
            pl.BlockSpec((d, LANES), lambda i: (0, 0)),
            pl.BlockSpec((d, LANES), lambda i: (0, 0)),
            pl.BlockSpec((1, LANES), lambda i: (0, 0)),
        ],
        out_specs=[pl.BlockSpec((tm, d // 2), lambda i: (i, 0)), pl.BlockSpec((tm, LANES), lambda i: (i, 0)),
                   pl.BlockSpec((SUBLANES, LANES), lambda i: (0, 0))],
        out_shape=[jax.ShapeDtypeStruct((m, d // 2), jnp.uint32), jax.ShapeDtypeStruct((m, LANES), F32),
                   jax.ShapeDtypeStruct((SUBLANES, LANES), F32)],
        compiler_params=_cparams(("arbitrary",)),
        name="router",
    )(x, gain, modp, mods, modp, mods, whi, wlo, rb)


def _slots_kernel(info_ref, offs_ref, pos_ref):
    info = info_ref[...]
    lane = lax.broadcasted_iota(jnp.int32, info.shape, 1).astype(F32)
    offs = offs_ref[...]
    base1 = jnp.sum(jnp.where(lane == info[:, INFO_E1:INFO_E1 + 1], offs, 0.0), axis=-1, keepdims=True)
    base2 = jnp.sum(jnp.where(lane == info[:, INFO_E2:INFO_E2 + 1], offs, 0.0), axis=-1, keepdims=True)
    rec = jnp.where(lane == 0.0, info[:, INFO_RANK1:INFO_RANK1 + 1] + base1,
                    jnp.where(lane == 1.0, info[:, INFO_RANK2:INFO_RANK2 + 1] + base2, 0.0))
    pos_ref[...] = rec.T[0:SUBLANES, :].astype(jnp.int32)


def _slots(info, offs_row, tm):
    m = info.shape[0]
    return pl.pallas_call(
        _slots_kernel,
        grid=(m // tm,),
        in_specs=[pl.BlockSpec((tm, LANES), lambda i: (i, 0)), pl.BlockSpec((1, LANES), lambda i: (0, 0))],
        out_specs=pl.BlockSpec((SUBLANES, tm), lambda i: (0, i)),
        out_shape=jax.ShapeDtypeStruct((SUBLANES, m), jnp.int32),
        compiler_params=_cparams(("arbitrary",)),
        name="moe_slots",
    )(info, offs_row)


def _moe_plan(info, cnt, n_e, tg, nt_max, tm):
    counts = cnt[0, :n_e].astype(jnp.int32)
    tiles = (counts + tg - 1) // tg
    tile_end = jnp.cumsum(tiles)
    offs = (tile_end - tiles) * tg
    n_used = tile_end[n_e - 1]
    offs_row = jnp.zeros((1, LANES), F32).at[0, :n_e].set(offs.astype(F32))
    pos = _slots(info, offs_row, tm)
    pos1, pos2 = pos[0], pos[1]
    tile = jnp.minimum(jnp.arange(nt_max, dtype=jnp.int32), n_used - 1)
    tile_expert = jnp.searchsorted(tile_end, tile, side="right").astype(jnp.int32)
    return pos1, pos2, tile_expert, n_used.reshape(1)


def _dispatch_kernel(pos1_ref, pos2_ref, hp_ref, xs0_ref, xs_ref, sem):
    del xs0_ref
    tm = hp_ref.shape[0]
    base = pl.program_id(0) * tm

    def issue(r, carry):
        src = hp_ref.at[pl.ds(r, 1)]
        pltpu.make_async_copy(src, xs_ref.at[pl.ds(pos1_ref[base + r], 1)], sem).start(priority=0)
        pltpu.make_async_copy(src, xs_ref.at[pl.ds(pos2_ref[base + r], 1)], sem).start(priority=1)
        return carry
    lax.fori_loop(0, tm, issue, 0, unroll=ROW_DMA_UNROLL)

    def drain(r, carry):
        for _ in range(2):
            pltpu.make_async_copy(hp_ref.at[pl.ds(0, 1)], xs_ref.at[pl.ds(0, 1)], sem).wait()
        return carry
    lax.fori_loop(0, tm, drain, 0, unroll=ROW_DMA_UNROLL)


def _dispatch(pos1, pos2, hp, n_slots, tm):
    m, half = hp.shape
    smem = pl.BlockSpec(memory_space=pltpu.SMEM)
    hbm = pl.BlockSpec(memory_space=pl.ANY)
    return pl.pallas_call(
        _dispatch_kernel,
        grid=(m // tm,),
        in_specs=[smem, smem, pl.BlockSpec((tm, half), lambda i: (i, 0)), hbm],
        out_specs=hbm,
        out_shape=jax.ShapeDtypeStruct((n_slots, half), jnp.uint32),
        scratch_shapes=[pltpu.SemaphoreType.DMA(())],
        input_output_aliases={3: 0},
        compiler_params=_cparams(("arbitrary",)),
        name="moe_dispatch",
    )(pos1, pos2, hp, jnp.zeros((n_slots, half), jnp.uint32))


def _experts_kernel(te_ref, nu_ref, xs_ref, wg_ref, wu_ref, wd_ref, ys_ref):
    del te_ref
    i = pl.program_id(0)

    @pl.when(i < nu_ref[0])
    def _():
        x = _unpack_bf16_pairs(xs_ref[...]).astype(BF16)
        hid = _silu(_dot(x, wg_ref[...].astype(BF16))) * _dot(x, wu_ref[...].astype(BF16))
        ys_ref[...] = _pack_bf16_pairs(_dot(hid.astype(BF16), wd_ref[...].astype(BF16)))

    @pl.when(i >= nu_ref[0])
    def _():
        ys_ref[...] = jnp.zeros(ys_ref.shape, jnp.uint32)


def _experts(tile_expert, n_used, xs, wg, wu, wd, layer, tg):
    n_slots, half = xs.shape
    _, _, d, f = wg.shape
    grid_spec = pltpu.PrefetchScalarGridSpec(
        num_scalar_prefetch=2,
        grid=(n_slots // tg,),
        in_specs=[
            pl.BlockSpec((tg, half), lambda i, te, nu: (i, 0)),
            pl.BlockSpec((None, None, d, f), lambda i, te, nu: (layer, te[i], 0, 0)),
            pl.BlockSpec((None, None, d, f), lambda i, te, nu: (layer, te[i], 0, 0)),
            pl.BlockSpec((None, None, f, d), lambda i, te, nu: (layer, te[i], 0, 0)),
        ],
        out_specs=pl.BlockSpec((tg, half), lambda i, te, nu: (i, 0)),
    )
    return pl.pallas_call(
        _experts_kernel,
        grid_spec=grid_spec,
        out_shape=jax.ShapeDtypeStruct((n_slots, half), jnp.uint32),
        compiler_params=_cparams(("arbitrary",)),
        name="moe_experts",
    )(tile_expert, n_used, xs, wg, wu, wd)


def _combine_kernel(pos1_ref, pos2_ref, info_ref, x_ref, gp_ref, gs_ref, ys_ref, o_ref, buf_a, buf_b, sem,
                    *, npt, ts):
    i = pl.program_id(0)
    tm = x_ref.shape[0]
    slot = lax.rem(i, 2)

    def issue(tile, s):
        def body(r, carry):
            t = tile * tm + r
            pltpu.make_async_copy(ys_ref.at[pl.ds(pos1_ref[t], 1)], buf_a.at[s, pl.ds(r, 1)], sem.at[s]).start(priority=0)
            pltpu.make_async_copy(ys_ref.at[pl.ds(pos2_ref[t], 1)], buf_b.at[s, pl.ds(r, 1)], sem.at[s]).start(priority=1)
            return carry
        lax.fori_loop(0, tm, body, 0, unroll=ROW_DMA_UNROLL)

    @pl.when(i == 0)
    def _():
        issue(0, 0)

    @pl.when(i + 1 < pl.num_programs(0))
    def _():
        issue(i + 1, 1 - slot)

    def wait_body(r, carry):
        pltpu.make_async_copy(ys_ref.at[pl.ds(0, 1)], buf_a.at[slot, pl.ds(0, 1)], sem.at[slot]).wait()
        pltpu.make_async_copy(ys_ref.at[pl.ds(0, 1)], buf_b.at[slot, pl.ds(0, 1)], sem.at[slot]).wait()
        return carry
    lax.fori_loop(0, tm, wait_body, 0, unroll=ROW_DMA_UNROLL)

    info = info_ref[...]
    w1 = info[:, INFO_W1:INFO_W1 + 1]
    w2 = info[:, INFO_W2:INFO_W2 + 1]
    moe = w1 * _unpack_bf16_pairs(buf_a[slot]) + w2 * _unpack_bf16_pairs(buf_b[slot])
    o_ref[...] = x_ref[...] + _tile_mod(i, npt, ts, gp_ref, gs_ref) * moe


def _combine(pos1, pos2, info, x, modp, mods, sec, ys, dims):
    m, d = x.shape
    tm, npt, tps, nb, bs, ts = dims
    col0 = lambda s, i: s
    smem = pl.BlockSpec(memory_space=pltpu.SMEM)
    return pl.pallas_call(
        functools.partial(_combine_kernel, npt=npt, ts=ts),
        grid=(m // tm,),
        in_specs=[
            smem, smem,
            pl.BlockSpec((tm, LANES), lambda i: (i, 0)),
            pl.BlockSpec((tm, d), lambda i: (i, 0)),
            *_mod_specs(npt, tps, nb, bs, d, sec, col0),
            pl.BlockSpec(memory_space=pl.ANY),
        ],
        out_specs=pl.BlockSpec((tm, d), lambda i: (i, 0)),
        out_shape=jax.ShapeDtypeStruct((m, d), F32),
        scratch_shapes=[pltpu.VMEM((2, tm, d // 2), jnp.uint32), pltpu.VMEM((2, tm, d // 2), jnp.uint32),
                        pltpu.SemaphoreType.DMA((2,))],
        compiler_params=_cparams(("arbitrary",)),
        name="moe_combine",
    )(pos1, pos2, info, x, modp, mods, ys)


def _moe(hp, info, cnt, wg, wu, wd, layer, x, modp, mods, sec, dims):
    m = x.shape[0]
    n_e = wg.shape[1]
    tg = MOE_TG
    nt_max = -(-2 * m // tg) + n_e
    pos1, pos2, tile_expert, n_used = _moe_plan(info, cnt, n_e, tg, nt_max, dims[0])
    xs = _dispatch(pos1, pos2, hp, nt_max * tg, dims[0])
    ys = _experts(tile_expert, n_used, xs, wg, wu, wd, layer, tg)
    return _combine(pos1, pos2, info, x, modp, mods, sec, ys, dims)


def _final_norm_kernel(x_ref, g_ref, o_ref):
    x = x_ref[...]
    o_ref[...] = x * lax.rsqrt(jnp.mean(x * x, axis=-1, keepdims=True) + EPS) * g_ref[...]


def _final_norm(x, g, tm, tile0, ntiles):
    d = x.shape[1]
    return pl.pallas_call(
        _final_norm_kernel,
        grid=(ntiles,),
        in_specs=[pl.BlockSpec((tm, d), lambda i: (tile0 + i, 0)), pl.BlockSpec((1, d), lambda i: (0, 0))],
        out_specs=pl.BlockSpec((tm, d), lambda i: (i, 0)),
        out_shape=jax.ShapeDtypeStruct((ntiles * tm, d), F32),
        compiler_params=_cparams(("arbitrary",)),
        name="final_norm",
    )(x, g)


def _to_seq8(a):
    ts, bs, c = a.shape
    a = jnp.transpose(a, (1, 0, 2))
    a = jnp.pad(a, ((0, 0), (0, SUBLANES - ts), (0, 0)))
    return a.reshape(bs * SUBLANES, c)


def kernel(x_prompt, x_sample, c_prompt, c_sample, state_ssm, state_ssd_conv, state_conf_conv, norm_mix_g, norm_ffn_g, w_ada, b_ada, w_in, ssd_conv_w, ssd_conv_b, ssd_dt_bias, ssd_a_log, ssd_d, ssd_norm_g, w_ssd_out, conf_conv_w, conf_conv_b, conf_ln_g, conf_ln_b, w_conf_out, w_o, w_group_router, b_group_router, w_expert_router, b_expert_router, w_exp_gate, w_exp_up, w_exp_down, final_norm_g):
    nb, t, d = x_prompt.shape
    bs, ts, _ = x_sample.shape
    depth = w_in.shape[0]
    n_h = ssd_dt_bias.shape[1]
    p = d // n_h
    n = state_ssm.shape[-1]
    cc = ssd_conv_w.shape[-1]
    g_n = (cc - d) // (2 * n)
    k_h = n_h // g_n
    gn = g_n * n
    kc = conf_conv_w.shape[1]
    n_e = w_exp_gate.shape[1]
    n_g = w_group_router.shape[-1]
    tm = bs * ts
    assert t % tm == 0 and tm % SUBLANES == 0 and bs % SUBLANES == 0 and ts <= SUBLANES
    assert 2 * p == LANES and k_h % 2 == 0 and n_h <= LANES and n_e + n_g <= LANES
    npt = nb * t // tm
    dims = (tm, npt, t // tm, nb, bs, ts)
    mp = nb * t

    x = jnp.concatenate([x_prompt.reshape(mp, d), jnp.transpose(x_sample, (1, 0, 2)).reshape(tm, d)], axis=0)

    rc = -(-(nb + bs) // SUBLANES) * SUBLANES
    c_all = jnp.concatenate([c_prompt, c_sample, jnp.zeros((rc - nb - bs, d), F32)], axis=0)
    mod = _adaln(c_all, w_ada, b_ada)

    head_of = jnp.arange(d) // p
    eh = (jnp.arange(LANES)[:, None] == head_of[None, :]).astype(BF16)
    gh = ((jnp.arange(gn)[:, None] // n) == (jnp.arange(LANES)[None, :] // k_h)).astype(BF16)
    gh = gh * (jnp.arange(LANES)[None, :] < n_h).astype(BF16)
    pad_h = lambda v: jnp.pad(v, (0, LANES - n_h)).reshape(1, LANES)

    m = mp + tm
    tblk = mp // bs // ts
    assert tblk * bs * ts == mp
    sconv_t = jnp.transpose(state_ssd_conv, (0, 2, 1, 3))
    cconv_t = jnp.transpose(state_conf_conv, (0, 2, 1, 3))
    ssm0_all = state_ssm.reshape(depth, bs, n_h * p, n)
    ssm_s = jnp.zeros(ssm0_all.shape, F32)
    o_dt, o_glu = d + cc, d + cc + n_h
    c_in = w_in.shape[2]
    w_bf = jnp.pad(w_in, ((0, 0), (0, 0), (0, -c_in % LANES + LANES))).astype(BF16)
    w_dt = jnp.pad(w_in[:, :, o_dt:o_glu], ((0, 0), (0, 0), (0, LANES - n_h))).astype(BF16)
    w_so, w_co, w_oo = w_ssd_out.astype(BF16), w_conf_out.astype(BF16), w_o.astype(BF16)
    ssm_p, sconv_p, cconv_p, sconv_s, cconv_s = [], [], [], [], []
    for l in range(depth):
        modp = mod[l, :nb].reshape(nb, 1, 6 * d)
        mods = mod[l, nb:nb + bs]
        h1 = _modnorm(x, norm_mix_g[l].reshape(1, d), modp, mods, 1, 0, dims)
        zx = _proj(h1, w_bf, [0], o_dt, l, BF16, "plain")
        dtraw = _proj(h1, w_dt, [0], LANES, l, F32, "plain")
        u = _proj(h1, w_bf, [o_glu, o_glu + d], d, l, BF16, "glu")
        gates = _proj(h1, w_bf, [o_glu + 2 * d], 2 * d, l, BF16, "sigmoid")

        cw = ssd_conv_w[l]
        cb = ssd_conv_b[l].reshape(1, cc)
        dtb = pad_h(ssd_dt_bias[l])
        alog = pad_h(ssd_a_log[l])
        dexp = jnp.repeat(ssd_d[l], p).reshape(1, d)
        ng = ssd_norm_g[l].reshape(1, d)

        y_p, s_p = _ssd_prompt(zx, dtraw, cw, cb, dtb, alog, dexp, ng, eh, nb, t, n_h * p, n, g_n, k_h, p)
        zx_s = zx[mp:].reshape(ts, bs, d + cc)
        sconv0 = state_ssd_conv[l]
        a_t, ex_t, zs_t, xw_t, c_t, b_t, dec = _ssd_sample_pre(
            zx.reshape(m // bs, bs, d + cc), dtraw.reshape(m // bs, bs, LANES), tblk, ts, sconv_t, l,
            cw, cb, dtb, alog, dexp, eh, gh, d, g_n, n)
        sb = min(SAMPLE_SEQ_BLOCK, bs)
        xwt = _to_seq8(xw_t).reshape(bs // sb, sb * SUBLANES, d).transpose(0, 2, 1)
        dec = dec[:, :n_h].reshape(bs * n_h)
        y_s8, ssm_s = _ssd_sample_state(dec, ssm0_all, l, ssm_s, _to_seq8(a_t), _to_seq8(ex_t),
                                        _to_seq8(zs_t), xwt, _to_seq8(c_t), _to_seq8(b_t), ng, g_n, n, k_h, p)
        y_s = jnp.transpose(y_s8.reshape(bs, SUBLANES, d)[:, :ts], (1, 0, 2)).reshape(tm, d).astype(BF16)

        ccw = conf_conv_w[l]
        ccb = conf_conv_b[l].reshape(1, d)
        lg = conf_ln_g[l].reshape(1, d)
        lb = conf_ln_b[l].reshape(1, d)
        v_p = _conf_prompt(u, ccw, ccb, lg, lb, nb, t)
        u_s = u[mp:].reshape(ts, bs, d)
        cconv0 = state_conf_conv[l]
        v_s = _conf_sample(u.reshape(m // bs, bs, d), tblk, ts, cconv_t, l, ccw, ccb, lg, lb).reshape(tm, d)

        mixed = _merge(y_p, y_s, v_p, v_s, w_so, w_co, l, gates, tm)
        x = _outproj(mixed, w_oo, l, x, modp, mods, 2, dims)

        w_r = jnp.concatenate([w_expert_router[l], w_group_router[l],
                               jnp.zeros((d, LANES - n_e - n_g), F32)], axis=1)
        w_r_hi = w_r.astype(BF16)
        w_r_lo = (w_r - w_r_hi.astype(F32)).astype(BF16)
        rb = jnp.concatenate([b_expert_router[l], b_group_router[l], jnp.zeros((LANES - n_e - n_g,), F32)]).reshape(1, LANES)
        hp, info, cnt = _router(x, norm_ffn_g[l].reshape(1, d), modp, mods, 4, 3, w_r_hi, w_r_lo, rb, n_e, n_g, dims)
        x = _moe(hp, info, cnt, w_exp_gate, w_exp_up, w_exp_down, l, x, modp, mods, 5, dims)

        ssm_p.append(s_p.reshape(nb, n_h, p, n))
        k3 = sconv0.shape[1]
        sconv_p.append(jnp.stack([zx[(b + 1) * t - k3:(b + 1) * t, d:] for b in range(nb)]).astype(F32))
        cconv_p.append(jnp.stack([u[(b + 1) * t - (kc - 1):(b + 1) * t] for b in range(nb)]).astype(F32))
        xraw_s = jnp.transpose(zx_s[:, :, d:], (1, 0, 2)).astype(F32)
        sconv_s.append(jnp.concatenate([sconv0, xraw_s], axis=1)[:, ts:])
        cconv_s.append(jnp.concatenate([cconv0, jnp.transpose(u_s, (1, 0, 2)).astype(F32)], axis=1)[:, ts:])

    fg = final_norm_g.reshape(1, d)
    y_prompt = _final_norm(x, fg, tm, 0, npt).reshape(nb, t, d)
    y_sample = jnp.transpose(_final_norm(x, fg, tm, npt, 1).reshape(ts, bs, d), (1, 0, 2))
    return (y_prompt, y_sample, jnp.stack(ssm_p), jnp.stack(sconv_p), jnp.stack(cconv_p),
            ssm_s.reshape(depth, bs, n_h, p, n), jnp.stack(sconv_s), jnp.stack(cconv_s))
```

```python
import functools

import jax
import jax.numpy as jnp
from jax import lax
from jax.experimental import pallas as pl
from jax.experimental.pallas import tpu as pltpu

F32 = jnp.float32
BF16 = jnp.bfloat16
EPS = 1e-6
LANES = 128
SUBLANES = 8
VMEM_LIMIT = 52 * 1024 * 1024
NEG_BIG = -1e30
SSD_CHUNK = 128
CONF_CHUNK = 256
SAMPLE_SEQ_BLOCK = 8
SAMPLE_PRE_BLOCK = 32
CONF_SAMPLE_BLOCK = 16
MOE_TG = 256
ROW_DMA_UNROLL = 8
ADA_TN = 1024
PROJ_TN = 1024
PROJ_TM = 1088


def _pick_tn(ncols):
    assert ncols % LANES == 0
    return max(tn for tn in range(LANES, min(PROJ_TN, ncols) + 1, LANES) if ncols % tn == 0)


def _pick_rows(m):
    return max(tm for tm in range(2 * SUBLANES, min(PROJ_TM, m) + 1, 2 * SUBLANES) if m % tm == 0)


def _cparams(sem):
    return pltpu.CompilerParams(dimension_semantics=sem, vmem_limit_bytes=VMEM_LIMIT)


def _silu(x):
    return x * jax.nn.sigmoid(x)


def _softplus(x):
    return jnp.maximum(x, 0.0) + jnp.log1p(jnp.exp(-jnp.abs(x)))


def _split2(x):
    hi = x.astype(BF16)
    lo = (x - hi.astype(F32)).astype(BF16)
    return hi, lo


def _split3(x):
    hi = x.astype(BF16)
    r = x - hi.astype(F32)
    mid = r.astype(BF16)
    lo = (r - mid.astype(F32)).astype(BF16)
    return hi, mid, lo


def _dot(a, b):
    return jnp.dot(a, b, preferred_element_type=F32)


def _dot_nt(a, b):
    return lax.dot_general(a, b, (((1,), (1,)), ((), ())), preferred_element_type=F32)


def _dot_parts(parts, w):
    acc = _dot(parts[0], w)
    for p in parts[1:]:
        acc = acc + _dot(p, w)
    return acc


def _adaln_kernel(c_ref, w_ref, b_ref, o_ref):
    s = _silu(c_ref[...]).astype(BF16)
    o_ref[...] = _dot(s, w_ref[...].astype(BF16)) + b_ref[...]


def _adaln(c_all, w_ada, b_ada):
    depth, d, n6 = w_ada.shape
    rc = c_all.shape[0]
    tn = min(ADA_TN, n6)
    return pl.pallas_call(
        _adaln_kernel,
        grid=(depth, n6 // tn),
        in_specs=[
            pl.BlockSpec((rc, d), lambda l, j: (0, 0)),
            pl.BlockSpec((None, d, tn), lambda l, j: (l, 0, j)),
            pl.BlockSpec((None, 1, tn), lambda l, j: (l, 0, j)),
        ],
        out_specs=pl.BlockSpec((None, rc, tn), lambda l, j: (l, 0, j)),
        out_shape=jax.ShapeDtypeStruct((depth, rc, n6), F32),
        compiler_params=_cparams(("arbitrary", "arbitrary")),
        name="adaln",
    )(c_all, w_ada, b_ada.reshape(depth, 1, n6))


def _tile_mod(i, npt, ts, p_ref, s_ref):
    ms = s_ref[...]
    tiled = jnp.concatenate([ms] * ts, axis=0)
    return jnp.where(i < npt, p_ref[...], tiled)


def _mod_specs(npt, tiles_per_seq, nseq_p, bs, width, sec, col_of):
    def p_map(*idx):
        i = idx[0]
        return (jnp.minimum(i // tiles_per_seq, nseq_p - 1), 0, col_of(sec, *idx))

    def s_map(*idx):
        return (0, col_of(sec, *idx))

    return [pl.BlockSpec((None, 1, width), p_map), pl.BlockSpec((bs, width), s_map)]


def _modnorm_kernel(x_ref, g_ref, scp_ref, scs_ref, shp_ref, shs_ref, h_ref, *, npt, ts):
    i = pl.program_id(0)
    x = x_ref[...]
    xn = x * lax.rsqrt(jnp.mean(x * x, axis=-1, keepdims=True) + EPS) * g_ref[...]
    sc = _tile_mod(i, npt, ts, scp_ref, scs_ref)
    sh = _tile_mod(i, npt, ts, shp_ref, shs_ref)
    h_ref[...] = (xn * (1.0 + sc) + sh).astype(h_ref.dtype)


def _modnorm(x, gain, modp, mods, sec_sc, sec_sh, dims):
    m, d = x.shape
    tm, npt, tps, nb, bs, ts = dims
    col0 = lambda sec, i: sec
    return pl.pallas_call(
        functools.partial(_modnorm_kernel, npt=npt, ts=ts),
        grid=(m // tm,),
        in_specs=[
            pl.BlockSpec((tm, d), lambda i: (i, 0)),
            pl.BlockSpec((1, d), lambda i: (0, 0)),
            *_mod_specs(npt, tps, nb, bs, d, sec_sc, col0),
            *_mod_specs(npt, tps, nb, bs, d, sec_sh, col0),
        ],
        out_specs=pl.BlockSpec((tm, d), lambda i: (i, 0)),
        out_shape=jax.ShapeDtypeStruct((m, d), BF16),
        compiler_params=_cparams(("arbitrary",)),
        name="modnorm",
    )(x, gain, modp, mods, modp, mods)


def _proj_kernel(h_ref, *rest, nw, epilogue):
    w_refs = rest[:nw]
    o_ref = rest[nw]
    h = h_ref[...]
    acc = _dot(h, w_refs[0][...])
    if epilogue == "glu":
        acc = acc * jax.nn.sigmoid(_dot(h, w_refs[1][...]))
    elif epilogue == "sigmoid":
        acc = jax.nn.sigmoid(acc)
    o_ref[...] = acc.astype(o_ref.dtype)


def _proj(h, w, col0s, ncols, layer, out_dtype, epilogue):
    m, d = h.shape
    tn = _pick_tn(ncols)
    tm = _pick_rows(m)
    assert all(c0 % tn == 0 for c0 in col0s)
    w_specs = [pl.BlockSpec((None, d, tn), lambda j, i, off=c0 // tn: (layer, 0, off + j)) for c0 in col0s]
    return pl.pallas_call(
        functools.partial(_proj_kernel, nw=len(col0s), epilogue=epilogue),
        grid=(ncols // tn, m // tm),
        in_specs=[pl.BlockSpec((tm, d), lambda j, i: (i, 0))] + w_specs,
        out_specs=pl.BlockSpec((tm, tn), lambda j, i: (i, j)),
        out_shape=jax.ShapeDtypeStruct((m, ncols), out_dtype),
        compiler_params=_cparams(("arbitrary", "arbitrary")),
        name="proj_" + epilogue,
    )(h, *([w] * len(col0s)))


def _ssd_prompt_kernel(zx_ref, dt_ref, cw_ref, cb_ref, dtb_ref, alog_ref, dexp_ref, ng_ref, eh_ref,
                       y_ref, st_ref, xpad, y_scr, *, d, g_n, n, k_h, p):
    c = pl.program_id(1)
    L = zx_ref.shape[0]
    kp = k_h * p
    cc = cw_ref.shape[1]
    kw = cw_ref.shape[0]

    @pl.when(c == 0)
    def _():
        xpad[0:SUBLANES, :] = jnp.zeros((SUBLANES, cc), F32)
        st_ref[...] = jnp.zeros(st_ref.shape, F32)

    zx = zx_ref[...]
    z = zx[:, :d].astype(F32)
    xr_b = zx[:, d:]
    xr = xr_b.astype(F32)
    xpad[SUBLANES:2 * SUBLANES, :] = xr[0:SUBLANES, :]
    row = lax.broadcasted_iota(jnp.int32, (L, L), 0)
    col = lax.broadcasted_iota(jnp.int32, (L, L), 1)
    conv = cb_ref[...] + cw_ref[kw - 1:kw, :] * xr
    head = cb_ref[...] + cw_ref[kw - 1:kw, :] * xr[0:SUBLANES, :]
    for s in range(1, kw):
        back = _dot(jnp.where(col == row - s, 1.0, 0.0).astype(BF16), xr_b)
        conv = conv + cw_ref[kw - 1 - s:kw - s, :] * back
        head = head + cw_ref[kw - 1 - s:kw - s, :] * xpad[SUBLANES - s:2 * SUBLANES - s, :]
    conv = jnp.concatenate([head, conv[SUBLANES:, :]], axis=0)
    xpad[0:SUBLANES, :] = xr[L - SUBLANES:, :]
    xbc = _silu(conv)
    xs = xbc[:, :d]
    bm = xbc[:, d:d + g_n * n]
    cm = xbc[:, d + g_n * n:]

    dt = _softplus(dt_ref[...] + dtb_ref[...])
    a = -jnp.exp(alog_ref[...])
    dta = dt * a
    row = lax.broadcasted_iota(jnp.int32, (L, L), 0)
    col = lax.broadcasted_iota(jnp.int32, (L, L), 1)
    causal = row >= col
    tril = jnp.where(causal, 1.0, 0.0).astype(BF16)
    a_hi, a_mid, a_lo = _split3(dta)
    acum = _dot(tril, a_hi) + _dot(tril, a_mid) + _dot(tril, a_lo)
    acum_t = acum.T
    dt_t = dt.T
    alast = acum[L - 1:L, :]
    wend = jnp.exp(alast - acum) * dt
    eac = jnp.exp(acum)
    eh = eh_ref[...]
    wend_x = _dot_parts(_split2(wend), eh)
    eac_x = _dot_parts(_split2(eac), eh)
    xw = xs * wend_x
    dec = jnp.exp(alast)
    lane = lax.broadcasted_iota(jnp.int32, (L, 2 * p), 1)

    for g in range(g_n):
        bg = bm[:, g * n:(g + 1) * n].astype(BF16)
        cg = cm[:, g * n:(g + 1) * n].astype(BF16)
        cbm = _dot_nt(cg, bg)
        s_g = st_ref[g * kp:(g + 1) * kp, :]
        yoff = _dot_nt(cg, s_g.astype(BF16))
        for j in range(k_h // 2):
            h0 = g * k_h + 2 * j
            ms = []
            for h in (h0, h0 + 1):
                seg = acum[:, h:h + 1] - acum_t[h:h + 1, :]
                dcy = jnp.exp(jnp.where(causal, seg, NEG_BIG))
                ms.append((cbm * dcy * dt_t[h:h + 1, :]).astype(BF16))
            lhs = jnp.concatenate(ms, axis=1)
            xp = xs[:, h0 * p:(h0 + 2) * p]
            rhs = jnp.concatenate([jnp.where(lane < p, xp, 0.0), jnp.where(lane >= p, xp, 0.0)],
                                  axis=0).astype(BF16)
            lo = h0 * p
            y_scr[:, lo:lo + 2 * p] = (_dot(lhs, rhs)
                                       + yoff[:, 2 * j * p:(2 * j + 2) * p] * eac_x[:, lo:lo + 2 * p])
        upd = _dot(xw[:, g * kp:(g + 1) * kp].T.astype(BF16), bg)
        for hh in range(k_h):
            h = g * k_h + hh
            r0 = h * p
            dsc = jnp.broadcast_to(dec[:, h:h + 1], (p, n))
            st_ref[r0:r0 + p, :] = dsc * s_g[hh * p:(hh + 1) * p, :] + upd[hh * p:(hh + 1) * p, :]

    y = (y_scr[...] + dexp_ref[...] * xs) * _silu(z)
    ng = ng_ref[...]
    for g in range(g_n):
        yg = y[:, g * kp:(g + 1) * kp]
        ms_ = jnp.mean(yg * yg, axis=-1, keepdims=True)
        y_ref[:, g * kp:(g + 1) * kp] = (yg * lax.rsqrt(ms_ + EPS) * ng[:, g * kp:(g + 1) * kp]).astype(y_ref.dtype)


def _ssd_prompt(zx, dtraw, cw, cb, dtb, alog, dexp, ng, eh, nb, t, hp, n, g_n, k_h, p):
    d = hp
    L = min(SSD_CHUNK, t)
    nc = t // L
    cc = cw.shape[1]
    kern = functools.partial(_ssd_prompt_kernel, d=d, g_n=g_n, n=n, k_h=k_h, p=p)
    const = lambda shape: pl.BlockSpec(shape, lambda b, c: (0,) * len(shape))
    return pl.pallas_call(
        kern,
        grid=(nb, nc),
        in_specs=[
            pl.BlockSpec((L, d + cc), lambda b, c: (b * nc + c, 0)),
            pl.BlockSpec((L, LANES), lambda b, c: (b * nc + c, 0)),
            const(cw.shape), const(cb.shape), const(dtb.shape), const(alog.shape),
            const(dexp.shape), const(ng.shape), const(eh.shape),
        ],
        out_specs=[
            pl.BlockSpec((L, d), lambda b, c: (b * nc + c, 0)),
            pl.BlockSpec((None, hp, n), lambda b, c: (b, 0, 0)),
        ],
        out_shape=[
            jax.ShapeDtypeStruct((nb * t, d), BF16),
            jax.ShapeDtypeStruct((nb, hp, n), F32),
        ],
        scratch_shapes=[pltpu.VMEM((2 * SUBLANES, cc), F32), pltpu.VMEM((L, d), F32)],
        compiler_params=_cparams(("arbitrary", "arbitrary")),
        name="ssd_prompt",
    )(zx, dtraw, cw, cb, dtb, alog, dexp, ng, eh)


def _ssd_sample_pre_kernel(zx_ref, dt_ref, st_ref, cw_ref, cb_ref, dtb_ref, alog_ref, dexp_ref,
                           eh_ref, gh_ref, a_ref, ex_ref, zs_ref, xw_ref, c_ref, b_ref, dec_ref,
                           *, d, g_n, n):
    ts, bs, _ = zx_ref.shape
    kw = cw_ref.shape[0]
    gn = g_n * n
    rows = [st_ref[j] for j in range(kw - 1)]
    rows += [zx_ref[t][:, d:].astype(F32) for t in range(ts)]
    eh = eh_ref[...]
    gh = gh_ref[...]
    a = -jnp.exp(alog_ref[...])
    xs, bm, cm, dts, acs = [], [], [], [], []
    run = None
    for t in range(ts):
        conv = cb_ref[...] + cw_ref[0:1, :] * rows[t]
        for k in range(1, kw):
            conv = conv + cw_ref[k:k + 1, :] * rows[t + k]
        xbc = _silu(conv)
        xs.append(xbc[:, :d])
        bm.append(xbc[:, d:d + gn])
        cm.append(xbc[:, d + gn:])
        dtt = _softplus(dt_ref[t] + dtb_ref[...])
        dts.append(dtt)
        run = dtt * a if run is None else run + dtt * a
        acs.append(run)
    alast = acs[-1]
    dec_ref[...] = jnp.exp(alast)
    for t in range(ts):
        acc = dexp_ref[...] * xs[t]
        for j in range(t + 1):
            cbh = _dot_parts(_split2(cm[t] * bm[j]), gh)
            coef = cbh * dts[j]
            if j < t:
                coef = coef * jnp.exp(acs[t] - acs[j])
            acc = acc + _dot_parts(_split2(coef), eh) * xs[j]
        a_ref[t] = acc
        ex_ref[t] = _dot_parts(_split2(jnp.exp(acs[t])), eh)
        zs_ref[t] = _silu(zx_ref[t][:, :d].astype(F32))
        xw_ref[t] = xs[t] * _dot_parts(_split2(jnp.exp(alast - acs[t]) * dts[t]), eh)
        c_ref[t] = cm[t]
        b_ref[t] = bm[t]


def _ssd_sample_pre(zx3, dt3, tblk, ts, sconv_t, layer, cw, cb, dtb, alog, dexp, eh, gh, d, g_n, n):
    bs = zx3.shape[1]
    gn = g_n * n
    kern = functools.partial(_ssd_sample_pre_kernel, d=d, g_n=g_n, n=n)
    sb = min(SAMPLE_PRE_BLOCK, bs)
    kw = cw.shape[0]
    const = lambda a: pl.BlockSpec(a.shape, lambda i: (0,) * a.ndim)
    seq3 = lambda c: pl.BlockSpec((ts, sb, c), lambda i: (0, i, 0))
    src3 = lambda c: pl.BlockSpec((ts, sb, c), lambda i: (tblk, i, 0))
    big = jax.ShapeDtypeStruct((ts, bs, d), F32)
    small = jax.ShapeDtypeStruct((ts, bs, gn), F32)
    return pl.pallas_call(
        kern,
        grid=(bs // sb,),
        in_specs=[src3(zx3.shape[2]), src3(LANES),
                  pl.BlockSpec((None, kw - 1, sb, sconv_t.shape[3]), lambda i: (layer, 0, i, 0)),
                  const(cw), const(cb), const(dtb), const(alog), const(dexp), const(eh), const(gh)],
        out_specs=[seq3(d), seq3(d), seq3(d), seq3(d), seq3(gn), seq3(gn),
                   pl.BlockSpec((sb, LANES), lambda i: (i, 0))],
        out_shape=[big, big, big, big, small, small, jax.ShapeDtypeStruct((bs, LANES), F32)],
        compiler_params=_cparams(("arbitrary",)),
        name="ssd_sample_pre",
    )(zx3, dt3, sconv_t, cw, cb, dtb, alog, dexp, eh, gh)


def _ssd_sample_state_kernel(dec_ref, s0_ref, a_ref, ex_ref, zs_ref, xwt_ref, c_ref, b_ref, ng_ref,
                             prev_ref, y_ref, s1_ref, *, g_n, n, k_h, p, sb):
    del prev_ref
    i = pl.program_id(0)
    kp = k_h * p
    n_heads = g_n * k_h
    rows8 = sb * SUBLANES
    rid = lax.broadcasted_iota(jnp.int32, (rows8, 1), 0)
    ng = ng_ref[...]

    def body(s, carry):
        r0 = pl.multiple_of(s * SUBLANES, SUBLANES)
        in_s = (rid >= r0) & (rid < r0 + SUBLANES)
        for g in range(g_n):
            cg = c_ref[pl.ds(r0, SUBLANES), g * n:(g + 1) * n].astype(BF16)
            s_g = s0_ref[s, g * kp:(g + 1) * kp, :]
            yoff = _dot_nt(cg, s_g.astype(BF16))
            b_all = b_ref[:, g * n:(g + 1) * n]
            b_msk = jnp.where(in_s, b_all, 0.0).astype(BF16)
            upd = _dot(xwt_ref[g * kp:(g + 1) * kp, :].astype(BF16), b_msk)
            for hh in range(k_h):
                h = g * k_h + hh
                dsc = dec_ref[(i * sb + s) * n_heads + h]
                s1_ref[s, h * p:(h + 1) * p, :] = (dsc * s_g[hh * p:(hh + 1) * p, :]
                                                   + upd[hh * p:(hh + 1) * p, :])
            cols = slice(g * kp, (g + 1) * kp)
            yg = ((a_ref[pl.ds(r0, SUBLANES), cols] + ex_ref[pl.ds(r0, SUBLANES), cols] * yoff)
                  * zs_ref[pl.ds(r0, SUBLANES), cols])
            ms_ = jnp.mean(yg * yg, axis=-1, keepdims=True)
            y_ref[pl.ds(r0, SUBLANES), cols] = (yg * lax.rsqrt(ms_ + EPS) * ng[:, cols]).astype(y_ref.dtype)
        return carry

    lax.fori_loop(0, sb, body, 0)


def _ssd_sample_state(dec, s0_all, layer, s1_prev, a8, ex8, zs8, xwt, c8, b8, ng, g_n, n, k_h, p):
    depth, bs, hp, _ = s0_all.shape
    d = a8.shape[1]
    gn = g_n * n
    sb = min(SAMPLE_SEQ_BLOCK, bs)
    r8 = sb * SUBLANES
    kern = functools.partial(_ssd_sample_state_kernel, g_n=g_n, n=n, k_h=k_h, p=p, sb=sb)
    return pl.pallas_call(
        kern,
        grid=(bs // sb,),
        in_specs=[
            pl.BlockSpec(memory_space=pltpu.SMEM),
            pl.BlockSpec((None, sb, hp, n), lambda i: (layer, i, 0, 0)),
            pl.BlockSpec((r8, d), lambda i: (i, 0)),
            pl.BlockSpec((r8, d), lambda i: (i, 0)),
            pl.BlockSpec((r8, d), lambda i: (i, 0)),
            pl.BlockSpec((None, d, r8), lambda i: (i, 0, 0)),
            pl.BlockSpec((r8, gn), lambda i: (i, 0)),
            pl.BlockSpec((r8, gn), lambda i: (i, 0)),
            pl.BlockSpec((1, d), lambda i: (0, 0)),
            pl.BlockSpec(memory_space=pl.ANY),
        ],
        out_specs=[
            pl.BlockSpec((r8, d), lambda i: (i, 0)),
            pl.BlockSpec((None, sb, hp, n), lambda i: (layer, i, 0, 0)),
        ],
        out_shape=[
            jax.ShapeDtypeStruct((bs * SUBLANES, d), F32),
            jax.ShapeDtypeStruct((depth, bs, hp, n), F32),
        ],
        input_output_aliases={9: 1},
        compiler_params=_cparams(("arbitrary",)),
        name="ssd_sample_state",
    )(dec, s0_all, a8, ex8, zs8, xwt, c8, b8, ng, s1_prev)


def _ln_swish(v, g, b):
    mu = jnp.mean(v, axis=-1, keepdims=True)
    cen = v - mu
    var = jnp.mean(cen * cen, axis=-1, keepdims=True)
    return _silu(cen * lax.rsqrt(var + EPS) * g + b)


def _conf_prompt_kernel(u_ref, w_ref, b_ref, lg_ref, lb_ref, v_ref, upad, ush, pre, *, halo):
    c = pl.program_id(1)
    L, d = u_ref.shape
    kw = w_ref.shape[0]
    span = L + halo - SUBLANES

    @pl.when(c == 0)
    def _():
        upad[0:halo, :] = jnp.zeros((halo, d), F32)

    u_b = u_ref[...]
    u = u_b.astype(F32)
    upad[halo:halo + L, :] = u
    row = lax.broadcasted_iota(jnp.int32, (L, L), 0)
    col = lax.broadcasted_iota(jnp.int32, (L, L), 1)
    for r in range(1, SUBLANES):
        ush[r - 1, 0:halo, :] = upad[r:r + halo, :]
        shifted = _dot(jnp.where(col == row + r, 1.0, 0.0).astype(BF16), u_b)
        ush[r - 1, halo:span, :] = shifted[0:span - halo, :]
    for j in range(d // LANES):
        cols = slice(j * LANES, (j + 1) * LANES)
        acc = jnp.broadcast_to(b_ref[:, cols], (L, LANES))
        for k in range(kw):
            q, r = divmod(halo - (kw - 1) + k, SUBLANES)
            rows = slice(q * SUBLANES, q * SUBLANES + L)
            src = upad[rows, cols] if r == 0 else ush[r - 1, rows, cols]
            acc = acc + w_ref[k:k + 1, cols] * src
        pre[:, cols] = acc
    upad[0:halo, :] = u[L - halo:, :]
    v_ref[...] = _ln_swish(pre[...], lg_ref[...], lb_ref[...]).astype(v_ref.dtype)


def _conf_prompt(u, w, b, lg, lb, nb, t):
    assert u.dtype == BF16
    d = u.shape[1]
    kw = w.shape[0]
    L = min(CONF_CHUNK, t)
    nc = t // L
    halo = -(-(kw - 1) // SUBLANES) * SUBLANES
    const = lambda a: pl.BlockSpec(a.shape, lambda bb, c: (0,) * a.ndim)
    return pl.pallas_call(
        functools.partial(_conf_prompt_kernel, halo=halo),
        grid=(nb, nc),
        in_specs=[pl.BlockSpec((L, d), lambda bb, c: (bb * nc + c, 0)), const(w), const(b), const(lg), const(lb)],
        out_specs=pl.BlockSpec((L, d), lambda bb, c: (bb * nc + c, 0)),
        out_shape=jax.ShapeDtypeStruct((nb * t, d), BF16),
        scratch_shapes=[pltpu.VMEM((L + halo, d), F32), pltpu.VMEM((SUBLANES - 1, L + halo, d), F32),
                        pltpu.VMEM((L, d), F32)],
        compiler_params=_cparams(("arbitrary", "arbitrary")),
        name="conf_prompt",
    )(u, w, b, lg, lb)


def _conf_sample_kernel(u_ref, st_ref, w_ref, b_ref, lg_ref, lb_ref, v_ref):
    ts, sb, d = u_ref.shape
    kw = w_ref.shape[0]
    rows = [st_ref[j] for j in range(kw - 1)]
    rows += [u_ref[t].astype(F32) for t in range(ts)]
    for t in range(ts):
        acc = b_ref[...] + w_ref[0:1, :] * rows[t]
        for k in range(1, kw):
            acc = acc + w_ref[k:k + 1, :] * rows[t + k]
        v_ref[t] = _ln_swish(acc, lg_ref[...], lb_ref[...]).astype(v_ref.dtype)


def _conf_sample(u3, tblk, ts, cconv_t, layer, w, b, lg, lb):
    _, bs, d = u3.shape
    kw = w.shape[0]
    sb = min(CONF_SAMPLE_BLOCK, bs)
    const = lambda a: pl.BlockSpec(a.shape, lambda i: (0,) * a.ndim)
    return pl.pallas_call(
        _conf_sample_kernel,
        grid=(bs // sb,),
        in_specs=[
            pl.BlockSpec((ts, sb, d), lambda i: (tblk, i, 0)),
            pl.BlockSpec((None, kw - 1, sb, d), lambda i: (layer, 0, i, 0)),
            const(w), const(b), const(lg), const(lb),
        ],
        out_specs=pl.BlockSpec((ts, sb, d), lambda i: (0, i, 0)),
        out_shape=jax.ShapeDtypeStruct((ts, bs, d), BF16),
        compiler_params=_cparams(("arbitrary",)),
        name="conf_sample",
    )(u3, cconv_t, w, b, lg, lb)


def _merge_kernel(yp_ref, ys_ref, vp_ref, vs_ref, ws_ref, wc_ref, gs_ref, gc_ref, o_ref, *, npt):
    is_p = pl.program_id(0) < npt
    y = jnp.where(is_p, yp_ref[...], ys_ref[...])
    v = jnp.where(is_p, vp_ref[...], vs_ref[...])
    a = _dot(y, ws_ref[...])
    b = _dot(v, wc_ref[...])
    o_ref[...] = (gs_ref[...].astype(F32) * a + gc_ref[...].astype(F32) * b).astype(o_ref.dtype)


def _merge(y_p, y_s, v_p, v_s, ws, wc, layer, gates, tm):
    m, d = gates.shape[0], y_p.shape[1]
    npt = y_p.shape[0] // tm
    tn = _pick_tn(d)
    nj = d // tn
    p_spec = pl.BlockSpec((tm, d), lambda i, j: (jnp.minimum(i, npt - 1), 0))
    s_spec = pl.BlockSpec((tm, d), lambda i, j: (0, 0))
    return pl.pallas_call(
        functools.partial(_merge_kernel, npt=npt),
        grid=(m // tm, nj),
        in_specs=[
            p_spec, s_spec, p_spec, s_spec,
            pl.BlockSpec((None, d, tn), lambda i, j: (layer, 0, j)),
            pl.BlockSpec((None, d, tn), lambda i, j: (layer, 0, j)),
            pl.BlockSpec((tm, tn), lambda i, j: (i, j)),
            pl.BlockSpec((tm, tn), lambda i, j: (i, nj + j)),
        ],
        out_specs=pl.BlockSpec((tm, tn), lambda i, j: (i, j)),
        out_shape=jax.ShapeDtypeStruct((m, d), BF16),
        compiler_params=_cparams(("arbitrary", "arbitrary")),
        name="merge",
    )(y_p, y_s, v_p, v_s, ws, wc, gates, gates)


def _outproj_kernel(mx_ref, w_ref, x_ref, gp_ref, gs_ref, o_ref, *, npt, ts):
    gate = _tile_mod(pl.program_id(0), npt, ts, gp_ref, gs_ref)
    o_ref[...] = x_ref[...] + gate * _dot(mx_ref[...], w_ref[...])


def _outproj(mixed, wo, layer, x, modp, mods, sec, dims):
    m, d = x.shape
    tm, npt, tps, nb, bs, ts = dims
    tn = _pick_tn(d)
    nj = d // tn
    colj = lambda s, i, j: s * nj + j
    return pl.pallas_call(
        functools.partial(_outproj_kernel, npt=npt, ts=ts),
        grid=(m // tm, nj),
        in_specs=[
            pl.BlockSpec((tm, d), lambda i, j: (i, 0)),
            pl.BlockSpec((None, d, tn), lambda i, j: (layer, 0, j)),
            pl.BlockSpec((tm, tn), lambda i, j: (i, j)),
            *_mod_specs(npt, tps, nb, bs, tn, sec, colj),
        ],
        out_specs=pl.BlockSpec((tm, tn), lambda i, j: (i, j)),
        out_shape=jax.ShapeDtypeStruct((m, d), F32),
        compiler_params=_cparams(("arbitrary", "arbitrary")),
        name="outproj",
    )(mixed, wo, x, modp, mods)


def _pack_bf16_pairs(v):
    half = v.shape[1] // 2
    bits = pltpu.bitcast(v.astype(BF16).astype(F32), jnp.uint32)
    return bits[:, :half] | (bits[:, half:] >> 16)


def _unpack_bf16_pairs(w):
    left = pltpu.bitcast(w & jnp.uint32(0xFFFF0000), F32)
    right = pltpu.bitcast(w << 16, F32)
    return jnp.concatenate([left, right], axis=1)


INFO_RANK1, INFO_RANK2, INFO_E1, INFO_E2, INFO_W1, INFO_W2 = range(6)


def _router_kernel(x_ref, g_ref, scp_ref, scs_ref, shp_ref, shs_ref, whi_ref, wlo_ref, rb_ref,
                   hp_ref, info_ref, cnt_ref, *, npt, ts, n_e, n_g):
    i = pl.program_id(0)
    epg = n_e // n_g

    @pl.when(i == 0)
    def _():
        cnt_ref[...] = jnp.zeros(cnt_ref.shape, F32)

    x = x_ref[...]
    xn = x * lax.rsqrt(jnp.mean(x * x, axis=-1, keepdims=True) + EPS) * g_ref[...]
    sc = _tile_mod(i, npt, ts, scp_ref, scs_ref)
    sh = _tile_mod(i, npt, ts, shp_ref, shs_ref)
    h = xn * (1.0 + sc) + sh
    hi, lo = _split2(h)
    hp_ref[...] = _pack_bf16_pairs(h)
    logit = _dot(hi, whi_ref[...]) + _dot(lo, whi_ref[...]) + _dot(hi, wlo_ref[...]) + rb_ref[...]
    tm = logit.shape[0]
    lane = lax.broadcasted_iota(jnp.int32, (tm, LANES), 1).astype(F32)
    first = lambda hit: jnp.min(jnp.where(hit, lane, float(LANES)), axis=-1, keepdims=True)
    gl = jnp.where(lane >= n_e, jnp.where(lane < n_e + n_g, logit, NEG_BIG), NEG_BIG)
    gmax = jnp.max(gl, axis=-1, keepdims=True)
    g_top = 1.0 / jnp.sum(jnp.exp(gl - gmax), axis=-1, keepdims=True)
    g_idx = first(gl == gmax) - n_e
    el = jnp.where(lane >= g_idx * epg, jnp.where(lane < (g_idx + 1.0) * epg, logit, NEG_BIG), NEG_BIG)
    m1 = jnp.max(el, axis=-1, keepdims=True)
    i1 = first(el == m1)
    el2 = jnp.where(lane == i1, NEG_BIG, el)
    m2 = jnp.max(el2, axis=-1, keepdims=True)
    i2 = first(el2 == m2)
    r = jnp.exp(m2 - m1)
    w1 = 1.0 / (1.0 + r)
    w2 = r * w1
    oh1 = jnp.where(lane == i1, 1.0, 0.0)
    oh2 = jnp.where(lane == i2, 1.0, 0.0)
    both = oh1 + oh2
    row = lax.broadcasted_iota(jnp.int32, (tm, tm), 0)
    col = lax.broadcasted_iota(jnp.int32, (tm, tm), 1)
    before = _dot(jnp.where(row > col, 1.0, 0.0).astype(BF16), both.astype(BF16)) + cnt_ref[0:1, :]
    rank1 = jnp.sum(before * oh1, axis=-1, keepdims=True)
    rank2 = jnp.sum(before * oh2, axis=-1, keepdims=True)
    cnt_ref[...] = cnt_ref[...] + jnp.sum(both, axis=0, keepdims=True)
    rec = [rank1, rank2, i1, i2, g_top * w1, g_top * w2]
    info = jnp.zeros((tm, LANES), F32)
    for k, val in enumerate(rec):
        info = jnp.where(lane == float(k), val, info)
    info_ref[...] = info


def _router(x, gain, modp, mods, sec_sc, sec_sh, whi, wlo, rb, n_e, n_g, dims):
    m, d = x.shape
    tm, npt, tps, nb, bs, ts = dims
    col0 = lambda sec, i: sec
    return pl.pallas_call(
        functools.partial(_router_kernel, npt=npt, ts=ts, n_e=n_e, n_g=n_g),
        grid=(m // tm,),
        in_specs=[
            pl.BlockSpec((tm, d), lambda i: (i, 0)),
            pl.BlockSpec((1, d), lambda i: (0, 0)),
            *_mod_specs(npt, tps, nb, bs, d, sec_sc, col0),
            *_mod_specs(npt, tps, nb, bs, d, sec_sh, col0),
            pl.BlockSpec((d, LANES), lambda i: (0, 0)),
            pl.BlockSpec((d, LANES), lambda i: (0, 0)),
            pl.BlockSpec((1, LANES), lambda i: (0, 0)),
        ],
        out_specs=[pl.BlockSpec((tm, d // 2), lambda i: (i, 0)), pl.BlockSpec((tm, LANES), lambda i: (i, 0)),
                   pl.BlockSpec((SUBLANES, LANES), lambda i: (0, 0))],
        out_shape=[jax.ShapeDtypeStruct((m, d // 2), jnp.uint32), jax.ShapeDtypeStruct((m, LANES), F32),
                   jax.ShapeDtypeStruct((SUBLANES, LANES), F32)],
        compiler_params=_cparams(("arbitrary",)),
        name="router",
    )(x, gain, modp, mods, modp, mods, whi, wlo, rb)


def _slots_kernel(info_ref, offs_ref, pos_ref):
    info = info_ref[...]
    lane = lax.broadcasted_iota(jnp.int32, info.shape, 1).astype(F32)
    offs = offs_ref[...]
    base1 = jnp.sum(jnp.where(lane == info[:, INFO_E1:INFO_E1 + 1], offs, 0.0), axis=-1, keepdims=True)
    base2 = jnp.sum(jnp.where(lane == info[:, INFO_E2:INFO_E2 + 1], offs, 0.0), axis=-1, keepdims=True)
    rec = jnp.where(lane == 0.0, info[:, INFO_RANK1:INFO_RANK1 + 1] + base1,
                    jnp.where(lane == 1.0, info[:, INFO_RANK2:INFO_RANK2 + 1] + base2, 0.0))
    pos_ref[...] = rec.T[0:SUBLANES, :].astype(jnp.int32)


def _slots(info, offs_row, tm):
    m = info.shape[0]
    return pl.pallas_call(
        _slots_kernel,
        grid=(m // tm,),
        in_specs=[pl.BlockSpec((tm, LANES), lambda i: (i, 0)), pl.BlockSpec((1, LANES), lambda i: (0, 0))],
        out_specs=pl.BlockSpec((SUBLANES, tm), lambda i: (0, i)),
        out_shape=jax.ShapeDtypeStruct((SUBLANES, m), jnp.int32),
        compiler_params=_cparams(("arbitrary",)),
        name="moe_slots",
    )(info, offs_row)


def _moe_plan(info, cnt, n_e, tg, nt_max, tm):
    counts = cnt[0, :n_e].astype(jnp.int32)
    tiles = (counts + tg - 1) // tg
    tile_end = jnp.cumsum(tiles)
    offs = (tile_end - tiles) * tg
    n_used = tile_end[n_e - 1]
    offs_row = jnp.zeros((1, LANES), F32).at[0, :n_e].set(offs.astype(F32))
    pos = _slots(info, offs_row, tm)
    pos1, pos2 = pos[0], pos[1]
    tile = jnp.minimum(jnp.arange(nt_max, dtype=jnp.int32), n_used - 1)
    tile_expert = jnp.searchsorted(tile_end, tile, side="right").astype(jnp.int32)
    return pos1, pos2, tile_expert, n_used.reshape(1)


def _dispatch_kernel(pos1_ref, pos2_ref, hp_ref, xs0_ref, xs_ref, sem):
    del xs0_ref
    tm = hp_ref.shape[0]
    base = pl.program_id(0) * tm

    def issue(r, carry):
        src = hp_ref.at[pl.ds(r, 1)]
        pltpu.make_async_copy(src, xs_ref.at[pl.ds(pos1_ref[base + r], 1)], sem).start(priority=0)
        pltpu.make_async_copy(src, xs_ref.at[pl.ds(pos2_ref[base + r], 1)], sem).start(priority=1)
        return carry
    lax.fori_loop(0, tm, issue, 0, unroll=ROW_DMA_UNROLL)

    def drain(r, carry):
        for _ in range(2):
            pltpu.make_async_copy(hp_ref.at[pl.ds(0, 1)], xs_ref.at[pl.ds(0, 1)], sem).wait()
        return carry
    lax.fori_loop(0, tm, drain, 0, unroll=ROW_DMA_UNROLL)


def _dispatch(pos1, pos2, hp, n_slots, tm):
    m, half = hp.shape
    smem = pl.BlockSpec(memory_space=pltpu.SMEM)
    hbm = pl.BlockSpec(memory_space=pl.ANY)
    return pl.pallas_call(
        _dispatch_kernel,
        grid=(m // tm,),
        in_specs=[smem, smem, pl.BlockSpec((tm, half), lambda i: (i, 0)), hbm],
        out_specs=hbm,
        out_shape=jax.ShapeDtypeStruct((n_slots, half), jnp.uint32),
        scratch_shapes=[pltpu.SemaphoreType.DMA(())],
        input_output_aliases={3: 0},
        compiler_params=_cparams(("arbitrary",)),
        name="moe_dispatch",
    )(pos1, pos2, hp, jnp.zeros((n_slots, half), jnp.uint32))


def _experts_kernel(te_ref, nu_ref, xs_ref, wg_ref, wu_ref, wd_ref, ys_ref):
    del te_ref
    i = pl.program_id(0)

    @pl.when(i < nu_ref[0])
    def _():
        x = _unpack_bf16_pairs(xs_ref[...]).astype(BF16)
        hid = _silu(_dot(x, wg_ref[...].astype(BF16))) * _dot(x, wu_ref[...].astype(BF16))
        ys_ref[...] = _pack_bf16_pairs(_dot(hid.astype(BF16), wd_ref[...].astype(BF16)))

    @pl.when(i >= nu_ref[0])
    def _():
        ys_ref[...] = jnp.zeros(ys_ref.shape, jnp.uint32)


def _experts(tile_expert, n_used, xs, wg, wu, wd, layer, tg):
    n_slots, half = xs.shape
    _, _, d, f = wg.shape
    grid_spec = pltpu.PrefetchScalarGridSpec(
        num_scalar_prefetch=2,
        grid=(n_slots // tg,),
        in_specs=[
            pl.BlockSpec((tg, half), lambda i, te, nu: (i, 0)),
            pl.BlockSpec((None, None, d, f), lambda i, te, nu: (layer, te[i], 0, 0)),
            pl.BlockSpec((None, None, d, f), lambda i, te, nu: (layer, te[i], 0, 0)),
            pl.BlockSpec((None, None, f, d), lambda i, te, nu: (layer, te[i], 0, 0)),
        ],
        out_specs=pl.BlockSpec((tg, half), lambda i, te, nu: (i, 0)),
    )
    return pl.pallas_call(
        _experts_kernel,
        grid_spec=grid_spec,
        out_shape=jax.ShapeDtypeStruct((n_slots, half), jnp.uint32),
        compiler_params=_cparams(("arbitrary",)),
        name="moe_experts",
    )(tile_expert, n_used, xs, wg, wu, wd)


def _combine_kernel(pos1_ref, pos2_ref, info_ref, x_ref, gp_ref, gs_ref, ys_ref, o_ref, buf_a, buf_b, sem,
                    *, npt, ts):
    i = pl.program_id(0)
    tm = x_ref.shape[0]
    slot = lax.rem(i, 2)

    def issue(tile, s):
        def body(r, carry):
            t = tile * tm + r
            pltpu.make_async_copy(ys_ref.at[pl.ds(pos1_ref[t], 1)], buf_a.at[s, pl.ds(r, 1)], sem.at[s]).start(priority=0)
            pltpu.make_async_copy(ys_ref.at[pl.ds(pos2_ref[t], 1)], buf_b.at[s, pl.ds(r, 1)], sem.at[s]).start(priority=1)
            return carry
        lax.fori_loop(0, tm, body, 0, unroll=ROW_DMA_UNROLL)

    @pl.when(i == 0)
    def _():
        issue(0, 0)

    @pl.when(i + 1 < pl.num_programs(0))
    def _():
        issue(i + 1, 1 - slot)

    def wait_body(r, carry):
        pltpu.make_async_copy(ys_ref.at[pl.ds(0, 1)], buf_a.at[slot, pl.ds(0, 1)], sem.at[slot]).wait()
        pltpu.make_async_copy(ys_ref.at[pl.ds(0, 1)], buf_b.at[slot, pl.ds(0, 1)], sem.at[slot]).wait()
        return carry
    lax.fori_loop(0, tm, wait_body, 0, unroll=ROW_DMA_UNROLL)

    info = info_ref[...]
    w1 = info[:, INFO_W1:INFO_W1 + 1]
    w2 = info[:, INFO_W2:INFO_W2 + 1]
    moe = w1 * _unpack_bf16_pairs(buf_a[slot]) + w2 * _unpack_bf16_pairs(buf_b[slot])
    o_ref[...] = x_ref[...] + _tile_mod(i, npt, ts, gp_ref, gs_ref) * moe


def _combine(pos1, pos2, info, x, modp, mods, sec, ys, dims):
    m, d = x.shape
    tm, npt, tps, nb, bs, ts = dims
    col0 = lambda s, i: s
    smem = pl.BlockSpec(memory_space=pltpu.SMEM)
    return pl.pallas_call(
        functools.partial(_combine_kernel, npt=npt, ts=ts),
        grid=(m // tm,),
        in_specs=[
            smem, smem,
            pl.BlockSpec((tm, LANES), lambda i: (i, 0)),
            pl.BlockSpec((tm, d), lambda i: (i, 0)),
            *_mod_specs(npt, tps, nb, bs, d, sec, col0),
            pl.BlockSpec(memory_space=pl.ANY),
        ],
        out_specs=pl.BlockSpec((tm, d), lambda i: (i, 0)),
        out_shape=jax.ShapeDtypeStruct((m, d), F32),
        scratch_shapes=[pltpu.VMEM((2, tm, d // 2), jnp.uint32), pltpu.VMEM((2, tm, d // 2), jnp.uint32),
                        pltpu.SemaphoreType.DMA((2,))],
        compiler_params=_cparams(("arbitrary",)),
        name="moe_combine",
    )(pos1, pos2, info, x, modp, mods, ys)


def _moe(hp, info, cnt, wg, wu, wd, layer, x, modp, mods, sec, dims):
    m = x.shape[0]
    n_e = wg.shape[1]
    tg = MOE_TG
    nt_max = -(-2 * m // tg) + n_e
    pos1, pos2, tile_expert, n_used = _moe_plan(info, cnt, n_e, tg, nt_max, dims[0])
    xs = _dispatch(pos1, pos2, hp, nt_max * tg, dims[0])
    ys = _experts(tile_expert, n_used, xs, wg, wu, wd, layer, tg)
    return _combine(pos1, pos2, info, x, modp, mods, sec, ys, dims)


def _final_norm_kernel(x_ref, g_ref, o_ref):
    x = x_ref[...]
    o_ref[...] = x * lax.rsqrt(jnp.mean(x * x, axis=-1, keepdims=True) + EPS) * g_ref[...]


def _final_norm(x, g, tm, tile0, ntiles):
    d = x.shape[1]
    return pl.pallas_call(
        _final_norm_kernel,
        grid=(ntiles,),
        in_specs=[pl.BlockSpec((tm, d), lambda i: (tile0 + i, 0)), pl.BlockSpec((1, d), lambda i: (0, 0))],
        out_specs=pl.BlockSpec((tm, d), lambda i: (i, 0)),
        out_shape=jax.ShapeDtypeStruct((ntiles * tm, d), F32),
        compiler_params=_cparams(("arbitrary",)),
        name="final_norm",
    )(x, g)


def _to_seq8(a):
    ts, bs, c = a.shape
    a = jnp.transpose(a, (1, 0, 2))
    a = jnp.pad(a, ((0, 0), (0, SUBLANES - ts), (0, 0)))
    return a.reshape(bs * SUBLANES, c)


def kernel(x_prompt, x_sample, c_prompt, c_sample, state_ssm, state_ssd_conv, state_conf_conv, norm_mix_g, norm_ffn_g, w_ada, b_ada, w_in, ssd_conv_w, ssd_conv_b, ssd_dt_bias, ssd_a_log, ssd_d, ssd_norm_g, w_ssd_out, conf_conv_w, conf_conv_b, conf_ln_g, conf_ln_b, w_conf_out, w_o, w_group_router, b_group_router, w_expert_router, b_expert_router, w_exp_gate, w_exp_up, w_exp_down, final_norm_g):
    nb, t, d = x_prompt.shape
    bs, ts, _ = x_sample.shape
    depth = w_in.shape[0]
    n_h = ssd_dt_bias.shape[1]
    p = d // n_h
    n = state_ssm.shape[-1]
    cc = ssd_conv_w.shape[-1]
    g_n = (cc - d) // (2 * n)
    k_h = n_h // g_n
    gn = g_n * n
    kc = conf_conv_w.shape[1]
    n_e = w_exp_gate.shape[1]
    n_g = w_group_router.shape[-1]
    tm = bs * ts
    assert t % tm == 0 and tm % SUBLANES == 0 and bs % SUBLANES == 0 and ts <= SUBLANES
    assert 2 * p == LANES and k_h % 2 == 0 and n_h <= LANES and n_e + n_g <= LANES
    npt = nb * t // tm
    dims = (tm, npt, t // tm, nb, bs, ts)
    mp = nb * t

    x = jnp.concatenate([x_prompt.reshape(mp, d), jnp.transpose(x_sample, (1, 0, 2)).reshape(tm, d)], axis=0)

    rc = -(-(nb + bs) // SUBLANES) * SUBLANES
    c_all = jnp.concatenate([c_prompt, c_sample, jnp.zeros((rc - nb - bs, d), F32)], axis=0)
    mod = _adaln(c_all, w_ada, b_ada)

    head_of = jnp.arange(d) // p
    eh = (jnp.arange(LANES)[:, None] == head_of[None, :]).astype(BF16)
    gh = ((jnp.arange(gn)[:, None] // n) == (jnp.arange(LANES)[None, :] // k_h)).astype(BF16)
    gh = gh * (jnp.arange(LANES)[None, :] < n_h).astype(BF16)
    pad_h = lambda v: jnp.pad(v, (0, LANES - n_h)).reshape(1, LANES)

    m = mp + tm
    tblk = mp // bs // ts
    assert tblk * bs * ts == mp
    sconv_t = jnp.transpose(state_ssd_conv, (0, 2, 1, 3))
    cconv_t = jnp.transpose(state_conf_conv, (0, 2, 1, 3))
    ssm0_all = state_ssm.reshape(depth, bs, n_h * p, n)
    ssm_s = jnp.zeros(ssm0_all.shape, F32)
    o_dt, o_glu = d + cc, d + cc + n_h
    w_main = w_in[:, :, :o_dt].astype(BF16)
    w_dt = jnp.pad(w_in[:, :, o_dt:o_glu], ((0, 0), (0, 0), (0, LANES - n_h))).astype(BF16)
    w_rest = w_in[:, :, o_glu:].astype(BF16)
    w_so, w_co, w_oo = w_ssd_out.astype(BF16), w_conf_out.astype(BF16), w_o.astype(BF16)
    ssm_p, sconv_p, cconv_p, sconv_s, cconv_s = [], [], [], [], []
    for l in range(depth):
        modp = mod[l, :nb].reshape(nb, 1, 6 * d)
        mods = mod[l, nb:nb + bs]
        h1 = _modnorm(x, norm_mix_g[l].reshape(1, d), modp, mods, 1, 0, dims)
        zx = _proj(h1, w_main, [0], o_dt, l, BF16, "plain")
        dtraw = _proj(h1, w_dt, [0], LANES, l, F32, "plain")
        u = _proj(h1, w_rest, [0, d], d, l, BF16, "glu")
        gates = _proj(h1, w_rest, [2 * d], 2 * d, l, BF16, "sigmoid")

        cw = ssd_conv_w[l]
        cb = ssd_conv_b[l].reshape(1, cc)
        dtb = pad_h(ssd_dt_bias[l])
        alog = pad_h(ssd_a_log[l])
        dexp = jnp.repeat(ssd_d[l], p).reshape(1, d)
        ng = ssd_norm_g[l].reshape(1, d)

        y_p, s_p = _ssd_prompt(zx, dtraw, cw, cb, dtb, alog, dexp, ng, eh, nb, t, n_h * p, n, g_n, k_h, p)
        zx_s = zx[mp:].reshape(ts, bs, d + cc)
        sconv0 = state_ssd_conv[l]
        a_t, ex_t, zs_t, xw_t, c_t, b_t, dec = _ssd_sample_pre(
            zx.reshape(m // bs, bs, d + cc), dtraw.reshape(m // bs, bs, LANES), tblk, ts, sconv_t, l,
            cw, cb, dtb, alog, dexp, eh, gh, d, g_n, n)
        sb = min(SAMPLE_SEQ_BLOCK, bs)
        xwt = _to_seq8(xw_t).reshape(bs // sb, sb * SUBLANES, d).transpose(0, 2, 1)
        dec = dec[:, :n_h].reshape(bs * n_h)
        y_s8, ssm_s = _ssd_sample_state(dec, ssm0_all, l, ssm_s, _to_seq8(a_t), _to_seq8(ex_t),
                                        _to_seq8(zs_t), xwt, _to_seq8(c_t), _to_seq8(b_t), ng, g_n, n, k_h, p)
        y_s = jnp.transpose(y_s8.reshape(bs, SUBLANES, d)[:, :ts], (1, 0, 2)).reshape(tm, d).astype(BF16)

        ccw = conf_conv_w[l]
        ccb = conf_conv_b[l].reshape(1, d)
        lg = conf_ln_g[l].reshape(1, d)
        lb = conf_ln_b[l].reshape(1, d)
        v_p = _conf_prompt(u, ccw, ccb, lg, lb, nb, t)
        u_s = u[mp:].reshape(ts, bs, d)
        cconv0 = state_conf_conv[l]
        v_s = _conf_sample(u.reshape(m // bs, bs, d), tblk, ts, cconv_t, l, ccw, ccb, lg, lb).reshape(tm, d)

        mixed = _merge(y_p, y_s, v_p, v_s, w_so, w_co, l, gates, tm)
        x = _outproj(mixed, w_oo, l, x, modp, mods, 2, dims)

        w_r = jnp.concatenate([w_expert_router[l], w_group_router[l],
                               jnp.zeros((d, LANES - n_e - n_g), F32)], axis=1)
        w_r_hi = w_r.astype(BF16)
        w_r_lo = (w_r - w_r_hi.astype(F32)).astype(BF16)
        rb = jnp.concatenate([b_expert_router[l], b_group_router[l], jnp.zeros((LANES - n_e - n_g,), F32)]).reshape(1, LANES)
        hp, info, cnt = _router(x, norm_ffn_g[l].reshape(1, d), modp, mods, 4, 3, w_r_hi, w_r_lo, rb, n_e, n_g, dims)
        x = _moe(hp, info, cnt, w_exp_gate, w_exp_up, w_exp_down, l, x, modp, mods, 5, dims)

        ssm_p.append(s_p.reshape(nb, n_h, p, n))
        k3 = sconv0.shape[1]
        sconv_p.append(jnp.stack([zx[(b + 1) * t - k3:(b + 1) * t, d:] for b in range(nb)]).astype(F32))
        cconv_p.append(jnp.stack([u[(b + 1) * t - (kc - 1):(b + 1) * t] for b in range(nb)]).astype(F32))
        xraw_s = jnp.transpose(zx_s[:, :, d:], (1, 0, 2)).astype(F32)
        sconv_s.append(jnp.concatenate([sconv0, xraw_s], axis=1)[:, ts:])
        cconv_s.append(jnp.concatenate([cconv0, jnp.transpose(u_s, (1, 0, 2)).astype(F32)], axis=1)[:, ts:])

    fg = final_norm_g.reshape(1, d)
    y_prompt = _final_norm(x, fg, tm, 0, npt).reshape(nb, t, d)
    y_sample = jnp.transpose(_final_norm(x, fg, tm, npt, 1).reshape(ts, bs, d), (1, 0, 2))
    return (y_prompt, y_sample, jnp.stack(ssm_p), jnp.stack(sconv_p), jnp.stack(cconv_p),
            ssm_s.reshape(depth, bs, n_h, p, n), jnp.stack(sconv_s), jnp.stack(cconv_s))
```

```python
import functools

import jax
import jax.numpy as jnp
from jax import lax
from jax.experimental import pallas as pl
from jax.experimental.pallas import tpu as pltpu

F32 = jnp.float32
BF16 = jnp.bfloat16
EPS = 1e-6
LANES = 128
SUBLANES = 8
VMEM_LIMIT = 52 * 1024 * 1024
NEG_BIG = -1e30
SSD_CHUNK = 128
CONF_CHUNK = 256
SAMPLE_SEQ_BLOCK = 8
SAMPLE_PRE_BLOCK = 32
CONF_SAMPLE_BLOCK = 16
MOE_TG = 256
ROW_DMA_UNROLL = 8
ADA_TN = 1024
PROJ_TN = 1024
PROJ_TM = 1088


def _pick_tn(ncols):
    assert ncols % LANES == 0
    return max(tn for tn in range(LANES, min(PROJ_TN, ncols) + 1, LANES) if ncols % tn == 0)


def _pick_rows(m):
    return max(tm for tm in range(2 * SUBLANES, min(PROJ_TM, m) + 1, 2 * SUBLANES) if m % tm == 0)


def _cparams(sem):
    return pltpu.CompilerParams(dimension_semantics=sem, vmem_limit_bytes=VMEM_LIMIT)


def _silu(x):
    return x * jax.nn.sigmoid(x)


def _softplus(x):
    return jnp.maximum(x, 0.0) + jnp.log1p(jnp.exp(-jnp.abs(x)))


def _split2(x):
    hi = x.astype(BF16)
    lo = (x - hi.astype(F32)).astype(BF16)
    return hi, lo


def _split3(x):
    hi = x.astype(BF16)
    r = x - hi.astype(F32)
    mid = r.astype(BF16)
    lo = (r - mid.astype(F32)).astype(BF16)
    return hi, mid, lo


def _dot(a, b):
    return jnp.dot(a, b, preferred_element_type=F32)


def _dot_nt(a, b):
    return lax.dot_general(a, b, (((1,), (1,)), ((), ())), preferred_element_type=F32)


def _dot_parts(parts, w):
    acc = _dot(parts[0], w)
    for p in parts[1:]:
        acc = acc + _dot(p, w)
    return acc


def _adaln_kernel(c_ref, w_ref, b_ref, o_ref):
    s = _silu(c_ref[...]).astype(BF16)
    o_ref[...] = _dot(s, w_ref[...].astype(BF16)) + b_ref[...]


def _adaln(c_all, w_ada, b_ada):
    depth, d, n6 = w_ada.shape
    rc = c_all.shape[0]
    tn = min(ADA_TN, n6)
    return pl.pallas_call(
        _adaln_kernel,
        grid=(depth, n6 // tn),
        in_specs=[
            pl.BlockSpec((rc, d), lambda l, j: (0, 0)),
            pl.BlockSpec((None, d, tn), lambda l, j: (l, 0, j)),
            pl.BlockSpec((None, 1, tn), lambda l, j: (l, 0, j)),
        ],
        out_specs=pl.BlockSpec((None, rc, tn), lambda l, j: (l, 0, j)),
        out_shape=jax.ShapeDtypeStruct((depth, rc, n6), F32),
        compiler_params=_cparams(("arbitrary", "arbitrary")),
        name="adaln",
    )(c_all, w_ada, b_ada.reshape(depth, 1, n6))


def _tile_mod(i, npt, ts, p_ref, s_ref):
    ms = s_ref[...]
    tiled = jnp.concatenate([ms] * ts, axis=0)
    return jnp.where(i < npt, p_ref[...], tiled)


def _mod_specs(npt, tiles_per_seq, nseq_p, bs, width, sec, col_of, tile_axis=0):
    def p_map(*idx):
        i = idx[tile_axis]
        return (jnp.minimum(i // tiles_per_seq, nseq_p - 1), 0, col_of(sec, *idx))

    def s_map(*idx):
        return (0, col_of(sec, *idx))

    return [pl.BlockSpec((None, 1, width), p_map), pl.BlockSpec((bs, width), s_map)]


def _modnorm_kernel(x_ref, g_ref, scp_ref, scs_ref, shp_ref, shs_ref, h_ref, *, npt, ts):
    i = pl.program_id(0)
    x = x_ref[...]
    xn = x * lax.rsqrt(jnp.mean(x * x, axis=-1, keepdims=True) + EPS) * g_ref[...]
    sc = _tile_mod(i, npt, ts, scp_ref, scs_ref)
    sh = _tile_mod(i, npt, ts, shp_ref, shs_ref)
    h_ref[...] = (xn * (1.0 + sc) + sh).astype(h_ref.dtype)


def _modnorm(x, gain, modp, mods, sec_sc, sec_sh, dims):
    m, d = x.shape
    tm, npt, tps, nb, bs, ts = dims
    col0 = lambda sec, i: sec
    return pl.pallas_call(
        functools.partial(_modnorm_kernel, npt=npt, ts=ts),
        grid=(m // tm,),
        in_specs=[
            pl.BlockSpec((tm, d), lambda i: (i, 0)),
            pl.BlockSpec((1, d), lambda i: (0, 0)),
            *_mod_specs(npt, tps, nb, bs, d, sec_sc, col0),
            *_mod_specs(npt, tps, nb, bs, d, sec_sh, col0),
        ],
        out_specs=pl.BlockSpec((tm, d), lambda i: (i, 0)),
        out_shape=jax.ShapeDtypeStruct((m, d), BF16),
        compiler_params=_cparams(("arbitrary",)),
        name="modnorm",
    )(x, gain, modp, mods, modp, mods)


def _proj_kernel(h_ref, *rest, nw, epilogue):
    w_refs = rest[:nw]
    o_ref = rest[nw]
    h = h_ref[...]
    acc = _dot(h, w_refs[0][...])
    if epilogue == "glu":
        acc = acc * jax.nn.sigmoid(_dot(h, w_refs[1][...]))
    elif epilogue == "sigmoid":
        acc = jax.nn.sigmoid(acc)
    o_ref[...] = acc.astype(o_ref.dtype)


def _proj(h, w, col0s, ncols, layer, out_dtype, epilogue):
    m, d = h.shape
    tn = _pick_tn(ncols)
    tm = _pick_rows(m)
    assert all(c0 % tn == 0 for c0 in col0s)
    w_specs = [pl.BlockSpec((None, d, tn), lambda j, i, off=c0 // tn: (layer, 0, off + j)) for c0 in col0s]
    return pl.pallas_call(
        functools.partial(_proj_kernel, nw=len(col0s), epilogue=epilogue),
        grid=(ncols // tn, m // tm),
        in_specs=[pl.BlockSpec((tm, d), lambda j, i: (i, 0))] + w_specs,
        out_specs=pl.BlockSpec((tm, tn), lambda j, i: (i, j)),
        out_shape=jax.ShapeDtypeStruct((m, ncols), out_dtype),
        compiler_params=_cparams(("arbitrary", "arbitrary")),
        name="proj_" + epilogue,
    )(h, *([w] * len(col0s)))


def _ssd_prompt_kernel(zx_ref, dt_ref, cw_ref, cb_ref, dtb_ref, alog_ref, dexp_ref, ng_ref, eh_ref,
                       y_ref, st_ref, xpad, y_scr, *, d, g_n, n, k_h, p):
    c = pl.program_id(1)
    L = zx_ref.shape[0]
    kp = k_h * p
    cc = cw_ref.shape[1]
    kw = cw_ref.shape[0]

    @pl.when(c == 0)
    def _():
        xpad[0:SUBLANES, :] = jnp.zeros((SUBLANES, cc), F32)
        st_ref[...] = jnp.zeros(st_ref.shape, F32)

    zx = zx_ref[...]
    z = zx[:, :d].astype(F32)
    xr_b = zx[:, d:]
    xr = xr_b.astype(F32)
    xpad[SUBLANES:2 * SUBLANES, :] = xr[0:SUBLANES, :]
    row = lax.broadcasted_iota(jnp.int32, (L, L), 0)
    col = lax.broadcasted_iota(jnp.int32, (L, L), 1)
    conv = cb_ref[...] + cw_ref[kw - 1:kw, :] * xr
    head = cb_ref[...] + cw_ref[kw - 1:kw, :] * xr[0:SUBLANES, :]
    for s in range(1, kw):
        back = _dot(jnp.where(col == row - s, 1.0, 0.0).astype(BF16), xr_b)
        conv = conv + cw_ref[kw - 1 - s:kw - s, :] * back
        head = head + cw_ref[kw - 1 - s:kw - s, :] * xpad[SUBLANES - s:2 * SUBLANES - s, :]
    conv = jnp.concatenate([head, conv[SUBLANES:, :]], axis=0)
    xpad[0:SUBLANES, :] = xr[L - SUBLANES:, :]
    xbc = _silu(conv)
    xs = xbc[:, :d]
    bm = xbc[:, d:d + g_n * n]
    cm = xbc[:, d + g_n * n:]

    dt = _softplus(dt_ref[...] + dtb_ref[...])
    a = -jnp.exp(alog_ref[...])
    dta = dt * a
    row = lax.broadcasted_iota(jnp.int32, (L, L), 0)
    col = lax.broadcasted_iota(jnp.int32, (L, L), 1)
    causal = row >= col
    tril = jnp.where(causal, 1.0, 0.0).astype(BF16)
    a_hi, a_mid, a_lo = _split3(dta)
    acum = _dot(tril, a_hi) + _dot(tril, a_mid) + _dot(tril, a_lo)
    acum_t = acum.T
    dt_t = dt.T
    alast = acum[L - 1:L, :]
    wend = jnp.exp(alast - acum) * dt
    eac = jnp.exp(acum)
    eh = eh_ref[...]
    wend_x = _dot_parts(_split2(wend), eh)
    eac_x = _dot_parts(_split2(eac), eh)
    xw = xs * wend_x
    dec = jnp.exp(alast)
    lane = lax.broadcasted_iota(jnp.int32, (L, 2 * p), 1)

    for g in range(g_n):
        bg = bm[:, g * n:(g + 1) * n].astype(BF16)
        cg = cm[:, g * n:(g + 1) * n].astype(BF16)
        cbm = _dot_nt(cg, bg)
        s_g = st_ref[g * kp:(g + 1) * kp, :]
        yoff = _dot_nt(cg, s_g.astype(BF16))
        for j in range(k_h // 2):
            h0 = g * k_h + 2 * j
            ms = []
            for h in (h0, h0 + 1):
                seg = acum[:, h:h + 1] - acum_t[h:h + 1, :]
                dcy = jnp.exp(jnp.where(causal, seg, NEG_BIG))
                ms.append((cbm * dcy * dt_t[h:h + 1, :]).astype(BF16))
            lhs = jnp.concatenate(ms, axis=1)
            xp = xs[:, h0 * p:(h0 + 2) * p]
            rhs = jnp.concatenate([jnp.where(lane < p, xp, 0.0), jnp.where(lane >= p, xp, 0.0)],
                                  axis=0).astype(BF16)
            lo = h0 * p
            y_scr[:, lo:lo + 2 * p] = (_dot(lhs, rhs)
                                       + yoff[:, 2 * j * p:(2 * j + 2) * p] * eac_x[:, lo:lo + 2 * p])
        upd = _dot(xw[:, g * kp:(g + 1) * kp].T.astype(BF16), bg)
        for hh in range(k_h):
            h = g * k_h + hh
            r0 = h * p
            dsc = jnp.broadcast_to(dec[:, h:h + 1], (p, n))
            st_ref[r0:r0 + p, :] = dsc * s_g[hh * p:(hh + 1) * p, :] + upd[hh * p:(hh + 1) * p, :]

    y = (y_scr[...] + dexp_ref[...] * xs) * _silu(z)
    ng = ng_ref[...]
    for g in range(g_n):
        yg = y[:, g * kp:(g + 1) * kp]
        ms_ = jnp.mean(yg * yg, axis=-1, keepdims=True)
        y_ref[:, g * kp:(g + 1) * kp] = (yg * lax.rsqrt(ms_ + EPS) * ng[:, g * kp:(g + 1) * kp]).astype(y_ref.dtype)


def _ssd_prompt(zx, dtraw, cw, cb, dtb, alog, dexp, ng, eh, nb, t, hp, n, g_n, k_h, p):
    d = hp
    L = min(SSD_CHUNK, t)
    nc = t // L
    cc = cw.shape[1]
    kern = functools.partial(_ssd_prompt_kernel, d=d, g_n=g_n, n=n, k_h=k_h, p=p)
    const = lambda shape: pl.BlockSpec(shape, lambda b, c: (0,) * len(shape))
    return pl.pallas_call(
        kern,
        grid=(nb, nc),
        in_specs=[
            pl.BlockSpec((L, d + cc), lambda b, c: (b * nc + c, 0)),
            pl.BlockSpec((L, LANES), lambda b, c: (b * nc + c, 0)),
            const(cw.shape), const(cb.shape), const(dtb.shape), const(alog.shape),
            const(dexp.shape), const(ng.shape), const(eh.shape),
        ],
        out_specs=[
            pl.BlockSpec((L, d), lambda b, c: (b * nc + c, 0)),
            pl.BlockSpec((None, hp, n), lambda b, c: (b, 0, 0)),
        ],
        out_shape=[
            jax.ShapeDtypeStruct((nb * t, d), BF16),
            jax.ShapeDtypeStruct((nb, hp, n), F32),
        ],
        scratch_shapes=[pltpu.VMEM((2 * SUBLANES, cc), F32), pltpu.VMEM((L, d), F32)],
        compiler_params=_cparams(("arbitrary", "arbitrary")),
        name="ssd_prompt",
    )(zx, dtraw, cw, cb, dtb, alog, dexp, ng, eh)


def _ssd_sample_pre_kernel(zx_ref, dt_ref, st_ref, cw_ref, cb_ref, dtb_ref, alog_ref, dexp_ref,
                           eh_ref, gh_ref, a_ref, ex_ref, zs_ref, xw_ref, c_ref, b_ref, dec_ref,
                           *, d, g_n, n):
    ts, bs, _ = zx_ref.shape
    kw = cw_ref.shape[0]
    gn = g_n * n
    rows = [st_ref[j] for j in range(kw - 1)]
    rows += [zx_ref[t][:, d:].astype(F32) for t in range(ts)]
    eh = eh_ref[...]
    gh = gh_ref[...]
    a = -jnp.exp(alog_ref[...])
    xs, bm, cm, dts, acs = [], [], [], [], []
    run = None
    for t in range(ts):
        conv = cb_ref[...] + cw_ref[0:1, :] * rows[t]
        for k in range(1, kw):
            conv = conv + cw_ref[k:k + 1, :] * rows[t + k]
        xbc = _silu(conv)
        xs.append(xbc[:, :d])
        bm.append(xbc[:, d:d + gn])
        cm.append(xbc[:, d + gn:])
        dtt = _softplus(dt_ref[t] + dtb_ref[...])
        dts.append(dtt)
        run = dtt * a if run is None else run + dtt * a
        acs.append(run)
    alast = acs[-1]
    dec_ref[...] = jnp.exp(alast)
    for t in range(ts):
        acc = dexp_ref[...] * xs[t]
        for j in range(t + 1):
            cbh = _dot_parts(_split2(cm[t] * bm[j]), gh)
            coef = cbh * dts[j]
            if j < t:
                coef = coef * jnp.exp(acs[t] - acs[j])
            acc = acc + _dot_parts(_split2(coef), eh) * xs[j]
        a_ref[t] = acc
        ex_ref[t] = _dot_parts(_split2(jnp.exp(acs[t])), eh)
        zs_ref[t] = _silu(zx_ref[t][:, :d].astype(F32))
        xw_ref[t] = xs[t] * _dot_parts(_split2(jnp.exp(alast - acs[t]) * dts[t]), eh)
        c_ref[t] = cm[t]
        b_ref[t] = bm[t]


def _ssd_sample_pre(zx3, dt3, tblk, ts, sconv_t, layer, cw, cb, dtb, alog, dexp, eh, gh, d, g_n, n):
    bs = zx3.shape[1]
    gn = g_n * n
    kern = functools.partial(_ssd_sample_pre_kernel, d=d, g_n=g_n, n=n)
    sb = min(SAMPLE_PRE_BLOCK, bs)
    kw = cw.shape[0]
    const = lambda a: pl.BlockSpec(a.shape, lambda i: (0,) * a.ndim)
    seq3 = lambda c: pl.BlockSpec((ts, sb, c), lambda i: (0, i, 0))
    src3 = lambda c: pl.BlockSpec((ts, sb, c), lambda i: (tblk, i, 0))
    big = jax.ShapeDtypeStruct((ts, bs, d), F32)
    small = jax.ShapeDtypeStruct((ts, bs, gn), F32)
    return pl.pallas_call(
        kern,
        grid=(bs // sb,),
        in_specs=[src3(zx3.shape[2]), src3(LANES),
                  pl.BlockSpec((None, kw - 1, sb, sconv_t.shape[3]), lambda i: (layer, 0, i, 0)),
                  const(cw), const(cb), const(dtb), const(alog), const(dexp), const(eh), const(gh)],
        out_specs=[seq3(d), seq3(d), seq3(d), seq3(d), seq3(gn), seq3(gn),
                   pl.BlockSpec((sb, LANES), lambda i: (i, 0))],
        out_shape=[big, big, big, big, small, small, jax.ShapeDtypeStruct((bs, LANES), F32)],
        compiler_params=_cparams(("arbitrary",)),
        name="ssd_sample_pre",
    )(zx3, dt3, sconv_t, cw, cb, dtb, alog, dexp, eh, gh)


def _ssd_sample_state_kernel(dec_ref, s0_ref, a_ref, ex_ref, zs_ref, xwt_ref, c_ref, b_ref, ng_ref,
                             prev_ref, y_ref, s1_ref, *, g_n, n, k_h, p, sb):
    del prev_ref
    i = pl.program_id(0)
    kp = k_h * p
    n_heads = g_n * k_h
    rows8 = sb * SUBLANES
    rid = lax.broadcasted_iota(jnp.int32, (rows8, 1), 0)
    ng = ng_ref[...]

    def body(s, carry):
        r0 = pl.multiple_of(s * SUBLANES, SUBLANES)
        in_s = (rid >= r0) & (rid < r0 + SUBLANES)
        for g in range(g_n):
            cg = c_ref[pl.ds(r0, SUBLANES), g * n:(g + 1) * n].astype(BF16)
            s_g = s0_ref[s, g * kp:(g + 1) * kp, :]
            yoff = _dot_nt(cg, s_g.astype(BF16))
            b_all = b_ref[:, g * n:(g + 1) * n]
            b_msk = jnp.where(in_s, b_all, 0.0).astype(BF16)
            upd = _dot(xwt_ref[g * kp:(g + 1) * kp, :].astype(BF16), b_msk)
            for hh in range(k_h):
                h = g * k_h + hh
                dsc = dec_ref[(i * sb + s) * n_heads + h]
                s1_ref[s, h * p:(h + 1) * p, :] = (dsc * s_g[hh * p:(hh + 1) * p, :]
                                                   + upd[hh * p:(hh + 1) * p, :])
            cols = slice(g * kp, (g + 1) * kp)
            yg = ((a_ref[pl.ds(r0, SUBLANES), cols] + ex_ref[pl.ds(r0, SUBLANES), cols] * yoff)
                  * zs_ref[pl.ds(r0, SUBLANES), cols])
            ms_ = jnp.mean(yg * yg, axis=-1, keepdims=True)
            y_ref[pl.ds(r0, SUBLANES), cols] = (yg * lax.rsqrt(ms_ + EPS) * ng[:, cols]).astype(y_ref.dtype)
        return carry

    lax.fori_loop(0, sb, body, 0)


def _ssd_sample_state(dec, s0_all, layer, s1_prev, a8, ex8, zs8, xwt, c8, b8, ng, g_n, n, k_h, p):
    depth, bs, hp, _ = s0_all.shape
    d = a8.shape[1]
    gn = g_n * n
    sb = min(SAMPLE_SEQ_BLOCK, bs)
    r8 = sb * SUBLANES
    kern = functools.partial(_ssd_sample_state_kernel, g_n=g_n, n=n, k_h=k_h, p=p, sb=sb)
    return pl.pallas_call(
        kern,
        grid=(bs // sb,),
        in_specs=[
            pl.BlockSpec(memory_space=pltpu.SMEM),
            pl.BlockSpec((None, sb, hp, n), lambda i: (layer, i, 0, 0)),
            pl.BlockSpec((r8, d), lambda i: (i, 0)),
            pl.BlockSpec((r8, d), lambda i: (i, 0)),
            pl.BlockSpec((r8, d), lambda i: (i, 0)),
            pl.BlockSpec((None, d, r8), lambda i: (i, 0, 0)),
            pl.BlockSpec((r8, gn), lambda i: (i, 0)),
            pl.BlockSpec((r8, gn), lambda i: (i, 0)),
            pl.BlockSpec((1, d), lambda i: (0, 0)),
            pl.BlockSpec(memory_space=pl.ANY),
        ],
        out_specs=[
            pl.BlockSpec((r8, d), lambda i: (i, 0)),
            pl.BlockSpec((None, sb, hp, n), lambda i: (layer, i, 0, 0)),
        ],
        out_shape=[
            jax.ShapeDtypeStruct((bs * SUBLANES, d), F32),
            jax.ShapeDtypeStruct((depth, bs, hp, n), F32),
        ],
        input_output_aliases={9: 1},
        compiler_params=_cparams(("arbitrary",)),
        name="ssd_sample_state",
    )(dec, s0_all, a8, ex8, zs8, xwt, c8, b8, ng, s1_prev)


def _ln_swish(v, g, b):
    mu = jnp.mean(v, axis=-1, keepdims=True)
    cen = v - mu
    var = jnp.mean(cen * cen, axis=-1, keepdims=True)
    return _silu(cen * lax.rsqrt(var + EPS) * g + b)


def _conf_prompt_kernel(u_ref, w_ref, b_ref, lg_ref, lb_ref, v_ref, upad, ush, pre, *, halo):
    c = pl.program_id(1)
    L, d = u_ref.shape
    kw = w_ref.shape[0]
    span = L + halo - SUBLANES

    @pl.when(c == 0)
    def _():
        upad[0:halo, :] = jnp.zeros((halo, d), F32)

    u_b = u_ref[...]
    u = u_b.astype(F32)
    upad[halo:halo + L, :] = u
    row = lax.broadcasted_iota(jnp.int32, (L, L), 0)
    col = lax.broadcasted_iota(jnp.int32, (L, L), 1)
    for r in range(1, SUBLANES):
        ush[r - 1, 0:halo, :] = upad[r:r + halo, :]
        shifted = _dot(jnp.where(col == row + r, 1.0, 0.0).astype(BF16), u_b)
        ush[r - 1, halo:span, :] = shifted[0:span - halo, :]
    for j in range(d // LANES):
        cols = slice(j * LANES, (j + 1) * LANES)
        acc = jnp.broadcast_to(b_ref[:, cols], (L, LANES))
        for k in range(kw):
            q, r = divmod(halo - (kw - 1) + k, SUBLANES)
            rows = slice(q * SUBLANES, q * SUBLANES + L)
            src = upad[rows, cols] if r == 0 else ush[r - 1, rows, cols]
            acc = acc + w_ref[k:k + 1, cols] * src
        pre[:, cols] = acc
    upad[0:halo, :] = u[L - halo:, :]
    v_ref[...] = _ln_swish(pre[...], lg_ref[...], lb_ref[...]).astype(v_ref.dtype)


def _conf_prompt(u, w, b, lg, lb, nb, t):
    assert u.dtype == BF16
    d = u.shape[1]
    kw = w.shape[0]
    L = min(CONF_CHUNK, t)
    nc = t // L
    halo = -(-(kw - 1) // SUBLANES) * SUBLANES
    const = lambda a: pl.BlockSpec(a.shape, lambda bb, c: (0,) * a.ndim)
    return pl.pallas_call(
        functools.partial(_conf_prompt_kernel, halo=halo),
        grid=(nb, nc),
        in_specs=[pl.BlockSpec((L, d), lambda bb, c: (bb * nc + c, 0)), const(w), const(b), const(lg), const(lb)],
        out_specs=pl.BlockSpec((L, d), lambda bb, c: (bb * nc + c, 0)),
        out_shape=jax.ShapeDtypeStruct((nb * t, d), BF16),
        scratch_shapes=[pltpu.VMEM((L + halo, d), F32), pltpu.VMEM((SUBLANES - 1, L + halo, d), F32),
                        pltpu.VMEM((L, d), F32)],
        compiler_params=_cparams(("arbitrary", "arbitrary")),
        name="conf_prompt",
    )(u, w, b, lg, lb)


def _conf_sample_kernel(u_ref, st_ref, w_ref, b_ref, lg_ref, lb_ref, v_ref):
    ts, sb, d = u_ref.shape
    kw = w_ref.shape[0]
    rows = [st_ref[j] for j in range(kw - 1)]
    rows += [u_ref[t].astype(F32) for t in range(ts)]
    for t in range(ts):
        acc = b_ref[...] + w_ref[0:1, :] * rows[t]
        for k in range(1, kw):
            acc = acc + w_ref[k:k + 1, :] * rows[t + k]
        v_ref[t] = _ln_swish(acc, lg_ref[...], lb_ref[...]).astype(v_ref.dtype)


def _conf_sample(u3, tblk, ts, cconv_t, layer, w, b, lg, lb):
    _, bs, d = u3.shape
    kw = w.shape[0]
    sb = min(CONF_SAMPLE_BLOCK, bs)
    const = lambda a: pl.BlockSpec(a.shape, lambda i: (0,) * a.ndim)
    return pl.pallas_call(
        _conf_sample_kernel,
        grid=(bs // sb,),
        in_specs=[
            pl.BlockSpec((ts, sb, d), lambda i: (tblk, i, 0)),
            pl.BlockSpec((None, kw - 1, sb, d), lambda i: (layer, 0, i, 0)),
            const(w), const(b), const(lg), const(lb),
        ],
        out_specs=pl.BlockSpec((ts, sb, d), lambda i: (0, i, 0)),
        out_shape=jax.ShapeDtypeStruct((ts, bs, d), BF16),
        compiler_params=_cparams(("arbitrary",)),
        name="conf_sample",
    )(u3, cconv_t, w, b, lg, lb)


def _merge_kernel(yp_ref, ys_ref, vp_ref, vs_ref, ws_ref, wc_ref, gs_ref, gc_ref, o_ref, *, npt):
    is_p = pl.program_id(1) < npt
    y = jnp.where(is_p, yp_ref[...], ys_ref[...])
    v = jnp.where(is_p, vp_ref[...], vs_ref[...])
    a = _dot(y, ws_ref[...])
    b = _dot(v, wc_ref[...])
    o_ref[...] = (gs_ref[...].astype(F32) * a + gc_ref[...].astype(F32) * b).astype(o_ref.dtype)


def _merge(y_p, y_s, v_p, v_s, ws, wc, layer, gates, tm):
    m, d = gates.shape[0], y_p.shape[1]
    npt = y_p.shape[0] // tm
    tn = _pick_tn(d)
    nj = d // tn
    p_spec = pl.BlockSpec((tm, d), lambda j, i: (jnp.minimum(i, npt - 1), 0))
    s_spec = pl.BlockSpec((tm, d), lambda j, i: (0, 0))
    return pl.pallas_call(
        functools.partial(_merge_kernel, npt=npt),
        grid=(nj, m // tm),
        in_specs=[
            p_spec, s_spec, p_spec, s_spec,
            pl.BlockSpec((None, d, tn), lambda j, i: (layer, 0, j)),
            pl.BlockSpec((None, d, tn), lambda j, i: (layer, 0, j)),
            pl.BlockSpec((tm, tn), lambda j, i: (i, j)),
            pl.BlockSpec((tm, tn), lambda j, i: (i, nj + j)),
        ],
        out_specs=pl.BlockSpec((tm, tn), lambda j, i: (i, j)),
        out_shape=jax.ShapeDtypeStruct((m, d), BF16),
        compiler_params=_cparams(("arbitrary", "arbitrary")),
        name="merge",
    )(y_p, y_s, v_p, v_s, ws, wc, gates, gates)


def _outproj_kernel(mx_ref, w_ref, x_ref, gp_ref, gs_ref, o_ref, *, npt, ts):
    gate = _tile_mod(pl.program_id(1), npt, ts, gp_ref, gs_ref)
    o_ref[...] = x_ref[...] + gate * _dot(mx_ref[...], w_ref[...])


def _outproj(mixed, wo, layer, x, modp, mods, sec, dims):
    m, d = x.shape
    tm, npt, tps, nb, bs, ts = dims
    tn = _pick_tn(d)
    nj = d // tn
    colj = lambda s, j, i: s * nj + j
    return pl.pallas_call(
        functools.partial(_outproj_kernel, npt=npt, ts=ts),
        grid=(nj, m // tm),
        in_specs=[
            pl.BlockSpec((tm, d), lambda j, i: (i, 0)),
            pl.BlockSpec((None, d, tn), lambda j, i: (layer, 0, j)),
            pl.BlockSpec((tm, tn), lambda j, i: (i, j)),
            *_mod_specs(npt, tps, nb, bs, tn, sec, colj, tile_axis=1),
        ],
        out_specs=pl.BlockSpec((tm, tn), lambda j, i: (i, j)),
        out_shape=jax.ShapeDtypeStruct((m, d), F32),
        compiler_params=_cparams(("arbitrary", "arbitrary")),
        name="outproj",
    )(mixed, wo, x, modp, mods)


def _pack_bf16_pairs(v):
    half = v.shape[1] // 2
    bits = pltpu.bitcast(v.astype(BF16).astype(F32), jnp.uint32)
    return bits[:, :half] | (bits[:, half:] >> 16)


def _unpack_bf16_pairs(w):
    left = pltpu.bitcast(w & jnp.uint32(0xFFFF0000), F32)
    right = pltpu.bitcast(w << 16, F32)
    return jnp.concatenate([left, right], axis=1)


INFO_RANK1, INFO_RANK2, INFO_E1, INFO_E2, INFO_W1, INFO_W2 = range(6)


def _router_kernel(x_ref, g_ref, scp_ref, scs_ref, shp_ref, shs_ref, whi_ref, wlo_ref, rb_ref,
                   hp_ref, info_ref, cnt_ref, *, npt, ts, n_e, n_g):
    i = pl.program_id(0)
    epg = n_e // n_g

    @pl.when(i == 0)
    def _():
        cnt_ref[...] = jnp.zeros(cnt_ref.shape, F32)

    x = x_ref[...]
    xn = x * lax.rsqrt(jnp.mean(x * x, axis=-1, keepdims=True) + EPS) * g_ref[...]
    sc = _tile_mod(i, npt, ts, scp_ref, scs_ref)
    sh = _tile_mod(i, npt, ts, shp_ref, shs_ref)
    h = xn * (1.0 + sc) + sh
    hi, lo = _split2(h)
    hp_ref[...] = _pack_bf16_pairs(h)
    logit = _dot(hi, whi_ref[...]) + _dot(lo, whi_ref[...]) + _dot(hi, wlo_ref[...]) + rb_ref[...]
    tm = logit.shape[0]
    lane = lax.broadcasted_iota(jnp.int32, (tm, LANES), 1).astype(F32)
    first = lambda hit: jnp.min(jnp.where(hit, lane, float(LANES)), axis=-1, keepdims=True)
    gl = jnp.where(lane >= n_e, jnp.where(lane < n_e + n_g, logit, NEG_BIG), NEG_BIG)
    gmax = jnp.max(gl, axis=-1, keepdims=True)
    g_top = 1.0 / jnp.sum(jnp.exp(gl - gmax), axis=-1, keepdims=True)
    g_idx = first(gl == gmax) - n_e
    el = jnp.where(lane >= g_idx * epg, jnp.where(lane < (g_idx + 1.0) * epg, logit, NEG_BIG), NEG_BIG)
    m1 = jnp.max(el, axis=-1, keepdims=True)
    i1 = first(el == m1)
    el2 = jnp.where(lane == i1, NEG_BIG, el)
    m2 = jnp.max(el2, axis=-1, keepdims=True)
    i2 = first(el2 == m2)
    r = jnp.exp(m2 - m1)
    w1 = 1.0 / (1.0 + r)
    w2 = r * w1
    oh1 = jnp.where(lane == i1, 1.0, 0.0)
    oh2 = jnp.where(lane == i2, 1.0, 0.0)
    both = oh1 + oh2
    row = lax.broadcasted_iota(jnp.int32, (tm, tm), 0)
    col = lax.broadcasted_iota(jnp.int32, (tm, tm), 1)
    before = _dot(jnp.where(row > col, 1.0, 0.0).astype(BF16), both.astype(BF16)) + cnt_ref[0:1, :]
    rank1 = jnp.sum(before * oh1, axis=-1, keepdims=True)
    rank2 = jnp.sum(before * oh2, axis=-1, keepdims=True)
    cnt_ref[...] = cnt_ref[...] + jnp.sum(both, axis=0, keepdims=True)
    rec = [rank1, rank2, i1, i2, g_top * w1, g_top * w2]
    info = jnp.zeros((tm, LANES), F32)
    for k, val in enumerate(rec):
        info = jnp.where(lane == float(k), val, info)
    info_ref[...] = info


def _router(x, gain, modp, mods, sec_sc, sec_sh, whi, wlo, rb, n_e, n_g, dims):
    m, d = x.shape
    tm, npt, tps, nb, bs, ts = dims
    col0 = lambda sec, i: sec
    return pl.pallas_call(
        functools.partial(_router_kernel, npt=npt, ts=ts, n_e=n_e, n_g=n_g),
        grid=(m // tm,),
        in_specs=[
            pl.BlockSpec((tm, d), lambda i: (i, 0)),
            pl.BlockSpec((1, d), lambda i: (0, 0)),
            *_mod_specs(npt, tps, nb, bs, d, sec_sc, col0),
            *_mod_specs(npt, tps, nb, bs, d, sec_sh, col0),
            pl.BlockSpec((d, LANES), lambda i: (0, 0)),
            pl.BlockSpec((d, LANES), lambda i: (0, 0)),
            pl.BlockSpec((1, LANES), lambda i: (0, 0)),
        ],
        out_specs=[pl.BlockSpec((tm, d // 2), lambda i: (i, 0)), pl.BlockSpec((tm, LANES), lambda i: (i, 0)),
                   pl.BlockSpec((SUBLANES, LANES), lambda i: (0, 0))],
        out_shape=[jax.ShapeDtypeStruct((m, d // 2), jnp.uint32), jax.ShapeDtypeStruct((m, LANES), F32),
                   jax.ShapeDtypeStruct((SUBLANES, LANES), F32)],
        compiler_params=_cparams(("arbitrary",)),
        name="router",
    )(x, gain, modp, mods, modp, mods, whi, wlo, rb)


def _slots_kernel(info_ref, offs_ref, pos_ref):
    info = info_ref[...]
    lane = lax.broadcasted_iota(jnp.int32, info.shape, 1).astype(F32)
    offs = offs_ref[...]
    base1 = jnp.sum(jnp.where(lane == info[:, INFO_E1:INFO_E1 + 1], offs, 0.0), axis=-1, keepdims=True)
    base2 = jnp.sum(jnp.where(lane == info[:, INFO_E2:INFO_E2 + 1], offs, 0.0), axis=-1, keepdims=True)
    rec = jnp.where(lane == 0.0, info[:, INFO_RANK1:INFO_RANK1 + 1] + base1,
                    jnp.where(lane == 1.0, info[:, INFO_RANK2:INFO_RANK2 + 1] + base2, 0.0))
    pos_ref[...] = rec.T[0:SUBLANES, :].astype(jnp.int32)


def _slots(info, offs_row, tm):
    m = info.shape[0]
    return pl.pallas_call(
        _slots_kernel,
        grid=(m // tm,),
        in_specs=[pl.BlockSpec((tm, LANES), lambda i: (i, 0)), pl.BlockSpec((1, LANES), lambda i: (0, 0))],
        out_specs=pl.BlockSpec((SUBLANES, tm), lambda i: (0, i)),
        out_shape=jax.ShapeDtypeStruct((SUBLANES, m), jnp.int32),
        compiler_params=_cparams(("arbitrary",)),
        name="moe_slots",
    )(info, offs_row)


def _moe_plan(info, cnt, n_e, tg, nt_max, tm):
    counts = cnt[0, :n_e].astype(jnp.int32)
    tiles = (counts + tg - 1) // tg
    tile_end = jnp.cumsum(tiles)
    offs = (tile_end - tiles) * tg
    n_used = tile_end[n_e - 1]
    offs_row = jnp.zeros((1, LANES), F32).at[0, :n_e].set(offs.astype(F32))
    pos = _slots(info, offs_row, tm)
    pos1, pos2 = pos[0], pos[1]
    tile = jnp.minimum(jnp.arange(nt_max, dtype=jnp.int32), n_used - 1)
    tile_expert = jnp.searchsorted(tile_end, tile, side="right").astype(jnp.int32)
    return pos1, pos2, tile_expert, n_used.reshape(1)


def _dispatch_kernel(pos1_ref, pos2_ref, hp_ref, xs0_ref, xs_ref, sem):
    del xs0_ref
    tm = hp_ref.shape[0]
    base = pl.program_id(0) * tm

    def issue(r, carry):
        src = hp_ref.at[pl.ds(r, 1)]
        pltpu.make_async_copy(src, xs_ref.at[pl.ds(pos1_ref[base + r], 1)], sem).start(priority=0)
        pltpu.make_async_copy(src, xs_ref.at[pl.ds(pos2_ref[base + r], 1)], sem).start(priority=1)
        return carry
    lax.fori_loop(0, tm, issue, 0, unroll=ROW_DMA_UNROLL)

    def drain(r, carry):
        for _ in range(2):
            pltpu.make_async_copy(hp_ref.at[pl.ds(0, 1)], xs_ref.at[pl.ds(0, 1)], sem).wait()
        return carry
    lax.fori_loop(0, tm, drain, 0, unroll=ROW_DMA_UNROLL)


def _dispatch(pos1, pos2, hp, n_slots, tm):
    m, half = hp.shape
    smem = pl.BlockSpec(memory_space=pltpu.SMEM)
    hbm = pl.BlockSpec(memory_space=pl.ANY)
    return pl.pallas_call(
        _dispatch_kernel,
        grid=(m // tm,),
        in_specs=[smem, smem, pl.BlockSpec((tm, half), lambda i: (i, 0)), hbm],
        out_specs=hbm,
        out_shape=jax.ShapeDtypeStruct((n_slots, half), jnp.uint32),
        scratch_shapes=[pltpu.SemaphoreType.DMA(())],
        input_output_aliases={3: 0},
        compiler_params=_cparams(("arbitrary",)),
        name="moe_dispatch",
    )(pos1, pos2, hp, jnp.zeros((n_slots, half), jnp.uint32))


def _experts_kernel(te_ref, nu_ref, xs_ref, wg_ref, wu_ref, wd_ref, ys_ref, wg_b, wu_b, wd_b):
    i = pl.program_id(0)

    @pl.when(jnp.logical_or(i == 0, te_ref[i] != te_ref[jnp.maximum(i - 1, 0)]))
    def _():
        wg_b[...] = wg_ref[...].astype(BF16)
        wu_b[...] = wu_ref[...].astype(BF16)
        wd_b[...] = wd_ref[...].astype(BF16)

    @pl.when(i < nu_ref[0])
    def _():
        x = _unpack_bf16_pairs(xs_ref[...]).astype(BF16)
        hid = _silu(_dot(x, wg_b[...])) * _dot(x, wu_b[...])
        ys_ref[...] = _pack_bf16_pairs(_dot(hid.astype(BF16), wd_b[...]))

    @pl.when(i >= nu_ref[0])
    def _():
        ys_ref[...] = jnp.zeros(ys_ref.shape, jnp.uint32)


def _experts(tile_expert, n_used, xs, wg, wu, wd, layer, tg):
    n_slots, half = xs.shape
    _, _, d, f = wg.shape
    grid_spec = pltpu.PrefetchScalarGridSpec(
        num_scalar_prefetch=2,
        grid=(n_slots // tg,),
        in_specs=[
            pl.BlockSpec((tg, half), lambda i, te, nu: (i, 0)),
            pl.BlockSpec((None, None, d, f), lambda i, te, nu: (layer, te[i], 0, 0)),
            pl.BlockSpec((None, None, d, f), lambda i, te, nu: (layer, te[i], 0, 0)),
            pl.BlockSpec((None, None, f, d), lambda i, te, nu: (layer, te[i], 0, 0)),
        ],
        out_specs=pl.BlockSpec((tg, half), lambda i, te, nu: (i, 0)),
        scratch_shapes=[pltpu.VMEM((d, f), BF16), pltpu.VMEM((d, f), BF16), pltpu.VMEM((f, d), BF16)],
    )
    return pl.pallas_call(
        _experts_kernel,
        grid_spec=grid_spec,
        out_shape=jax.ShapeDtypeStruct((n_slots, half), jnp.uint32),
        compiler_params=_cparams(("arbitrary",)),
        name="moe_experts",
    )(tile_expert, n_used, xs, wg, wu, wd)


def _combine_kernel(pos1_ref, pos2_ref, info_ref, x_ref, gp_ref, gs_ref, ys_ref, o_ref, buf_a, buf_b, sem,
                    *, npt, ts):
    i = pl.program_id(0)
    tm = x_ref.shape[0]
    slot = lax.rem(i, 2)

    def issue(tile, s):
        def body(r, carry):
            t = tile * tm + r
            pltpu.make_async_copy(ys_ref.at[pl.ds(pos1_ref[t], 1)], buf_a.at[s, pl.ds(r, 1)], sem.at[s]).start(priority=0)
            pltpu.make_async_copy(ys_ref.at[pl.ds(pos2_ref[t], 1)], buf_b.at[s, pl.ds(r, 1)], sem.at[s]).start(priority=1)
            return carry
        lax.fori_loop(0, tm, body, 0, unroll=ROW_DMA_UNROLL)

    @pl.when(i == 0)
    def _():
        issue(0, 0)

    @pl.when(i + 1 < pl.num_programs(0))
    def _():
        issue(i + 1, 1 - slot)

    def wait_body(r, carry):
        pltpu.make_async_copy(ys_ref.at[pl.ds(0, 1)], buf_a.at[slot, pl.ds(0, 1)], sem.at[slot]).wait()
        pltpu.make_async_copy(ys_ref.at[pl.ds(0, 1)], buf_b.at[slot, pl.ds(0, 1)], sem.at[slot]).wait()
        return carry
    lax.fori_loop(0, tm, wait_body, 0, unroll=ROW_DMA_UNROLL)

    info = info_ref[...]
    w1 = info[:, INFO_W1:INFO_W1 + 1]
    w2 = info[:, INFO_W2:INFO_W2 + 1]
    moe = w1 * _unpack_bf16_pairs(buf_a[slot]) + w2 * _unpack_bf16_pairs(buf_b[slot])
    o_ref[...] = x_ref[...] + _tile_mod(i, npt, ts, gp_ref, gs_ref) * moe


def _combine(pos1, pos2, info, x, modp, mods, sec, ys, dims):
    m, d = x.shape
    tm, npt, tps, nb, bs, ts = dims
    col0 = lambda s, i: s
    smem = pl.BlockSpec(memory_space=pltpu.SMEM)
    return pl.pallas_call(
        functools.partial(_combine_kernel, npt=npt, ts=ts),
        grid=(m // tm,),
        in_specs=[
            smem, smem,
            pl.BlockSpec((tm, LANES), lambda i: (i, 0)),
            pl.BlockSpec((tm, d), lambda i: (i, 0)),
            *_mod_specs(npt, tps, nb, bs, d, sec, col0),
            pl.BlockSpec(memory_space=pl.ANY),
        ],
        out_specs=pl.BlockSpec((tm, d), lambda i: (i, 0)),
        out_shape=jax.ShapeDtypeStruct((m, d), F32),
        scratch_shapes=[pltpu.VMEM((2, tm, d // 2), jnp.uint32), pltpu.VMEM((2, tm, d // 2), jnp.uint32),
                        pltpu.SemaphoreType.DMA((2,))],
        compiler_params=_cparams(("arbitrary",)),
        name="moe_combine",
    )(pos1, pos2, info, x, modp, mods, ys)


def _moe(hp, info, cnt, wg, wu, wd, layer, x, modp, mods, sec, dims):
    m = x.shape[0]
    n_e = wg.shape[1]
    tg = MOE_TG
    nt_max = -(-2 * m // tg) + n_e
    pos1, pos2, tile_expert, n_used = _moe_plan(info, cnt, n_e, tg, nt_max, dims[0])
    xs = _dispatch(pos1, pos2, hp, nt_max * tg, dims[0])
    ys = _experts(tile_expert, n_used, xs, wg, wu, wd, layer, tg)
    return _combine(pos1, pos2, info, x, modp, mods, sec, ys, dims)


def _final_norm_kernel(x_ref, g_ref, o_ref):
    x = x_ref[...]
    o_ref[...] = x * lax.rsqrt(jnp.mean(x * x, axis=-1, keepdims=True) + EPS) * g_ref[...]


def _final_norm(x, g, tm, tile0, ntiles):
    d = x.shape[1]
    return pl.pallas_call(
        _final_norm_kernel,
        grid=(ntiles,),
        in_specs=[pl.BlockSpec((tm, d), lambda i: (tile0 + i, 0)), pl.BlockSpec((1, d), lambda i: (0, 0))],
        out_specs=pl.BlockSpec((tm, d), lambda i: (i, 0)),
        out_shape=jax.ShapeDtypeStruct((ntiles * tm, d), F32),
        compiler_params=_cparams(("arbitrary",)),
        name="final_norm",
    )(x, g)


def _to_seq8(a):
    ts, bs, c = a.shape
    a = jnp.transpose(a, (1, 0, 2))
    a = jnp.pad(a, ((0, 0), (0, SUBLANES - ts), (0, 0)))
    return a.reshape(bs * SUBLANES, c)


def kernel(x_prompt, x_sample, c_prompt, c_sample, state_ssm, state_ssd_conv, state_conf_conv, norm_mix_g, norm_ffn_g, w_ada, b_ada, w_in, ssd_conv_w, ssd_conv_b, ssd_dt_bias, ssd_a_log, ssd_d, ssd_norm_g, w_ssd_out, conf_conv_w, conf_conv_b, conf_ln_g, conf_ln_b, w_conf_out, w_o, w_group_router, b_group_router, w_expert_router, b_expert_router, w_exp_gate, w_exp_up, w_exp_down, final_norm_g):
    nb, t, d = x_prompt.shape
    bs, ts, _ = x_sample.shape
    depth = w_in.shape[0]
    n_h = ssd_dt_bias.shape[1]
    p = d // n_h
    n = state_ssm.shape[-1]
    cc = ssd_conv_w.shape[-1]
    g_n = (cc - d) // (2 * n)
    k_h = n_h // g_n
    gn = g_n * n
    kc = conf_conv_w.shape[1]
    n_e = w_exp_gate.shape[1]
    n_g = w_group_router.shape[-1]
    tm = bs * ts
    assert t % tm == 0 and tm % SUBLANES == 0 and bs % SUBLANES == 0 and ts <= SUBLANES
    assert 2 * p == LANES and k_h % 2 == 0 and n_h <= LANES and n_e + n_g <= LANES
    npt = nb * t // tm
    dims = (tm, npt, t // tm, nb, bs, ts)
    mp = nb * t

    x = jnp.concatenate([x_prompt.reshape(mp, d), jnp.transpose(x_sample, (1, 0, 2)).reshape(tm, d)], axis=0)

    rc = -(-(nb + bs) // SUBLANES) * SUBLANES
    c_all = jnp.concatenate([c_prompt, c_sample, jnp.zeros((rc - nb - bs, d), F32)], axis=0)
    mod = _adaln(c_all, w_ada, b_ada)

    head_of = jnp.arange(d) // p
    eh = (jnp.arange(LANES)[:, None] == head_of[None, :]).astype(BF16)
    gh = ((jnp.arange(gn)[:, None] // n) == (jnp.arange(LANES)[None, :] // k_h)).astype(BF16)
    gh = gh * (jnp.arange(LANES)[None, :] < n_h).astype(BF16)
    pad_h = lambda v: jnp.pad(v, (0, LANES - n_h)).reshape(1, LANES)

    m = mp + tm
    tblk = mp // bs // ts
    assert tblk * bs * ts == mp
    sconv_t = jnp.transpose(state_ssd_conv, (0, 2, 1, 3))
    cconv_t = jnp.transpose(state_conf_conv, (0, 2, 1, 3))
    ssm0_all = state_ssm.reshape(depth, bs, n_h * p, n)
    ssm_s = jnp.zeros(ssm0_all.shape, F32)
    o_dt, o_glu = d + cc, d + cc + n_h
    w_main = w_in[:, :, :o_dt].astype(BF16)
    w_dt = jnp.pad(w_in[:, :, o_dt:o_glu], ((0, 0), (0, 0), (0, LANES - n_h))).astype(BF16)
    w_rest = w_in[:, :, o_glu:].astype(BF16)
    w_so, w_co, w_oo = w_ssd_out.astype(BF16), w_conf_out.astype(BF16), w_o.astype(BF16)
    ssm_p, sconv_p, cconv_p, sconv_s, cconv_s = [], [], [], [], []
    for l in range(depth):
        modp = mod[l, :nb].reshape(nb, 1, 6 * d)
        mods = mod[l, nb:nb + bs]
        h1 = _modnorm(x, norm_mix_g[l].reshape(1, d), modp, mods, 1, 0, dims)
        zx = _proj(h1, w_main, [0], o_dt, l, BF16, "plain")
        dtraw = _proj(h1, w_dt, [0], LANES, l, F32, "plain")
        u = _proj(h1, w_rest, [0, d], d, l, BF16, "glu")
        gates = _proj(h1, w_rest, [2 * d], 2 * d, l, BF16, "sigmoid")

        cw = ssd_conv_w[l]
        cb = ssd_conv_b[l].reshape(1, cc)
        dtb = pad_h(ssd_dt_bias[l])
        alog = pad_h(ssd_a_log[l])
        dexp = jnp.repeat(ssd_d[l], p).reshape(1, d)
        ng = ssd_norm_g[l].reshape(1, d)

        y_p, s_p = _ssd_prompt(zx, dtraw, cw, cb, dtb, alog, dexp, ng, eh, nb, t, n_h * p, n, g_n, k_h, p)
        zx_s = zx[mp:].reshape(ts, bs, d + cc)
        sconv0 = state_ssd_conv[l]
        a_t, ex_t, zs_t, xw_t, c_t, b_t, dec = _ssd_sample_pre(
            zx.reshape(m // bs, bs, d + cc), dtraw.reshape(m // bs, bs, LANES), tblk, ts, sconv_t, l,
            cw, cb, dtb, alog, dexp, eh, gh, d, g_n, n)
        sb = min(SAMPLE_SEQ_BLOCK, bs)
        xwt = _to_seq8(xw_t).reshape(bs // sb, sb * SUBLANES, d).transpose(0, 2, 1)
        dec = dec[:, :n_h].reshape(bs * n_h)
        y_s8, ssm_s = _ssd_sample_state(dec, ssm0_all, l, ssm_s, _to_seq8(a_t), _to_seq8(ex_t),
                                        _to_seq8(zs_t), xwt, _to_seq8(c_t), _to_seq8(b_t), ng, g_n, n, k_h, p)
        y_s = jnp.transpose(y_s8.reshape(bs, SUBLANES, d)[:, :ts], (1, 0, 2)).reshape(tm, d).astype(BF16)

        ccw = conf_conv_w[l]
        ccb = conf_conv_b[l].reshape(1, d)
        lg = conf_ln_g[l].reshape(1, d)
        lb = conf_ln_b[l].reshape(1, d)
        v_p = _conf_prompt(u, ccw, ccb, lg, lb, nb, t)
        u_s = u[mp:].reshape(ts, bs, d)
        cconv0 = state_conf_conv[l]
        v_s = _conf_sample(u.reshape(m // bs, bs, d), tblk, ts, cconv_t, l, ccw, ccb, lg, lb).reshape(tm, d)

        mixed = _merge(y_p, y_s, v_p, v_s, w_so, w_co, l, gates, tm)
        x = _outproj(mixed, w_oo, l, x, modp, mods, 2, dims)

        w_r = jnp.concatenate([w_expert_router[l], w_group_router[l],
                               jnp.zeros((d, LANES - n_e - n_g), F32)], axis=1)
        w_r_hi = w_r.astype(BF16)
        w_r_lo = (w_r - w_r_hi.astype(F32)).astype(BF16)
        rb = jnp.concatenate([b_expert_router[l], b_group_router[l], jnp.zeros((LANES - n_e - n_g,), F32)]).reshape(1, LANES)
        hp, info, cnt = _router(x, norm_ffn_g[l].reshape(1, d), modp, mods, 4, 3, w_r_hi, w_r_lo, rb, n_e, n_g, dims)
        x = _moe(hp, info, cnt, w_exp_gate, w_exp_up, w_exp_down, l, x, modp, mods, 5, dims)

        ssm_p.append(s_p.reshape(nb, n_h, p, n))
        k3 = sconv0.shape[1]
        sconv_p.append(jnp.stack([zx[(b + 1) * t - k3:(b + 1) * t, d:] for b in range(nb)]).astype(F32))
        cconv_p.append(jnp.stack([u[(b + 1) * t - (kc - 1):(b + 1) * t] for b in range(nb)]).astype(F32))
        xraw_s = jnp.transpose(zx_s[:, :, d:], (1, 0, 2)).astype(F32)
        sconv_s.append(jnp.concatenate([sconv0, xraw_s], axis=1)[:, ts:])
        cconv_s.append(jnp.concatenate([cconv0, jnp.transpose(u_s, (1, 0, 2)).astype(F32)], axis=1)[:, ts:])

    fg = final_norm_g.reshape(1, d)
    y_prompt = _final_norm(x, fg, tm, 0, npt).reshape(nb, t, d)
    y_sample = jnp.transpose(_final_norm(x, fg, tm, npt, 1).reshape(ts, bs, d), (1, 0, 2))
    return (y_prompt, y_sample, jnp.stack(ssm_p), jnp.stack(sconv_p), jnp.stack(cconv_p),
            ssm_s.reshape(depth, bs, n_h, p, n), jnp.stack(sconv_s), jnp.stack(cconv_s))
```

```python
import functools

import jax
import jax.numpy as jnp
from jax import lax
from jax.experimental import pallas as pl
from jax.experimental.pallas import tpu as pltpu

F32 = jnp.float32
BF16 = jnp.bfloat16
EPS = 1e-6
LANES = 128
SUBLANES = 8
VMEM_LIMIT = 52 * 1024 * 1024
NEG_BIG = -1e30
SSD_CHUNK = 128
CONF_CHUNK = 256
SAMPLE_SEQ_BLOCK = 8
SAMPLE_PRE_BLOCK = 32
CONF_SAMPLE_BLOCK = 16
MOE_TG = 256
ROW_DMA_UNROLL = 8
ADA_TN = 1024
PROJ_TN = 1024
PROJ_TM = 1088


def _pick_tn(ncols):
    assert ncols % LANES == 0
    return max(tn for tn in range(LANES, min(PROJ_TN, ncols) + 1, LANES) if ncols % tn == 0)


def _pick_rows(m):
    return max(tm for tm in range(2 * SUBLANES, min(PROJ_TM, m) + 1, 2 * SUBLANES) if m % tm == 0)


def _cparams(sem):
    return pltpu.CompilerParams(dimension_semantics=sem, vmem_limit_bytes=VMEM_LIMIT)


def _silu(x):
    return x * jax.nn.sigmoid(x)


def _softplus(x):
    return jnp.maximum(x, 0.0) + jnp.log1p(jnp.exp(-jnp.abs(x)))


def _split2(x):
    hi = x.astype(BF16)
    lo = (x - hi.astype(F32)).astype(BF16)
    return hi, lo


def _split3(x):
    hi = x.astype(BF16)
    r = x - hi.astype(F32)
    mid = r.astype(BF16)
    lo = (r - mid.astype(F32)).astype(BF16)
    return hi, mid, lo


def _dot(a, b):
    return jnp.dot(a, b, preferred_element_type=F32)


def _dot_nt(a, b):
    return lax.dot_general(a, b, (((1,), (1,)), ((), ())), preferred_element_type=F32)


def _dot_parts(parts, w):
    acc = _dot(parts[0], w)
    for p in parts[1:]:
        acc = acc + _dot(p, w)
    return acc


def _adaln_kernel(c_ref, w_ref, b_ref, o_ref):
    s = _silu(c_ref[...]).astype(BF16)
    o_ref[...] = _dot(s, w_ref[...].astype(BF16)) + b_ref[...]


def _adaln(c_all, w_ada, b_ada):
    depth, d, n6 = w_ada.shape
    rc = c_all.shape[0]
    tn = min(ADA_TN, n6)
    return pl.pallas_call(
        _adaln_kernel,
        grid=(depth, n6 // tn),
        in_specs=[
            pl.BlockSpec((rc, d), lambda l, j: (0, 0)),
            pl.BlockSpec((None, d, tn), lambda l, j: (l, 0, j)),
            pl.BlockSpec((None, 1, tn), lambda l, j: (l, 0, j)),
        ],
        out_specs=pl.BlockSpec((None, rc, tn), lambda l, j: (l, 0, j)),
        out_shape=jax.ShapeDtypeStruct((depth, rc, n6), F32),
        compiler_params=_cparams(("arbitrary", "arbitrary")),
        name="adaln",
    )(c_all, w_ada, b_ada.reshape(depth, 1, n6))


def _tile_mod(i, npt, ts, p_ref, s_ref):
    ms = s_ref[...]
    tiled = jnp.concatenate([ms] * ts, axis=0)
    return jnp.where(i < npt, p_ref[...], tiled)


def _mod_specs(npt, tiles_per_seq, nseq_p, bs, width, sec, col_of, tile_axis=0):
    def p_map(*idx):
        i = idx[tile_axis]
        return (jnp.minimum(i // tiles_per_seq, nseq_p - 1), 0, col_of(sec, *idx))

    def s_map(*idx):
        return (0, col_of(sec, *idx))

    return [pl.BlockSpec((None, 1, width), p_map), pl.BlockSpec((bs, width), s_map)]


def _modnorm_kernel(x_ref, g_ref, scp_ref, scs_ref, shp_ref, shs_ref, h_ref, *, npt, ts):
    i = pl.program_id(0)
    x = x_ref[...]
    xn = x * lax.rsqrt(jnp.mean(x * x, axis=-1, keepdims=True) + EPS) * g_ref[...]
    sc = _tile_mod(i, npt, ts, scp_ref, scs_ref)
    sh = _tile_mod(i, npt, ts, shp_ref, shs_ref)
    h_ref[...] = (xn * (1.0 + sc) + sh).astype(h_ref.dtype)


def _modnorm(x, gain, modp, mods, sec_sc, sec_sh, dims):
    m, d = x.shape
    tm, npt, tps, nb, bs, ts = dims
    col0 = lambda sec, i: sec
    return pl.pallas_call(
        functools.partial(_modnorm_kernel, npt=npt, ts=ts),
        grid=(m // tm,),
        in_specs=[
            pl.BlockSpec((tm, d), lambda i: (i, 0)),
            pl.BlockSpec((1, d), lambda i: (0, 0)),
            *_mod_specs(npt, tps, nb, bs, d, sec_sc, col0),
            *_mod_specs(npt, tps, nb, bs, d, sec_sh, col0),
        ],
        out_specs=pl.BlockSpec((tm, d), lambda i: (i, 0)),
        out_shape=jax.ShapeDtypeStruct((m, d), BF16),
        compiler_params=_cparams(("arbitrary",)),
        name="modnorm",
    )(x, gain, modp, mods, modp, mods)


def _proj_kernel(h_ref, *rest, nw, epilogue):
    w_refs = rest[:nw]
    o_ref = rest[nw]
    h = h_ref[...]
    acc = _dot(h, w_refs[0][...])
    if epilogue == "glu":
        acc = acc * jax.nn.sigmoid(_dot(h, w_refs[1][...]))
    elif epilogue == "sigmoid":
        acc = jax.nn.sigmoid(acc)
    o_ref[...] = acc.astype(o_ref.dtype)


def _proj(h, w, col0s, ncols, layer, out_dtype, epilogue):
    m, d = h.shape
    tn = _pick_tn(ncols)
    tm = _pick_rows(m)
    assert all(c0 % tn == 0 for c0 in col0s)
    w_specs = [pl.BlockSpec((None, d, tn), lambda j, i, off=c0 // tn: (layer, 0, off + j)) for c0 in col0s]
    return pl.pallas_call(
        functools.partial(_proj_kernel, nw=len(col0s), epilogue=epilogue),
        grid=(ncols // tn, m // tm),
        in_specs=[pl.BlockSpec((tm, d), lambda j, i: (i, 0))] + w_specs,
        out_specs=pl.BlockSpec((tm, tn), lambda j, i: (i, j)),
        out_shape=jax.ShapeDtypeStruct((m, ncols), out_dtype),
        compiler_params=_cparams(("arbitrary", "arbitrary")),
        name="proj_" + epilogue,
    )(h, *([w] * len(col0s)))


def _ssd_prompt_kernel(zx_ref, dt_ref, cw_ref, cb_ref, dtb_ref, alog_ref, dexp_ref, ng_ref, eh_ref,
                       y_ref, st_ref, xpad, y_scr, *, d, g_n, n, k_h, p):
    c = pl.program_id(1)
    L = zx_ref.shape[0]
    kp = k_h * p
    cc = cw_ref.shape[1]
    kw = cw_ref.shape[0]

    @pl.when(c == 0)
    def _():
        xpad[0:SUBLANES, :] = jnp.zeros((SUBLANES, cc), F32)
        st_ref[...] = jnp.zeros(st_ref.shape, F32)

    zx = zx_ref[...]
    z = zx[:, :d].astype(F32)
    xr_b = zx[:, d:]
    xr = xr_b.astype(F32)
    xpad[SUBLANES:2 * SUBLANES, :] = xr[0:SUBLANES, :]
    row = lax.broadcasted_iota(jnp.int32, (L, L), 0)
    col = lax.broadcasted_iota(jnp.int32, (L, L), 1)
    conv = cb_ref[...] + cw_ref[kw - 1:kw, :] * xr
    head = cb_ref[...] + cw_ref[kw - 1:kw, :] * xr[0:SUBLANES, :]
    for s in range(1, kw):
        back = _dot(jnp.where(col == row - s, 1.0, 0.0).astype(BF16), xr_b)
        conv = conv + cw_ref[kw - 1 - s:kw - s, :] * back
        head = head + cw_ref[kw - 1 - s:kw - s, :] * xpad[SUBLANES - s:2 * SUBLANES - s, :]
    conv = jnp.concatenate([head, conv[SUBLANES:, :]], axis=0)
    xpad[0:SUBLANES, :] = xr[L - SUBLANES:, :]
    xbc = _silu(conv)
    xs = xbc[:, :d]
    bm = xbc[:, d:d + g_n * n]
    cm = xbc[:, d + g_n * n:]

    dt = _softplus(dt_ref[...] + dtb_ref[...])
    a = -jnp.exp(alog_ref[...])
    dta = dt * a
    row = lax.broadcasted_iota(jnp.int32, (L, L), 0)
    col = lax.broadcasted_iota(jnp.int32, (L, L), 1)
    causal = row >= col
    tril = jnp.where(causal, 1.0, 0.0).astype(BF16)
    a_hi, a_mid, a_lo = _split3(dta)
    acum = _dot(tril, a_hi) + _dot(tril, a_mid) + _dot(tril, a_lo)
    acum_t = acum.T
    dt_t = dt.T
    alast = acum[L - 1:L, :]
    wend = jnp.exp(alast - acum) * dt
    eac = jnp.exp(acum)
    eh = eh_ref[...]
    wend_x = _dot_parts(_split2(wend), eh)
    eac_x = _dot_parts(_split2(eac), eh)
    xw = xs * wend_x
    dec = jnp.exp(alast)
    lane = lax.broadcasted_iota(jnp.int32, (L, 2 * p), 1)

    for g in range(g_n):
        bg = bm[:, g * n:(g + 1) * n].astype(BF16)
        cg = cm[:, g * n:(g + 1) * n].astype(BF16)
        cbm = _dot_nt(cg, bg)
        s_g = st_ref[g * kp:(g + 1) * kp, :]
        yoff = _dot_nt(cg, s_g.astype(BF16))
        for j in range(k_h // 2):
            h0 = g * k_h + 2 * j
            ms = []
            for h in (h0, h0 + 1):
                seg = acum[:, h:h + 1] - acum_t[h:h + 1, :]
                dcy = jnp.exp(jnp.where(causal, seg, NEG_BIG))
                ms.append((cbm * dcy * dt_t[h:h + 1, :]).astype(BF16))
            lhs = jnp.concatenate(ms, axis=1)
            xp = xs[:, h0 * p:(h0 + 2) * p]
            rhs = jnp.concatenate([jnp.where(lane < p, xp, 0.0), jnp.where(lane >= p, xp, 0.0)],
                                  axis=0).astype(BF16)
            lo = h0 * p
            y_scr[:, lo:lo + 2 * p] = (_dot(lhs, rhs)
                                       + yoff[:, 2 * j * p:(2 * j + 2) * p] * eac_x[:, lo:lo + 2 * p])
        upd = _dot(xw[:, g * kp:(g + 1) * kp].T.astype(BF16), bg)
        for hh in range(k_h):
            h = g * k_h + hh
            r0 = h * p
            dsc = jnp.broadcast_to(dec[:, h:h + 1], (p, n))
            st_ref[r0:r0 + p, :] = dsc * s_g[hh * p:(hh + 1) * p, :] + upd[hh * p:(hh + 1) * p, :]

    y = (y_scr[...] + dexp_ref[...] * xs) * _silu(z)
    ng = ng_ref[...]
    for g in range(g_n):
        yg = y[:, g * kp:(g + 1) * kp]
        ms_ = jnp.mean(yg * yg, axis=-1, keepdims=True)
        y_ref[:, g * kp:(g + 1) * kp] = (yg * lax.rsqrt(ms_ + EPS) * ng[:, g * kp:(g + 1) * kp]).astype(y_ref.dtype)


def _ssd_prompt(zx, dtraw, cw, cb, dtb, alog, dexp, ng, eh, nb, t, hp, n, g_n, k_h, p):
    d = hp
    L = min(SSD_CHUNK, t)
    nc = t // L
    cc = cw.shape[1]
    kern = functools.partial(_ssd_prompt_kernel, d=d, g_n=g_n, n=n, k_h=k_h, p=p)
    const = lambda shape: pl.BlockSpec(shape, lambda b, c: (0,) * len(shape))
    return pl.pallas_call(
        kern,
        grid=(nb, nc),
        in_specs=[
            pl.BlockSpec((L, d + cc), lambda b, c: (b * nc + c, 0)),
            pl.BlockSpec((L, LANES), lambda b, c: (b * nc + c, 0)),
            const(cw.shape), const(cb.shape), const(dtb.shape), const(alog.shape),
            const(dexp.shape), const(ng.shape), const(eh.shape),
        ],
        out_specs=[
            pl.BlockSpec((L, d), lambda b, c: (b * nc + c, 0)),
            pl.BlockSpec((None, hp, n), lambda b, c: (b, 0, 0)),
        ],
        out_shape=[
            jax.ShapeDtypeStruct((nb * t, d), BF16),
            jax.ShapeDtypeStruct((nb, hp, n), F32),
        ],
        scratch_shapes=[pltpu.VMEM((2 * SUBLANES, cc), F32), pltpu.VMEM((L, d), F32)],
        compiler_params=_cparams(("arbitrary", "arbitrary")),
        name="ssd_prompt",
    )(zx, dtraw, cw, cb, dtb, alog, dexp, ng, eh)


def _ssd_sample_pre_kernel(zx_ref, dt_ref, st_ref, cw_ref, cb_ref, dtb_ref, alog_ref, dexp_ref,
                           eh_ref, gh_ref, a_ref, ex_ref, zs_ref, xw_ref, c_ref, b_ref, dec_ref,
                           *, d, g_n, n):
    ts, bs, _ = zx_ref.shape
    kw = cw_ref.shape[0]
    gn = g_n * n
    rows = [st_ref[j] for j in range(kw - 1)]
    rows += [zx_ref[t][:, d:].astype(F32) for t in range(ts)]
    eh = eh_ref[...]
    gh = gh_ref[...]
    a = -jnp.exp(alog_ref[...])
    xs, bm, cm, dts, acs = [], [], [], [], []
    run = None
    for t in range(ts):
        conv = cb_ref[...] + cw_ref[0:1, :] * rows[t]
        for k in range(1, kw):
            conv = conv + cw_ref[k:k + 1, :] * rows[t + k]
        xbc = _silu(conv)
        xs.append(xbc[:, :d])
        bm.append(xbc[:, d:d + gn])
        cm.append(xbc[:, d + gn:])
        dtt = _softplus(dt_ref[t] + dtb_ref[...])
        dts.append(dtt)
        run = dtt * a if run is None else run + dtt * a
        acs.append(run)
    alast = acs[-1]
    dec_ref[...] = jnp.exp(alast)
    for t in range(ts):
        acc = dexp_ref[...] * xs[t]
        for j in range(t + 1):
            cbh = _dot_parts(_split2(cm[t] * bm[j]), gh)
            coef = cbh * dts[j]
            if j < t:
                coef = coef * jnp.exp(acs[t] - acs[j])
            acc = acc + _dot_parts(_split2(coef), eh) * xs[j]
        a_ref[t] = acc
        ex_ref[t] = _dot_parts(_split2(jnp.exp(acs[t])), eh)
        zs_ref[t] = _silu(zx_ref[t][:, :d].astype(F32))
        xw_ref[t] = xs[t] * _dot_parts(_split2(jnp.exp(alast - acs[t]) * dts[t]), eh)
        c_ref[t] = cm[t]
        b_ref[t] = bm[t]


def _ssd_sample_pre(zx3, dt3, tblk, ts, sconv_t, layer, cw, cb, dtb, alog, dexp, eh, gh, d, g_n, n):
    bs = zx3.shape[1]
    gn = g_n * n
    kern = functools.partial(_ssd_sample_pre_kernel, d=d, g_n=g_n, n=n)
    sb = min(SAMPLE_PRE_BLOCK, bs)
    kw = cw.shape[0]
    const = lambda a: pl.BlockSpec(a.shape, lambda i: (0,) * a.ndim)
    seq3 = lambda c: pl.BlockSpec((ts, sb, c), lambda i: (0, i, 0))
    src3 = lambda c: pl.BlockSpec((ts, sb, c), lambda i: (tblk, i, 0))
    big = jax.ShapeDtypeStruct((ts, bs, d), F32)
    small = jax.ShapeDtypeStruct((ts, bs, gn), F32)
    return pl.pallas_call(
        kern,
        grid=(bs // sb,),
        in_specs=[src3(zx3.shape[2]), src3(LANES),
                  pl.BlockSpec((None, kw - 1, sb, sconv_t.shape[3]), lambda i: (layer, 0, i, 0)),
                  const(cw), const(cb), const(dtb), const(alog), const(dexp), const(eh), const(gh)],
        out_specs=[seq3(d), seq3(d), seq3(d), seq3(d), seq3(gn), seq3(gn),
                   pl.BlockSpec((sb, LANES), lambda i: (i, 0))],
        out_shape=[big, big, big, big, small, small, jax.ShapeDtypeStruct((bs, LANES), F32)],
        compiler_params=_cparams(("arbitrary",)),
        name="ssd_sample_pre",
    )(zx3, dt3, sconv_t, cw, cb, dtb, alog, dexp, eh, gh)


def _ssd_sample_state_kernel(dec_ref, s0_ref, a_ref, ex_ref, zs_ref, xwt_ref, c_ref, b_ref, ng_ref,
                             prev_ref, y_ref, s1_ref, *, g_n, n, k_h, p, sb):
    del prev_ref
    i = pl.program_id(0)
    kp = k_h * p
    n_heads = g_n * k_h
    rows8 = sb * SUBLANES
    rid = lax.broadcasted_iota(jnp.int32, (rows8, 1), 0)
    ng = ng_ref[...]

    def body(s, carry):
        r0 = pl.multiple_of(s * SUBLANES, SUBLANES)
        in_s = (rid >= r0) & (rid < r0 + SUBLANES)
        for g in range(g_n):
            cg = c_ref[pl.ds(r0, SUBLANES), g * n:(g + 1) * n].astype(BF16)
            s_g = s0_ref[s, g * kp:(g + 1) * kp, :]
            yoff = _dot_nt(cg, s_g.astype(BF16))
            b_all = b_ref[:, g * n:(g + 1) * n]
            b_msk = jnp.where(in_s, b_all, 0.0).astype(BF16)
            upd = _dot(xwt_ref[g * kp:(g + 1) * kp, :].astype(BF16), b_msk)
            for hh in range(k_h):
                h = g * k_h + hh
                dsc = dec_ref[(i * sb + s) * n_heads + h]
                s1_ref[s, h * p:(h + 1) * p, :] = (dsc * s_g[hh * p:(hh + 1) * p, :]
                                                   + upd[hh * p:(hh + 1) * p, :])
            cols = slice(g * kp, (g + 1) * kp)
            yg = ((a_ref[pl.ds(r0, SUBLANES), cols] + ex_ref[pl.ds(r0, SUBLANES), cols] * yoff)
                  * zs_ref[pl.ds(r0, SUBLANES), cols])
            ms_ = jnp.mean(yg * yg, axis=-1, keepdims=True)
            y_ref[pl.ds(r0, SUBLANES), cols] = (yg * lax.rsqrt(ms_ + EPS) * ng[:, cols]).astype(y_ref.dtype)
        return carry

    lax.fori_loop(0, sb, body, 0)


def _ssd_sample_state(dec, s0_all, layer, s1_prev, a8, ex8, zs8, xwt, c8, b8, ng, g_n, n, k_h, p):
    depth, bs, hp, _ = s0_all.shape
    d = a8.shape[1]
    gn = g_n * n
    sb = min(SAMPLE_SEQ_BLOCK, bs)
    r8 = sb * SUBLANES
    kern = functools.partial(_ssd_sample_state_kernel, g_n=g_n, n=n, k_h=k_h, p=p, sb=sb)
    return pl.pallas_call(
        kern,
        grid=(bs // sb,),
        in_specs=[
            pl.BlockSpec(memory_space=pltpu.SMEM),
            pl.BlockSpec((None, sb, hp, n), lambda i: (layer, i, 0, 0)),
            pl.BlockSpec((r8, d), lambda i: (i, 0)),
            pl.BlockSpec((r8, d), lambda i: (i, 0)),
            pl.BlockSpec((r8, d), lambda i: (i, 0)),
            pl.BlockSpec((None, d, r8), lambda i: (i, 0, 0)),
            pl.BlockSpec((r8, gn), lambda i: (i, 0)),
            pl.BlockSpec((r8, gn), lambda i: (i, 0)),
            pl.BlockSpec((1, d), lambda i: (0, 0)),
            pl.BlockSpec(memory_space=pl.ANY),
        ],
        out_specs=[
            pl.BlockSpec((r8, d), lambda i: (i, 0)),
            pl.BlockSpec((None, sb, hp, n), lambda i: (layer, i, 0, 0)),
        ],
        out_shape=[
            jax.ShapeDtypeStruct((bs * SUBLANES, d), F32),
            jax.ShapeDtypeStruct((depth, bs, hp, n), F32),
        ],
        input_output_aliases={9: 1},
        compiler_params=_cparams(("arbitrary",)),
        name="ssd_sample_state",
    )(dec, s0_all, a8, ex8, zs8, xwt, c8, b8, ng, s1_prev)


def _ln_swish(v, g, b):
    mu = jnp.mean(v, axis=-1, keepdims=True)
    cen = v - mu
    var = jnp.mean(cen * cen, axis=-1, keepdims=True)
    return _silu(cen * lax.rsqrt(var + EPS) * g + b)


def _conf_prompt_kernel(u_ref, w_ref, b_ref, lg_ref, lb_ref, v_ref, upad, ush, pre, *, halo):
    c = pl.program_id(1)
    L, d = u_ref.shape
    kw = w_ref.shape[0]
    span = L + halo - SUBLANES

    @pl.when(c == 0)
    def _():
        upad[0:halo, :] = jnp.zeros((halo, d), F32)

    u_b = u_ref[...]
    u = u_b.astype(F32)
    upad[halo:halo + L, :] = u
    row = lax.broadcasted_iota(jnp.int32, (L, L), 0)
    col = lax.broadcasted_iota(jnp.int32, (L, L), 1)
    for r in range(1, SUBLANES):
        ush[r - 1, 0:halo, :] = upad[r:r + halo, :]
        shifted = _dot(jnp.where(col == row + r, 1.0, 0.0).astype(BF16), u_b)
        ush[r - 1, halo:span, :] = shifted[0:span - halo, :]
    for j in range(d // LANES):
        cols = slice(j * LANES, (j + 1) * LANES)
        acc = jnp.broadcast_to(b_ref[:, cols], (L, LANES))
        for k in range(kw):
            q, r = divmod(halo - (kw - 1) + k, SUBLANES)
            rows = slice(q * SUBLANES, q * SUBLANES + L)
            src = upad[rows, cols] if r == 0 else ush[r - 1, rows, cols]
            acc = acc + w_ref[k:k + 1, cols] * src
        pre[:, cols] = acc
    upad[0:halo, :] = u[L - halo:, :]
    v_ref[...] = _ln_swish(pre[...], lg_ref[...], lb_ref[...]).astype(v_ref.dtype)


def _conf_prompt(u, w, b, lg, lb, nb, t):
    assert u.dtype == BF16
    d = u.shape[1]
    kw = w.shape[0]
    L = min(CONF_CHUNK, t)
    nc = t // L
    halo = -(-(kw - 1) // SUBLANES) * SUBLANES
    const = lambda a: pl.BlockSpec(a.shape, lambda bb, c: (0,) * a.ndim)
    return pl.pallas_call(
        functools.partial(_conf_prompt_kernel, halo=halo),
        grid=(nb, nc),
        in_specs=[pl.BlockSpec((L, d), lambda bb, c: (bb * nc + c, 0)), const(w), const(b), const(lg), const(lb)],
        out_specs=pl.BlockSpec((L, d), lambda bb, c: (bb * nc + c, 0)),
        out_shape=jax.ShapeDtypeStruct((nb * t, d), BF16),
        scratch_shapes=[pltpu.VMEM((L + halo, d), F32), pltpu.VMEM((SUBLANES - 1, L + halo, d), F32),
                        pltpu.VMEM((L, d), F32)],
        compiler_params=_cparams(("arbitrary", "arbitrary")),
        name="conf_prompt",
    )(u, w, b, lg, lb)


def _conf_sample_kernel(u_ref, st_ref, w_ref, b_ref, lg_ref, lb_ref, v_ref):
    ts, sb, d = u_ref.shape
    kw = w_ref.shape[0]
    rows = [st_ref[j] for j in range(kw - 1)]
    rows += [u_ref[t].astype(F32) for t in range(ts)]
    for t in range(ts):
        acc = b_ref[...] + w_ref[0:1, :] * rows[t]
        for k in range(1, kw):
            acc = acc + w_ref[k:k + 1, :] * rows[t + k]
        v_ref[t] = _ln_swish(acc, lg_ref[...], lb_ref[...]).astype(v_ref.dtype)


def _conf_sample(u3, tblk, ts, cconv_t, layer, w, b, lg, lb):
    _, bs, d = u3.shape
    kw = w.shape[0]
    sb = min(CONF_SAMPLE_BLOCK, bs)
    const = lambda a: pl.BlockSpec(a.shape, lambda i: (0,) * a.ndim)
    return pl.pallas_call(
        _conf_sample_kernel,
        grid=(bs // sb,),
        in_specs=[
            pl.BlockSpec((ts, sb, d), lambda i: (tblk, i, 0)),
            pl.BlockSpec((None, kw - 1, sb, d), lambda i: (layer, 0, i, 0)),
            const(w), const(b), const(lg), const(lb),
        ],
        out_specs=pl.BlockSpec((ts, sb, d), lambda i: (0, i, 0)),
        out_shape=jax.ShapeDtypeStruct((ts, bs, d), BF16),
        compiler_params=_cparams(("arbitrary",)),
        name="conf_sample",
    )(u3, cconv_t, w, b, lg, lb)


def _merge_kernel(yp_ref, ys_ref, vp_ref, vs_ref, ws_ref, wc_ref, gs_ref, gc_ref, o_ref, *, npt):
    is_p = pl.program_id(1) < npt
    y = jnp.where(is_p, yp_ref[...], ys_ref[...])
    v = jnp.where(is_p, vp_ref[...], vs_ref[...])
    a = _dot(y, ws_ref[...])
    b = _dot(v, wc_ref[...])
    o_ref[...] = (gs_ref[...].astype(F32) * a + gc_ref[...].astype(F32) * b).astype(o_ref.dtype)


def _merge(y_p, y_s, v_p, v_s, ws, wc, layer, gates, tm):
    m, d = gates.shape[0], y_p.shape[1]
    npt = y_p.shape[0] // tm
    tn = _pick_tn(d)
    nj = d // tn
    p_spec = pl.BlockSpec((tm, d), lambda j, i: (jnp.minimum(i, npt - 1), 0))
    s_spec = pl.BlockSpec((tm, d), lambda j, i: (0, 0))
    return pl.pallas_call(
        functools.partial(_merge_kernel, npt=npt),
        grid=(nj, m // tm),
        in_specs=[
            p_spec, s_spec, p_spec, s_spec,
            pl.BlockSpec((None, d, tn), lambda j, i: (layer, 0, j)),
            pl.BlockSpec((None, d, tn), lambda j, i: (layer, 0, j)),
            pl.BlockSpec((tm, tn), lambda j, i: (i, j)),
            pl.BlockSpec((tm, tn), lambda j, i: (i, nj + j)),
        ],
        out_specs=pl.BlockSpec((tm, tn), lambda j, i: (i, j)),
        out_shape=jax.ShapeDtypeStruct((m, d), BF16),
        compiler_params=_cparams(("arbitrary", "arbitrary")),
        name="merge",
    )(y_p, y_s, v_p, v_s, ws, wc, gates, gates)


def _outproj_kernel(mx_ref, w_ref, x_ref, gp_ref, gs_ref, o_ref, *, npt, ts):
    gate = _tile_mod(pl.program_id(1), npt, ts, gp_ref, gs_ref)
    o_ref[...] = x_ref[...] + gate * _dot(mx_ref[...], w_ref[...])


def _outproj(mixed, wo, layer, x, modp, mods, sec, dims):
    m, d = x.shape
    tm, npt, tps, nb, bs, ts = dims
    tn = _pick_tn(d)
    nj = d // tn
    colj = lambda s, j, i: s * nj + j
    return pl.pallas_call(
        functools.partial(_outproj_kernel, npt=npt, ts=ts),
        grid=(nj, m // tm),
        in_specs=[
            pl.BlockSpec((tm, d), lambda j, i: (i, 0)),
            pl.BlockSpec((None, d, tn), lambda j, i: (layer, 0, j)),
            pl.BlockSpec((tm, tn), lambda j, i: (i, j)),
            *_mod_specs(npt, tps, nb, bs, tn, sec, colj, tile_axis=1),
        ],
        out_specs=pl.BlockSpec((tm, tn), lambda j, i: (i, j)),
        out_shape=jax.ShapeDtypeStruct((m, d), F32),
        compiler_params=_cparams(("arbitrary", "arbitrary")),
        name="outproj",
    )(mixed, wo, x, modp, mods)


def _pack_bf16_pairs(v):
    half = v.shape[1] // 2
    bits = pltpu.bitcast(v.astype(BF16).astype(F32), jnp.uint32)
    return bits[:, :half] | (bits[:, half:] >> 16)


def _unpack_bf16_pairs(w):
    left = pltpu.bitcast(w & jnp.uint32(0xFFFF0000), F32)
    right = pltpu.bitcast(w << 16, F32)
    return jnp.concatenate([left, right], axis=1)


INFO_RANK1, INFO_RANK2, INFO_E1, INFO_E2, INFO_W1, INFO_W2 = range(6)


def _router_kernel(x_ref, g_ref, scp_ref, scs_ref, shp_ref, shs_ref, whi_ref, wlo_ref, rb_ref,
                   hp_ref, info_ref, cnt_ref, *, npt, ts, n_e, n_g):
    i = pl.program_id(0)
    epg = n_e // n_g

    @pl.when(i == 0)
    def _():
        cnt_ref[...] = jnp.zeros(cnt_ref.shape, F32)

    x = x_ref[...]
    xn = x * lax.rsqrt(jnp.mean(x * x, axis=-1, keepdims=True) + EPS) * g_ref[...]
    sc = _tile_mod(i, npt, ts, scp_ref, scs_ref)
    sh = _tile_mod(i, npt, ts, shp_ref, shs_ref)
    h = xn * (1.0 + sc) + sh
    hi, lo = _split2(h)
    hp_ref[...] = _pack_bf16_pairs(h)
    logit = _dot(hi, whi_ref[...]) + _dot(lo, whi_ref[...]) + _dot(hi, wlo_ref[...]) + rb_ref[...]
    tm = logit.shape[0]
    lane = lax.broadcasted_iota(jnp.int32, (tm, LANES), 1).astype(F32)
    first = lambda hit: jnp.min(jnp.where(hit, lane, float(LANES)), axis=-1, keepdims=True)
    gl = jnp.where(lane >= n_e, jnp.where(lane < n_e + n_g, logit, NEG_BIG), NEG_BIG)
    gmax = jnp.max(gl, axis=-1, keepdims=True)
    g_top = 1.0 / jnp.sum(jnp.exp(gl - gmax), axis=-1, keepdims=True)
    g_idx = first(gl == gmax) - n_e
    el = jnp.where(lane >= g_idx * epg, jnp.where(lane < (g_idx + 1.0) * epg, logit, NEG_BIG), NEG_BIG)
    m1 = jnp.max(el, axis=-1, keepdims=True)
    i1 = first(el == m1)
    el2 = jnp.where(lane == i1, NEG_BIG, el)
    m2 = jnp.max(el2, axis=-1, keepdims=True)
    i2 = first(el2 == m2)
    r = jnp.exp(m2 - m1)
    w1 = 1.0 / (1.0 + r)
    w2 = r * w1
    oh1 = jnp.where(lane == i1, 1.0, 0.0)
    oh2 = jnp.where(lane == i2, 1.0, 0.0)
    both = oh1 + oh2
    row = lax.broadcasted_iota(jnp.int32, (tm, tm), 0)
    col = lax.broadcasted_iota(jnp.int32, (tm, tm), 1)
    before = _dot(jnp.where(row > col, 1.0, 0.0).astype(BF16), both.astype(BF16)) + cnt_ref[0:1, :]
    rank1 = jnp.sum(before * oh1, axis=-1, keepdims=True)
    rank2 = jnp.sum(before * oh2, axis=-1, keepdims=True)
    cnt_ref[...] = cnt_ref[...] + jnp.sum(both, axis=0, keepdims=True)
    rec = [rank1, rank2, i1, i2, g_top * w1, g_top * w2]
    info = jnp.zeros((tm, LANES), F32)
    for k, val in enumerate(rec):
        info = jnp.where(lane == float(k), val, info)
    info_ref[...] = info


def _router(x, gain, modp, mods, sec_sc, sec_sh, whi, wlo, rb, n_e, n_g, dims):
    m, d = x.shape
    tm, npt, tps, nb, bs, ts = dims
    col0 = lambda sec, i: sec
    return pl.pallas_call(
        functools.partial(_router_kernel, npt=npt, ts=ts, n_e=n_e, n_g=n_g),
        grid=(m // tm,),
        in_specs=[
            pl.BlockSpec((tm, d), lambda i: (i, 0)),
            pl.BlockSpec((1, d), lambda i: (0, 0)),
            *_mod_specs(npt, tps, nb, bs, d, sec_sc, col0),
            *_mod_specs(npt, tps, nb, bs, d, sec_sh, col0),
            pl.BlockSpec((d, LANES), lambda i: (0, 0)),
            pl.BlockSpec((d, LANES), lambda i: (0, 0)),
            pl.BlockSpec((1, LANES), lambda i: (0, 0)),
        ],
        out_specs=[pl.BlockSpec((tm, d // 2), lambda i: (i, 0)), pl.BlockSpec((tm, LANES), lambda i: (i, 0)),
                   pl.BlockSpec((SUBLANES, LANES), lambda i: (0, 0))],
        out_shape=[jax.ShapeDtypeStruct((m, d // 2), jnp.uint32), jax.ShapeDtypeStruct((m, LANES), F32),
                   jax.ShapeDtypeStruct((SUBLANES, LANES), F32)],
        compiler_params=_cparams(("arbitrary",)),
        name="router",
    )(x, gain, modp, mods, modp, mods, whi, wlo, rb)


def _slots_kernel(info_ref, offs_ref, pos_ref):
    info = info_ref[...]
    lane = lax.broadcasted_iota(jnp.int32, info.shape, 1).astype(F32)
    offs = offs_ref[...]
    base1 = jnp.sum(jnp.where(lane == info[:, INFO_E1:INFO_E1 + 1], offs, 0.0), axis=-1, keepdims=True)
    base2 = jnp.sum(jnp.where(lane == info[:, INFO_E2:INFO_E2 + 1], offs, 0.0), axis=-1, keepdims=True)
    rec = jnp.where(lane == 0.0, info[:, INFO_RANK1:INFO_RANK1 + 1] + base1,
                    jnp.where(lane == 1.0, info[:, INFO_RANK2:INFO_RANK2 + 1] + base2, 0.0))
    pos_ref[...] = rec.T[0:SUBLANES, :].astype(jnp.int32)


def _slots(info, offs_row, tm):
    m = info.shape[0]
    return pl.pallas_call(
        _slots_kernel,
        grid=(m // tm,),
        in_specs=[pl.BlockSpec((tm, LANES), lambda i: (i, 0)), pl.BlockSpec((1, LANES), lambda i: (0, 0))],
        out_specs=pl.BlockSpec((SUBLANES, tm), lambda i: (0, i)),
        out_shape=jax.ShapeDtypeStruct((SUBLANES, m), jnp.int32),
        compiler_params=_cparams(("arbitrary",)),
        name="moe_slots",
    )(info, offs_row)


def _moe_plan(info, cnt, n_e, tg, nt_max, tm):
    counts = cnt[0, :n_e].astype(jnp.int32)
    tiles = (counts + tg - 1) // tg
    tile_end = jnp.cumsum(tiles)
    offs = (tile_end - tiles) * tg
    n_used = tile_end[n_e - 1]
    offs_row = jnp.zeros((1, LANES), F32).at[0, :n_e].set(offs.astype(F32))
    pos = _slots(info, offs_row, tm)
    pos1, pos2 = pos[0], pos[1]
    tile = jnp.minimum(jnp.arange(nt_max, dtype=jnp.int32), n_used - 1)
    tile_expert = jnp.searchsorted(tile_end, tile, side="right").astype(jnp.int32)
    return pos1, pos2, tile_expert, n_used.reshape(1)


def _dispatch_kernel(pos1_ref, pos2_ref, hp_ref, xs0_ref, xs_ref, sem):
    del xs0_ref
    tm = hp_ref.shape[0]
    base = pl.program_id(0) * tm

    def issue(r, carry):
        src = hp_ref.at[pl.ds(r, 1)]
        pltpu.make_async_copy(src, xs_ref.at[pl.ds(pos1_ref[base + r], 1)], sem).start(priority=0)
        pltpu.make_async_copy(src, xs_ref.at[pl.ds(pos2_ref[base + r], 1)], sem).start(priority=1)
        return carry
    lax.fori_loop(0, tm, issue, 0, unroll=ROW_DMA_UNROLL)

    def drain(r, carry):
        for _ in range(2):
            pltpu.make_async_copy(hp_ref.at[pl.ds(0, 1)], xs_ref.at[pl.ds(0, 1)], sem).wait()
        return carry
    lax.fori_loop(0, tm, drain, 0, unroll=ROW_DMA_UNROLL)


def _dispatch(pos1, pos2, hp, n_slots, tm):
    m, half = hp.shape
    smem = pl.BlockSpec(memory_space=pltpu.SMEM)
    hbm = pl.BlockSpec(memory_space=pl.ANY)
    return pl.pallas_call(
        _dispatch_kernel,
        grid=(m // tm,),
        in_specs=[smem, smem, pl.BlockSpec((tm, half), lambda i: (i, 0)), hbm],
        out_specs=hbm,
        out_shape=jax.ShapeDtypeStruct((n_slots, half), jnp.uint32),
        scratch_shapes=[pltpu.SemaphoreType.DMA(())],
        input_output_aliases={3: 0},
        compiler_params=_cparams(("arbitrary",)),
        name="moe_dispatch",
    )(pos1, pos2, hp, jnp.zeros((n_slots, half), jnp.uint32))


def _experts_kernel(te_ref, nu_ref, xs_ref, wg_ref, wu_ref, wd_ref, ys_ref, wg_b, wu_b, wd_b):
    i = pl.program_id(0)

    @pl.when(jnp.logical_or(i == 0, te_ref[i] != te_ref[jnp.maximum(i - 1, 0)]))
    def _():
        wg_b[...] = wg_ref[...].astype(BF16)
        wu_b[...] = wu_ref[...].astype(BF16)
        wd_b[...] = wd_ref[...].astype(BF16)

    @pl.when(i < nu_ref[0])
    def _():
        x = _unpack_bf16_pairs(xs_ref[...]).astype(BF16)
        hid = _silu(_dot(x, wg_b[...])) * _dot(x, wu_b[...])
        ys_ref[...] = _pack_bf16_pairs(_dot(hid.astype(BF16), wd_b[...]))

    @pl.when(i >= nu_ref[0])
    def _():
        ys_ref[...] = jnp.zeros(ys_ref.shape, jnp.uint32)


def _experts(tile_expert, n_used, xs, wg, wu, wd, layer, tg):
    n_slots, half = xs.shape
    _, _, d, f = wg.shape
    grid_spec = pltpu.PrefetchScalarGridSpec(
        num_scalar_prefetch=2,
        grid=(n_slots // tg,),
        in_specs=[
            pl.BlockSpec((tg, half), lambda i, te, nu: (i, 0)),
            pl.BlockSpec((None, None, d, f), lambda i, te, nu: (layer, te[i], 0, 0)),
            pl.BlockSpec((None, None, d, f), lambda i, te, nu: (layer, te[i], 0, 0)),
            pl.BlockSpec((None, None, f, d), lambda i, te, nu: (layer, te[i], 0, 0)),
        ],
        out_specs=pl.BlockSpec((tg, half), lambda i, te, nu: (i, 0)),
        scratch_shapes=[pltpu.VMEM((d, f), BF16), pltpu.VMEM((d, f), BF16), pltpu.VMEM((f, d), BF16)],
    )
    return pl.pallas_call(
        _experts_kernel,
        grid_spec=grid_spec,
        out_shape=jax.ShapeDtypeStruct((n_slots, half), jnp.uint32),
        compiler_params=_cparams(("arbitrary",)),
        name="moe_experts",
    )(tile_expert, n_used, xs, wg, wu, wd)


def _combine_kernel(pos1_ref, pos2_ref, info_ref, x_ref, gp_ref, gs_ref, ys_ref, o_ref, buf_a, buf_b, sem,
                    *, npt, ts):
    i = pl.program_id(0)
    tm = x_ref.shape[0]
    slot = lax.rem(i, 2)

    def issue(tile, s):
        def body(r, carry):
            t = tile * tm + r
            pltpu.make_async_copy(ys_ref.at[pl.ds(pos1_ref[t], 1)], buf_a.at[s, pl.ds(r, 1)], sem.at[s]).start(priority=0)
            pltpu.make_async_copy(ys_ref.at[pl.ds(pos2_ref[t], 1)], buf_b.at[s, pl.ds(r, 1)], sem.at[s]).start(priority=1)
            return carry
        lax.fori_loop(0, tm, body, 0, unroll=ROW_DMA_UNROLL)

    @pl.when(i == 0)
    def _():
        issue(0, 0)

    @pl.when(i + 1 < pl.num_programs(0))
    def _():
        issue(i + 1, 1 - slot)

    def wait_body(r, carry):
        pltpu.make_async_copy(ys_ref.at[pl.ds(0, 1)], buf_a.at[slot, pl.ds(0, 1)], sem.at[slot]).wait()
        pltpu.make_async_copy(ys_ref.at[pl.ds(0, 1)], buf_b.at[slot, pl.ds(0, 1)], sem.at[slot]).wait()
        return carry
    lax.fori_loop(0, tm, wait_body, 0, unroll=ROW_DMA_UNROLL)

    info = info_ref[...]
    w1 = info[:, INFO_W1:INFO_W1 + 1]
    w2 = info[:, INFO_W2:INFO_W2 + 1]
    moe = w1 * _unpack_bf16_pairs(buf_a[slot]) + w2 * _unpack_bf16_pairs(buf_b[slot])
    o_ref[...] = x_ref[...] + _tile_mod(i, npt, ts, gp_ref, gs_ref) * moe


def _combine(pos1, pos2, info, x, modp, mods, sec, ys, dims):
    m, d = x.shape
    tm, npt, tps, nb, bs, ts = dims
    col0 = lambda s, i: s
    smem = pl.BlockSpec(memory_space=pltpu.SMEM)
    return pl.pallas_call(
        functools.partial(_combine_kernel, npt=npt, ts=ts),
        grid=(m // tm,),
        in_specs=[
            smem, smem,
            pl.BlockSpec((tm, LANES), lambda i: (i, 0)),
            pl.BlockSpec((tm, d), lambda i: (i, 0)),
            *_mod_specs(npt, tps, nb, bs, d, sec, col0),
            pl.BlockSpec(memory_space=pl.ANY),
        ],
        out_specs=pl.BlockSpec((tm, d), lambda i: (i, 0)),
        out_shape=jax.ShapeDtypeStruct((m, d), F32),
        scratch_shapes=[pltpu.VMEM((2, tm, d // 2), jnp.uint32), pltpu.VMEM((2, tm, d // 2), jnp.uint32),
                        pltpu.SemaphoreType.DMA((2,))],
        compiler_params=_cparams(("arbitrary",)),
        name="moe_combine",
    )(pos1, pos2, info, x, modp, mods, ys)


def _moe(hp, info, cnt, wg, wu, wd, layer, x, modp, mods, sec, dims):
    m = x.shape[0]
    n_e = wg.shape[1]
    tg = MOE_TG
    nt_max = -(-2 * m // tg) + n_e
    pos1, pos2, tile_expert, n_used = _moe_plan(info, cnt, n_e, tg, nt_max, dims[0])
    xs = _dispatch(pos1, pos2, hp, nt_max * tg, dims[0])
    ys = _experts(tile_expert, n_used, xs, wg, wu, wd, layer, tg)
    return _combine(pos1, pos2, info, x, modp, mods, sec, ys, dims)


def _final_norm_kernel(x_ref, g_ref, o_ref):
    x = x_ref[...]
    o_ref[...] = x * lax.rsqrt(jnp.mean(x * x, axis=-1, keepdims=True) + EPS) * g_ref[...]


def _final_norm(x, g, tm, tile0, ntiles):
    d = x.shape[1]
    return pl.pallas_call(
        _final_norm_kernel,
        grid=(ntiles,),
        in_specs=[pl.BlockSpec((tm, d), lambda i: (tile0 + i, 0)), pl.BlockSpec((1, d), lambda i: (0, 0))],
        out_specs=pl.BlockSpec((tm, d), lambda i: (i, 0)),
        out_shape=jax.ShapeDtypeStruct((ntiles * tm, d), F32),
        compiler_params=_cparams(("arbitrary",)),
        name="final_norm",
    )(x, g)


def _to_seq8(a):
    ts, bs, c = a.shape
    a = jnp.transpose(a, (1, 0, 2))
    a = jnp.pad(a, ((0, 0), (0, SUBLANES - ts), (0, 0)))
    return a.reshape(bs * SUBLANES, c)


def kernel(x_prompt, x_sample, c_prompt, c_sample, state_ssm, state_ssd_conv, state_conf_conv, norm_mix_g, norm_ffn_g, w_ada, b_ada, w_in, ssd_conv_w, ssd_conv_b, ssd_dt_bias, ssd_a_log, ssd_d, ssd_norm_g, w_ssd_out, conf_conv_w, conf_conv_b, conf_ln_g, conf_ln_b, w_conf_out, w_o, w_group_router, b_group_router, w_expert_router, b_expert_router, w_exp_gate, w_exp_up, w_exp_down, final_norm_g):
    nb, t, d = x_prompt.shape
    bs, ts, _ = x_sample.shape
    depth = w_in.shape[0]
    n_h = ssd_dt_bias.shape[1]
    p = d // n_h
    n = state_ssm.shape[-1]
    cc = ssd_conv_w.shape[-1]
    g_n = (cc - d) // (2 * n)
    k_h = n_h // g_n
    gn = g_n * n
    kc = conf_conv_w.shape[1]
    n_e = w_exp_gate.shape[1]
    n_g = w_group_router.shape[-1]
    tm = bs * ts
    assert t % tm == 0 and tm % SUBLANES == 0 and bs % SUBLANES == 0 and ts <= SUBLANES
    assert 2 * p == LANES and k_h % 2 == 0 and n_h <= LANES and n_e + n_g <= LANES
    npt = nb * t // tm
    dims = (tm, npt, t // tm, nb, bs, ts)
    mp = nb * t

    x = jnp.concatenate([x_prompt.reshape(mp, d), jnp.transpose(x_sample, (1, 0, 2)).reshape(tm, d)], axis=0)

    rc = -(-(nb + bs) // SUBLANES) * SUBLANES
    c_all = jnp.concatenate([c_prompt, c_sample, jnp.zeros((rc - nb - bs, d), F32)], axis=0)
    mod = _adaln(c_all, w_ada, b_ada)

    head_of = jnp.arange(d) // p
    eh = (jnp.arange(LANES)[:, None] == head_of[None, :]).astype(BF16)
    gh = ((jnp.arange(gn)[:, None] // n) == (jnp.arange(LANES)[None, :] // k_h)).astype(BF16)
    gh = gh * (jnp.arange(LANES)[None, :] < n_h).astype(BF16)
    pad_h = lambda v: jnp.pad(v, (0, LANES - n_h)).reshape(1, LANES)

    m = mp + tm
    tblk = mp // bs // ts
    assert tblk * bs * ts == mp
    sconv_t = jnp.transpose(state_ssd_conv, (0, 2, 1, 3))
    cconv_t = jnp.transpose(state_conf_conv, (0, 2, 1, 3))
    ssm0_all = state_ssm.reshape(depth, bs, n_h * p, n)
    ssm_s = jnp.zeros(ssm0_all.shape, F32)
    o_dt, o_glu = d + cc, d + cc + n_h
    w_main = w_in.astype(BF16)
    w_dt = jnp.pad(w_in[:, :, o_dt:o_glu], ((0, 0), (0, 0), (0, LANES - n_h))).astype(BF16)
    w_rest = w_main[:, :, o_glu:]
    w_so, w_co, w_oo = w_ssd_out.astype(BF16), w_conf_out.astype(BF16), w_o.astype(BF16)
    ssm_p, sconv_p, cconv_p, sconv_s, cconv_s = [], [], [], [], []
    for l in range(depth):
        modp = mod[l, :nb].reshape(nb, 1, 6 * d)
        mods = mod[l, nb:nb + bs]
        h1 = _modnorm(x, norm_mix_g[l].reshape(1, d), modp, mods, 1, 0, dims)
        zx = _proj(h1, w_main, [0], o_dt, l, BF16, "plain")
        dtraw = _proj(h1, w_dt, [0], LANES, l, F32, "plain")
        u = _proj(h1, w_rest, [0, d], d, l, BF16, "glu")
        gates = _proj(h1, w_rest, [2 * d], 2 * d, l, BF16, "sigmoid")

        cw = ssd_conv_w[l]
        cb = ssd_conv_b[l].reshape(1, cc)
        dtb = pad_h(ssd_dt_bias[l])
        alog = pad_h(ssd_a_log[l])
        dexp = jnp.repeat(ssd_d[l], p).reshape(1, d)
        ng = ssd_norm_g[l].reshape(1, d)

        y_p, s_p = _ssd_prompt(zx, dtraw, cw, cb, dtb, alog, dexp, ng, eh, nb, t, n_h * p, n, g_n, k_h, p)
        zx_s = zx[mp:].reshape(ts, bs, d + cc)
        sconv0 = state_ssd_conv[l]
        a_t, ex_t, zs_t, xw_t, c_t, b_t, dec = _ssd_sample_pre(
            zx.reshape(m // bs, bs, d + cc), dtraw.reshape(m // bs, bs, LANES), tblk, ts, sconv_t, l,
            cw, cb, dtb, alog, dexp, eh, gh, d, g_n, n)
        sb = min(SAMPLE_SEQ_BLOCK, bs)
        xwt = _to_seq8(xw_t).reshape(bs // sb, sb * SUBLANES, d).transpose(0, 2, 1)
        dec = dec[:, :n_h].reshape(bs * n_h)
        y_s8, ssm_s = _ssd_sample_state(dec, ssm0_all, l, ssm_s, _to_seq8(a_t), _to_seq8(ex_t),
                                        _to_seq8(zs_t), xwt, _to_seq8(c_t), _to_seq8(b_t), ng, g_n, n, k_h, p)
        y_s = jnp.transpose(y_s8.reshape(bs, SUBLANES, d)[:, :ts], (1, 0, 2)).reshape(tm, d).astype(BF16)

        ccw = conf_conv_w[l]
        ccb = conf_conv_b[l].reshape(1, d)
        lg = conf_ln_g[l].reshape(1, d)
        lb = conf_ln_b[l].reshape(1, d)
        v_p = _conf_prompt(u, ccw, ccb, lg, lb, nb, t)
        u_s = u[mp:].reshape(ts, bs, d)
        cconv0 = state_conf_conv[l]
        v_s = _conf_sample(u.reshape(m // bs, bs, d), tblk, ts, cconv_t, l, ccw, ccb, lg, lb).reshape(tm, d)

        mixed = _merge(y_p, y_s, v_p, v_s, w_so, w_co, l, gates, tm)
        x = _outproj(mixed, w_oo, l, x, modp, mods, 2, dims)

        w_r = jnp.concatenate([w_expert_router[l], w_group_router[l],
                               jnp.zeros((d, LANES - n_e - n_g), F32)], axis=1)
        w_r_hi = w_r.astype(BF16)
        w_r_lo = (w_r - w_r_hi.astype(F32)).astype(BF16)
        rb = jnp.concatenate([b_expert_router[l], b_group_router[l], jnp.zeros((LANES - n_e - n_g,), F32)]).reshape(1, LANES)
        hp, info, cnt = _router(x, norm_ffn_g[l].reshape(1, d), modp, mods, 4, 3, w_r_hi, w_r_lo, rb, n_e, n_g, dims)
        x = _moe(hp, info, cnt, w_exp_gate, w_exp_up, w_exp_down, l, x, modp, mods, 5, dims)

        ssm_p.append(s_p.reshape(nb, n_h, p, n))
        k3 = sconv0.shape[1]
        sconv_p.append(jnp.stack([zx[(b + 1) * t - k3:(b + 1) * t, d:] for b in range(nb)]).astype(F32))
        cconv_p.append(jnp.stack([u[(b + 1) * t - (kc - 1):(b + 1) * t] for b in range(nb)]).astype(F32))
        xraw_s = jnp.transpose(zx_s[:, :, d:], (1, 0, 2)).astype(F32)
        sconv_s.append(jnp.concatenate([sconv0, xraw_s], axis=1)[:, ts:])
        cconv_s.append(jnp.concatenate([cconv0, jnp.transpose(u_s, (1, 0, 2)).astype(F32)], axis=1)[:, ts:])

    fg = final_norm_g.reshape(1, d)
    y_prompt = _final_norm(x, fg, tm, 0, npt).reshape(nb, t, d)
    y_sample = jnp.transpose(_final_norm(x, fg, tm, npt, 1).reshape(ts, bs, d), (1, 0, 2))
    return (y_prompt, y_sample, jnp.stack(ssm_p), jnp.stack(sconv_p), jnp.stack(cconv_p),
            ssm_s.reshape(depth, bs, n_h, p, n), jnp.stack(sconv_s), jnp.stack(cconv_s))
```
